```python
import math
import jax
import jax.numpy as jnp
from jax import lax
import numpy as np

D_MODEL = 1024
BATCH = 2
SEQ = 8192
DEPTH = 2

CHUNK = 64
EPS = 1e-6

RWKV_HEAD_DIM = 64
RWKV_HEADS = D_MODEL // 128
RWKV_WIDTH = RWKV_HEADS * RWKV_HEAD_DIM
DECAY_LORA = 64
ICLR_LORA = 64
GATE_LORA = 128
RWKV_GN_EPS = 64e-5
RWKV_SPLITS = (RWKV_WIDTH, RWKV_WIDTH, RWKV_WIDTH, DECAY_LORA, ICLR_LORA, GATE_LORA)
RWKV_PROJ = 3 * RWKV_WIDTH + DECAY_LORA + ICLR_LORA + GATE_LORA

SSD_HEAD_DIM = 64
SSD_WIDTH = D_MODEL
SSD_HEADS = SSD_WIDTH // SSD_HEAD_DIM
SSD_GROUPS = 2
SSD_HEADS_PER_GROUP = SSD_HEADS // SSD_GROUPS
SSD_STATE = 128
SSD_CONV = 4
SSD_CONV_DIM = SSD_WIDTH + 2 * SSD_GROUPS * SSD_STATE

S5_GROUP_CH = 16
S5_WIDTH = D_MODEL // 2
S5_GROUPS = S5_WIDTH // S5_GROUP_CH
S5_STATE = 64

RET_HEADS = 4
RET_QK_DIM = 128
RET_V_DIM = 256
RET_QK_WIDTH = RET_HEADS * RET_QK_DIM
RET_V_WIDTH = RET_HEADS * RET_V_DIM

EVEN_SPLITS = (RWKV_PROJ, SSD_WIDTH, SSD_CONV_DIM, SSD_HEADS)
EVEN_PROJ = RWKV_PROJ + SSD_WIDTH + SSD_CONV_DIM + SSD_HEADS
EVEN_OUT = RWKV_WIDTH + SSD_WIDTH
ODD_SPLITS = (S5_WIDTH, RET_QK_WIDTH, RET_QK_WIDTH, RET_V_WIDTH, RET_V_WIDTH)
ODD_PROJ = S5_WIDTH + 2 * RET_QK_WIDTH + 2 * RET_V_WIDTH
ODD_OUT = S5_WIDTH + RET_V_WIDTH

FFN_HIDDEN = ((8 * D_MODEL // 3 + 127) // 128) * 128
FFN_CONV = 3

kernel_name = "chunk_causal_hybrid_rwkv7_ssd_s5_retnet"


def split_cols(h, sizes):
    idx = np.cumsum(sizes)[:-1].tolist()
    return jnp.split(h, idx, axis=-1)


def rms_norm(x, g):
    xf = x.astype(jnp.float32)
    y = xf * lax.rsqrt(jnp.mean(xf * xf, axis=-1, keepdims=True) + EPS)
    return (y * g.astype(jnp.float32)).astype(x.dtype)


def causal_dwconv(x, w, b):
    K, C = w.shape
    y = lax.conv_general_dilated(
        x, w[:, None, :].astype(x.dtype), window_strides=(1,), padding=[(K - 1, 0)],
        dimension_numbers=("NWC", "WIO", "NWC"), feature_group_count=C)
    return y + b.astype(x.dtype)


def token_shift(x):
    return jnp.pad(x, ((0, 0), (1, 0), (0, 0)))[:, :-1]


def exclusive_chunk_scan(states, decay):
    def step(h, inp):
        s, d = inp
        return h * d + s, h
    h0 = jnp.zeros_like(states[:, 0])
    _, prev = lax.scan(step, h0, (jnp.moveaxis(states, 1, 0), jnp.moveaxis(decay, 1, 0)))
    return jnp.moveaxis(prev, 0, 1)


def rwkv7_recurrence(r, decay, k, v, kk, a):
    Bsz, L, H, N = r.shape

    def step(S, inp):
        r_t, w_t, k_t, v_t, kk_t, a_t = inp
        sa = jnp.einsum("bhij,bhj->bhi", S, -kk_t)
        S = (S * w_t[:, :, None, :] + sa[..., None] * (kk_t * a_t)[:, :, None, :]
             + v_t[..., None] * k_t[:, :, None, :])
        return S, jnp.einsum("bhij,bhj->bhi", S, r_t)

    S0 = jnp.zeros((Bsz, H, N, N), jnp.float32)
    xs = tuple(jnp.moveaxis(t, 1, 0) for t in (r, decay, k, v, kk, a))
    _, y = lax.scan(step, S0, xs)
    return jnp.moveaxis(y, 0, 1)


def rwkv7_group(p, mu, w0, w2, a0, a2, g2, k_k, k_a, r_k, ln_g, ln_b):
    mu, w0, w2, a0, a2, g2, k_k, k_a, r_k, ln_g, ln_b = (
        t.astype(jnp.float32) for t in (mu, w0, w2, a0, a2, g2, k_k, k_a, r_k, ln_g, ln_b))
    Bsz, L, _ = p.shape
    H, N = RWKV_HEADS, RWKV_HEAD_DIM
    p = p + (token_shift(p) - p) * mu
    r, k, v, z_w, z_a, z_g = split_cols(p, RWKV_SPLITS)
    w_log = -jax.nn.softplus(-(w0 + jnp.tanh(z_w) @ w2)) - 0.5
    decay = jnp.exp(-jnp.exp(w_log))
    a = jax.nn.sigmoid(a0 + z_a @ a2)
    g = jax.nn.sigmoid(z_g) @ g2
    kk = (k * k_k).reshape(Bsz, L, H, N)
    kk = kk * lax.rsqrt(jnp.maximum(jnp.sum(kk * kk, axis=-1, keepdims=True), 1e-24))
    k = k * (1.0 + (a - 1.0) * k_a)
    heads = lambda t: t.reshape(Bsz, L, H, N)
    rh, kh, vh, ah, dh = heads(r), heads(k), heads(v), heads(a), heads(decay)
    y = rwkv7_recurrence(rh, dh, kh, vh, kk, ah)
    mean = jnp.mean(y, axis=-1, keepdims=True)
    var = jnp.mean(jnp.square(y - mean), axis=-1, keepdims=True)
    y = ((y - mean) * lax.rsqrt(var + RWKV_GN_EPS)).reshape(Bsz, L, RWKV_WIDTH) * ln_g + ln_b
    y = y + (jnp.sum(rh * kh * r_k, axis=-1, keepdims=True) * vh).reshape(Bsz, L, RWKV_WIDTH)
    return y * g


def ssd_chunked(x, a, bm, cm):
    Bsz, L, G, J, P = x.shape
    N = bm.shape[-1]
    nc = L // CHUNK
    x = x.reshape(Bsz, nc, CHUNK, G, J, P)
    a = a.reshape(Bsz, nc, CHUNK, G, J)
    bm = bm.reshape(Bsz, nc, CHUNK, G, N)
    cm = cm.reshape(Bsz, nc, CHUNK, G, N)
    a_cum = jnp.cumsum(a, axis=2)
    causal = jnp.tril(jnp.ones((CHUNK, CHUNK), dtype=bool))[:, :, None, None]
    seg = a_cum[:, :, :, None] - a_cum[:, :, None, :]
    intra = jnp.exp(jnp.where(causal, seg, -jnp.inf))
    cb = jnp.einsum("bclgn,bcsgn->bclsg", cm, bm)
    y_diag = jnp.einsum("bclsg,bclsgj,bcsgjp->bclgjp", cb, intra, x)
    decay_to_end = jnp.exp(a_cum[:, :, -1:] - a_cum)
    states = jnp.einsum("bclgn,bclgj,bclgjp->bcgjpn", bm, decay_to_end, x)
    chunk_decay = jnp.exp(a_cum[:, :, -1])[..., None, None]
    prev = exclusive_chunk_scan(states, chunk_decay)
    y_off = jnp.einsum("bclgn,bcgjpn,bclgj->bclgjp", cm, prev, jnp.exp(a_cum))
    return (y_diag + y_off).reshape(Bsz, L, G, J, P)


def ssd_group(z, xbc, dt_raw, conv_w, conv_b, dt_bias, a_log, d_skip, norm_g):
    conv_w, conv_b, dt_bias, a_log, d_skip, norm_g = (
        t.astype(jnp.float32) for t in (conv_w, conv_b, dt_bias, a_log, d_skip, norm_g))
    Bsz, L, _ = z.shape
    G, J, P, N = SSD_GROUPS, SSD_HEADS_PER_GROUP, SSD_HEAD_DIM, SSD_STATE
    xbc = jax.nn.silu(causal_dwconv(xbc, conv_w, conv_b))
    xs, bm, cm = split_cols(xbc, (SSD_WIDTH, G * N, G * N))
    xs = xs.reshape(Bsz, L, G, J, P)
    bm = bm.reshape(Bsz, L, G, N)
    cm = cm.reshape(Bsz, L, G, N)
    dt = jax.nn.softplus(dt_raw + dt_bias).reshape(Bsz, L, G, J)
    A = -jnp.exp(a_log).reshape(G, J)
    y = ssd_chunked(xs * dt[..., None], dt * A, bm, cm) + d_skip.reshape(G, J)[..., None] * xs
    y = y.reshape(Bsz, L, G, J * P) * jax.nn.silu(z.reshape(Bsz, L, G, J * P))
    y = y * lax.rsqrt(jnp.mean(y * y, axis=-1, keepdims=True) + EPS)
    return y.reshape(Bsz, L, SSD_WIDTH) * norm_g


def rwkv_ssd_mixer(h, w_in, rwkv_mu, rwkv_w0, rwkv_w2, rwkv_a0, rwkv_a2, rwkv_g2, rwkv_k_k,
                   rwkv_k_a, rwkv_r_k, rwkv_ln_g, rwkv_ln_b, ssd_conv_w, ssd_conv_b,
                   ssd_dt_bias, ssd_a_log, ssd_d, ssd_norm_g, w_out):
    proj = (h @ w_in).astype(jnp.float32)
    p_rwkv, p_z, p_xbc, p_dt = split_cols(proj, EVEN_SPLITS)
    y_a = rwkv7_group(p_rwkv, rwkv_mu, rwkv_w0, rwkv_w2, rwkv_a0, rwkv_a2, rwkv_g2,
                      rwkv_k_k, rwkv_k_a, rwkv_r_k, rwkv_ln_g, rwkv_ln_b)
    y_b = ssd_group(p_z, p_xbc, p_dt, ssd_conv_w, ssd_conv_b, ssd_dt_bias, ssd_a_log,
                    ssd_d, ssd_norm_g)
    y = jnp.concatenate([y_a, y_b], axis=-1).astype(h.dtype)
    return y @ w_out


def s5_group(u, lam_re, lam_im, log_dt, b_re, b_im, c_re, c_im, d_skip, w_glu, b_glu):
    lam_re, lam_im, log_dt, b_re, b_im, c_re, c_im, d_skip, w_glu, b_glu = (
        t.astype(jnp.float32) for t in
        (lam_re, lam_im, log_dt, b_re, b_im, c_re, c_im, d_skip, w_glu, b_glu))
    Bsz, L, _ = u.shape
    G, H = S5_GROUPS, S5_GROUP_CH
    uf = u.reshape(Bsz, L, G, H)
    lam = lax.complex(lam_re, lam_im)
    lam_bar = jnp.exp(lam * jnp.exp(log_dt)[:, None])
    b_bar = ((lam_bar - 1.0) / lam)[..., None] * lax.complex(b_re, b_im)
    bu = jnp.einsum("gph,blgh->blgp", b_bar, uf)
    a_el = jnp.broadcast_to(lam_bar, bu.shape)

    def combine(e1, e2):
        a1, b1 = e1
        a2, b2 = e2
        return a2 * a1, a2 * b1 + b2

    _, states = lax.associative_scan(combine, (a_el, bu), axis=1)
    c = lax.complex(c_re, c_im)
    y = jnp.einsum("ghp,blgp->blgh", c, states).real + d_skip.reshape(G, H) * uf
    y = jax.nn.gelu(y.reshape(Bsz, L, S5_WIDTH))
    return y * jax.nn.sigmoid(y @ w_glu + b_glu)


def rotary(x, pos):
    half = x.shape[-1] // 2
    theta = 1.0 / (10000.0 ** jnp.linspace(0.0, 1.0, half, dtype=jnp.float32))
    ang = pos.astype(jnp.float32)[:, None] * theta[None, :]
    cos = jnp.cos(ang)[None, :, None, :]
    sin = jnp.sin(ang)[None, :, None, :]
    x1, x2 = x[..., :half], x[..., half:]
    return jnp.concatenate([x1 * cos - x2 * sin, x1 * sin + x2 * cos], axis=-1)


def retention_group(q, k, v, g):
    Bsz, L, _ = q.shape
    H, DK, DV = RET_HEADS, RET_QK_DIM, RET_V_DIM
    nc = L // CHUNK
    pos = jnp.arange(L)
    q = rotary(q.reshape(Bsz, L, H, DK), pos)
    k = rotary(k.reshape(Bsz, L, H, DK), pos) * (DK ** -0.5)
    log_gamma = jnp.log(1.0 - 2.0 ** (-5.0 - jnp.arange(H, dtype=jnp.float32)))
    qc = q.reshape(Bsz, nc, CHUNK, H, DK)
    kc = k.reshape(Bsz, nc, CHUNK, H, DK)
    vc = v.reshape(Bsz, nc, CHUNK, H, DV)
    t = jnp.arange(CHUNK, dtype=jnp.float32)
    diff = t[:, None] - t[None, :]
    intra = jnp.where((diff >= 0)[..., None],
                      jnp.exp(jnp.maximum(diff, 0.0)[..., None] * log_gamma), 0.0)
    scores = jnp.einsum("bclhd,bcshd->bclsh", qc, kc) * intra
    y = jnp.einsum("bclsh,bcshe->bclhe", scores, vc)
    k_dec = jnp.exp((CHUNK - 1.0 - t)[:, None] * log_gamma)
    states = jnp.einsum("bcshd,sh,bcshe->bchde", kc, k_dec, vc)
    chunk_decay = jnp.broadcast_to(jnp.exp(CHUNK * log_gamma)[:, None, None], (Bsz, nc, H, 1, 1))
    prev = exclusive_chunk_scan(states, chunk_decay)
    q_dec = jnp.exp((t + 1.0)[:, None] * log_gamma)
    y = y + jnp.einsum("bclhd,lh,bchde->bclhe", qc, q_dec, prev)
    y = y.reshape(Bsz, L, H, DV)
    y = y * lax.rsqrt(jnp.mean(y * y, axis=-1, keepdims=True) + EPS)
    return jax.nn.silu(g) * y.reshape(Bsz, L, RET_V_WIDTH)


def s5_retention_mixer(h, w_in, lam_re, lam_im, log_dt, b_re, b_im, c_re, c_im, d_skip,
                       w_glu, b_glu, w_out):
    proj = (h @ w_in).astype(jnp.float32)
    u, q, k, v, g = split_cols(proj, ODD_SPLITS)
    y_c = s5_group(u, lam_re, lam_im, log_dt, b_re, b_im, c_re, c_im, d_skip, w_glu, b_glu)
    y_d = retention_group(q, k, v, g)
    y = jnp.concatenate([y_c, y_d], axis=-1).astype(h.dtype)
    return y @ w_out


def conv_ffn(h, w_up, conv_w, conv_b, w_down):
    u = causal_dwconv(h @ w_up, conv_w, conv_b)
    gate, val = jnp.split(u, 2, axis=-1)
    return (jax.nn.silu(gate) * val) @ w_down


def setup_inputs(seed: int = 0) -> dict:
    key = jax.random.key(seed)
    ks = iter(jax.random.split(key, 96))
    f32 = jnp.float32

    def nrm(shape, scale):
        return scale * jax.random.normal(next(ks), shape, f32)

    def gain(n):
        return 1.0 + nrm((n,), 0.02)

    def unif(shape, lo, hi):
        return jax.random.uniform(next(ks), shape, f32, lo, hi)

    D = D_MODEL
    F2 = 2 * FFN_HIDDEN
    inp = {}
    inp["x"] = nrm((BATCH, SEQ, D), 1.0)
    inp["l0_norm_mix"] = gain(D)
    inp["l0_w_in"] = nrm((D, EVEN_PROJ), D ** -0.5)
    inp["l0_rwkv_mu"] = unif((RWKV_PROJ,), 0.0, 1.0)
    ramp = jnp.arange(RWKV_WIDTH, dtype=f32) / (RWKV_WIDTH - 1)
    inp["l0_rwkv_w0"] = -6.0 + 5.0 * ramp ** 0.85 + 0.5 + nrm((RWKV_WIDTH,), 0.01)
    inp["l0_rwkv_w2"] = nrm((DECAY_LORA, RWKV_WIDTH), 0.1 * DECAY_LORA ** -0.5)
    inp["l0_rwkv_a0"] = nrm((RWKV_WIDTH,), 0.1)
    inp["l0_rwkv_a2"] = nrm((ICLR_LORA, RWKV_WIDTH), 0.1 * ICLR_LORA ** -0.5)
    inp["l0_rwkv_g2"] = nrm((GATE_LORA, RWKV_WIDTH), GATE_LORA ** -0.5)
    inp["l0_rwkv_k_k"] = 0.85 + nrm((RWKV_WIDTH,), 0.02)
    inp["l0_rwkv_k_a"] = 1.0 + nrm((RWKV_WIDTH,), 0.02)
    inp["l0_rwkv_r_k"] = nrm((RWKV_HEADS, RWKV_HEAD_DIM), 0.1)
    inp["l0_rwkv_ln_g"] = gain(RWKV_WIDTH)
    inp["l0_rwkv_ln_b"] = nrm((RWKV_WIDTH,), 0.02)
    inp["l0_ssd_conv_w"] = nrm((SSD_CONV, SSD_CONV_DIM), SSD_CONV ** -0.5)
    inp["l0_ssd_conv_b"] = nrm((SSD_CONV_DIM,), 0.02)
    dt0 = jnp.exp(unif((SSD_HEADS,), math.log(1e-3), math.log(1e-1)))
    inp["l0_ssd_dt_bias"] = dt0 + jnp.log(-jnp.expm1(-dt0))
    inp["l0_ssd_a_log"] = jnp.log(unif((SSD_HEADS,), 1.0, 16.0))
    inp["l0_ssd_d"] = gain(SSD_HEADS)
    inp["l0_ssd_norm_g"] = gain(SSD_WIDTH)
    inp["l0_w_out"] = nrm((EVEN_OUT, D), EVEN_OUT ** -0.5)
    inp["l0_norm_ffn"] = gain(D)
    inp["l0_ffn_up"] = nrm((D, F2), D ** -0.5)
    inp["l0_ffn_conv_w"] = nrm((FFN_CONV, F2), FFN_CONV ** -0.5)
    inp["l0_ffn_conv_b"] = nrm((F2,), 0.02)
    inp["l0_ffn_down"] = nrm((FFN_HIDDEN, D), FFN_HIDDEN ** -0.5)
    inp["l1_norm_mix"] = gain(D)
    inp["l1_w_in"] = nrm((D, ODD_PROJ), D ** -0.5)
    inp["l1_s5_lam_re"] = -0.5 + nrm((S5_GROUPS, S5_STATE), 0.01)
    inp["l1_s5_lam_im"] = (math.pi * jnp.arange(S5_STATE, dtype=f32))[None, :] + nrm((S5_GROUPS, S5_STATE), 0.01)
    inp["l1_s5_log_dt"] = unif((S5_GROUPS,), math.log(1e-3), math.log(1e-1))
    inp["l1_s5_b_re"] = nrm((S5_GROUPS, S5_STATE, S5_GROUP_CH), (2 * S5_GROUP_CH) ** -0.5)
    inp["l1_s5_b_im"] = nrm((S5_GROUPS, S5_STATE, S5_GROUP_CH), (2 * S5_GROUP_CH) ** -0.5)
    inp["l1_s5_c_re"] = nrm((S5_GROUPS, S5_GROUP_CH, S5_STATE), S5_STATE ** -0.5)
    inp["l1_s5_c_im"] = nrm((S5_GROUPS, S5_GROUP_CH, S5_STATE), S5_STATE ** -0.5)
    inp["l1_s5_d"] = nrm((S5_WIDTH,), 1.0)
    inp["l1_s5_w_glu"] = nrm((S5_WIDTH, S5_WIDTH), S5_WIDTH ** -0.5)
    inp["l1_s5_b_glu"] = nrm((S5_WIDTH,), 0.02)
    inp["l1_w_out"] = nrm((ODD_OUT, D), ODD_OUT ** -0.5)
    inp["l1_norm_ffn"] = gain(D)
    inp["l1_ffn_up"] = nrm((D, F2), D ** -0.5)
    inp["l1_ffn_conv_w"] = nrm((FFN_CONV, F2), FFN_CONV ** -0.5)
    inp["l1_ffn_conv_b"] = nrm((F2,), 0.02)
    inp["l1_ffn_down"] = nrm((FFN_HIDDEN, D), FFN_HIDDEN ** -0.5)
    inp["final_norm"] = gain(D)
    return inp


def reference(x,
              l0_norm_mix, l0_w_in, l0_rwkv_mu, l0_rwkv_w0, l0_rwkv_w2, l0_rwkv_a0, l0_rwkv_a2,
              l0_rwkv_g2, l0_rwkv_k_k, l0_rwkv_k_a, l0_rwkv_r_k, l0_rwkv_ln_g, l0_rwkv_ln_b,
              l0_ssd_conv_w, l0_ssd_conv_b, l0_ssd_dt_bias, l0_ssd_a_log, l0_ssd_d, l0_ssd_norm_g,
              l0_w_out, l0_norm_ffn, l0_ffn_up, l0_ffn_conv_w, l0_ffn_conv_b, l0_ffn_down,
              l1_norm_mix, l1_w_in, l1_s5_lam_re, l1_s5_lam_im, l1_s5_log_dt, l1_s5_b_re,
              l1_s5_b_im, l1_s5_c_re, l1_s5_c_im, l1_s5_d, l1_s5_w_glu, l1_s5_b_glu, l1_w_out,
              l1_norm_ffn, l1_ffn_up, l1_ffn_conv_w, l1_ffn_conv_b, l1_ffn_down,
              final_norm):
    mixers = (
        (rwkv_ssd_mixer, (l0_w_in, l0_rwkv_mu, l0_rwkv_w0, l0_rwkv_w2, l0_rwkv_a0, l0_rwkv_a2,
                          l0_rwkv_g2, l0_rwkv_k_k, l0_rwkv_k_a, l0_rwkv_r_k, l0_rwkv_ln_g,
                          l0_rwkv_ln_b, l0_ssd_conv_w, l0_ssd_conv_b, l0_ssd_dt_bias,
                          l0_ssd_a_log, l0_ssd_d, l0_ssd_norm_g, l0_w_out)),
        (s5_retention_mixer, (l1_w_in, l1_s5_lam_re, l1_s5_lam_im, l1_s5_log_dt, l1_s5_b_re,
                              l1_s5_b_im, l1_s5_c_re, l1_s5_c_im, l1_s5_d, l1_s5_w_glu,
                              l1_s5_b_glu, l1_w_out)),
    )
    mix_norms = (l0_norm_mix, l1_norm_mix)
    ffn_norms = (l0_norm_ffn, l1_norm_ffn)
    ffns = ((l0_ffn_up, l0_ffn_conv_w, l0_ffn_conv_b, l0_ffn_down),
            (l1_ffn_up, l1_ffn_conv_w, l1_ffn_conv_b, l1_ffn_down))
    for i in range(DEPTH):
        mixer, params = mixers[i]
        x = x + mixer(rms_norm(x, mix_norms[i]), *params)
        x = x + conv_ffn(rms_norm(x, ffn_norms[i]), *ffns[i])
    return rms_norm(x, final_norm)
```

```python
import functools
import math

import numpy as np
import jax
import jax.numpy as jnp
from jax import lax
from jax.experimental import pallas as pl
from jax.experimental.pallas import tpu as pltpu

F32 = jnp.float32
BF16 = jnp.bfloat16

EPS = 1e-6
D_MODEL = 1024

RWKV_HEADS = 8
RWKV_N = 64
RWKV_W = 512
RWKV_GN_EPS = 64e-5
RWKV_CHUNK = 64

SSD_HEADS = 16
SSD_P = 64
SSD_GROUPS = 2
SSD_N = 128
SSD_W = 1024
SSD_CONV = 4
SSD_CHUNK = 64

S5_W = 512
S5_GROUPS = 32
S5_H = 16
S5_P = 64
S5_LC = 8
S5_TILES = 4
S5_ROWS = 64

RET_HEADS = 4
RET_DK = 128
RET_DV = 256
RET_CHUNK = 256

FFN_HIDDEN = 2816
FFN_TN = 256
FFN_HALO = 16

VMEM_LIMIT = 48 * 1024 * 1024


def _dot(a, b):
    return jnp.dot(a, b, preferred_element_type=F32)


def _dot_nt(a, b):
    return lax.dot_general(a, b, (((1,), (1,)), ((), ())), preferred_element_type=F32)


def _dot_tn(a, b):
    return lax.dot_general(a, b, (((0,), (0,)), ((), ())), preferred_element_type=F32)


def _split3(x):
    hi = x.astype(BF16)
    r1 = x - hi.astype(F32)
    mid = r1.astype(BF16)
    lo = (r1 - mid.astype(F32)).astype(BF16)
    return hi, mid, lo


def _dot_sel_rhs(x, sel):
    hi, mid, lo = _split3(x)
    return _dot(hi, sel) + _dot(mid, sel) + _dot(lo, sel)


def _dot_sel_lhs(sel, x):
    hi, mid, lo = _split3(x)
    return _dot(sel, hi) + _dot(sel, mid) + _dot(sel, lo)


def _sigmoid(x):
    return 1.0 / (1.0 + jnp.exp(-x))


def _silu(x):
    return x * _sigmoid(x)


def _softplus(x):
    return jnp.maximum(x, 0.0) + jnp.log(1.0 + jnp.exp(-jnp.abs(x)))


def _cparams(sem):
    return pltpu.CompilerParams(dimension_semantics=sem, vmem_limit_bytes=VMEM_LIMIT)


def _proj_kernel(x_ref, g_ref, w_ref, o_ref):
    x = x_ref[...]
    y = x * lax.rsqrt(jnp.mean(x * x, axis=-1, keepdims=True) + EPS)
    hn = (y * g_ref[...]).astype(BF16)
    o_ref[...] = _dot(hn, w_ref[...])


def _norm_proj(x2, g, w, tm=256):
    T, D = x2.shape
    N = w.shape[1]
    return pl.pallas_call(
        _proj_kernel,
        grid=(T // tm,),
        in_specs=[pl.BlockSpec((tm, D), lambda i: (i, 0)),
                  pl.BlockSpec((1, D), lambda i: (0, 0)),
                  pl.BlockSpec((D, N), lambda i: (0, 0))],
        out_specs=pl.BlockSpec((tm, N), lambda i: (i, 0)),
        out_shape=jax.ShapeDtypeStruct((T, N), F32),
        compiler_params=_cparams(("parallel",)),
    )(x2, g.reshape(1, D), w)


def _outproj_kernel(x_ref, ya_ref, yb_ref, wa_ref, wb_ref, o_ref):
    o_ref[...] = x_ref[...] + _dot(ya_ref[...], wa_ref[...]) + _dot(yb_ref[...], wb_ref[...])


def _out_proj(x2, ya, yb, wa, wb, tm=512):
    T, D = x2.shape
    Ka, Kb = ya.shape[1], yb.shape[1]
    return pl.pallas_call(
        _outproj_kernel,
        grid=(T // tm,),
        in_specs=[pl.BlockSpec((tm, D), lambda i: (i, 0)),
                  pl.BlockSpec((tm, Ka), lambda i: (i, 0)),
                  pl.BlockSpec((tm, Kb), lambda i: (i, 0)),
                  pl.BlockSpec((Ka, D), lambda i: (0, 0)),
                  pl.BlockSpec((Kb, D), lambda i: (0, 0))],
        out_specs=pl.BlockSpec((tm, D), lambda i: (i, 0)),
        out_shape=jax.ShapeDtypeStruct((T, D), F32),
        compiler_params=_cparams(("parallel",)),
    )(x2, ya, yb, wa, wb)


def _ffn_kernel(x_ref, xh_ref, g_ref, wg_ref, wv_ref, cwg_ref, cwv_ref, cbg_ref, cbv_ref, wd_ref,
                gf_ref, o_ref, hn_scr, acc_scr, *, blocks_per_seq, final_norm):
    i = pl.program_id(0)
    j = pl.program_id(1)
    tm = x_ref.shape[0]

    def norm(x):
        y = x * lax.rsqrt(jnp.mean(x * x, axis=-1, keepdims=True) + EPS)
        return (y * g_ref[...]).astype(BF16)

    @pl.when(j == 0)
    def _():
        keep = jnp.where(i % blocks_per_seq == 0, 0.0, 1.0)
        hn_scr[0:FFN_HALO, :] = norm(xh_ref[...] * keep)
        hn_scr[FFN_HALO:, :] = norm(x_ref[...])
        acc_scr[...] = jnp.zeros_like(acc_scr)

    hn = hn_scr[...]

    def conv(u, cw_ref, cb_ref):
        u1 = pltpu.roll(u, 1, 0)
        u2 = pltpu.roll(u, 2, 0)
        c = cw_ref[2:3, :] * u + cw_ref[1:2, :] * u1 + cw_ref[0:1, :] * u2 + cb_ref[...]
        return c[FFN_HALO:, :]

    cg = conv(_dot(hn, wg_ref[...]), cwg_ref, cbg_ref)
    cv = conv(_dot(hn, wv_ref[...]), cwv_ref, cbv_ref)
    act = (_silu(cg) * cv).astype(BF16)
    acc_scr[...] += _dot(act, wd_ref[...])

    @pl.when(j == pl.num_programs(1) - 1)
    def _():
        y = x_ref[...] + acc_scr[...]
        if final_norm:
            y = y * lax.rsqrt(jnp.mean(y * y, axis=-1, keepdims=True) + EPS) * gf_ref[...]
        o_ref[...] = y


def _conv_ffn(x2, seq_len, g, w_up, conv_w, conv_b, w_down, g_final, final_norm, tm=512):
    T, D = x2.shape
    Hd = w_down.shape[0]
    tn = FFN_TN
    nH = Hd // tn
    tm = min(tm, seq_len)
    hb = tm // FFN_HALO
    kern = functools.partial(_ffn_kernel, blocks_per_seq=seq_len // tm, final_norm=final_norm)
    return pl.pallas_call(
        kern,
        grid=(T // tm, nH),
        in_specs=[pl.BlockSpec((tm, D), lambda i, j: (i, 0)),
                  pl.BlockSpec((FFN_HALO, D), lambda i, j: (jnp.maximum(i * hb - 1, 0), 0)),
                  pl.BlockSpec((1, D), lambda i, j: (0, 0)),
                  pl.BlockSpec((D, tn), lambda i, j: (0, j)),
                  pl.BlockSpec((D, tn), lambda i, j: (0, j + nH)),
                  pl.BlockSpec((3, tn), lambda i, j: (0, j)),
                  pl.BlockSpec((3, tn), lambda i, j: (0, j + nH)),
                  pl.BlockSpec((1, tn), lambda i, j: (0, j)),
                  pl.BlockSpec((1, tn), lambda i, j: (0, j + nH)),
                  pl.BlockSpec((tn, D), lambda i, j: (j, 0)),
                  pl.BlockSpec((1, D), lambda i, j: (0, 0))],
        out_specs=pl.BlockSpec((tm, D), lambda i, j: (i, 0)),
        out_shape=jax.ShapeDtypeStruct((T, D), F32),
        scratch_shapes=[pltpu.VMEM((tm + FFN_HALO, D), BF16), pltpu.VMEM((tm, D), F32)],
        compiler_params=_cparams(("parallel", "arbitrary")),
    )(x2, x2, g.reshape(1, D), w_up, w_up, conv_w, conv_w, conv_b.reshape(1, -1),
      conv_b.reshape(1, -1), w_down, g_final.reshape(1, D))


def _rwkv_kernel(r_ref, k_ref, v_ref, zwa_ref, zg_ref, vec_ref, vec2_ref, w2a2_ref, g2_ref,
                 eseg_ref, ltri_ref, o_ref, h_scr, prev_scr, prev2_scr, y_scr):
    c = pl.program_id(1)
    C = r_ref.shape[0]
    N = RWKV_N

    @pl.when(c == 0)
    def _():
        h_scr[...] = jnp.zeros_like(h_scr)
        prev_scr[...] = jnp.zeros_like(prev_scr)
        prev2_scr[...] = jnp.zeros_like(prev2_scr)

    row = lax.broadcasted_iota(jnp.int32, (C, 1), 0)

    def shift_lerp(x, prev_row, mu):
        xs = jnp.where(row == 0, prev_row, pltpu.roll(x, 1, 0))
        return x + (xs - x) * mu

    r_raw, k_raw, v_raw = r_ref[...], k_ref[...], v_ref[...]
    zwa_raw, zg_raw = zwa_ref[...], zg_ref[...]
    r = shift_lerp(r_raw, prev_scr[0:1, :], vec_ref[0:1, :])
    k = shift_lerp(k_raw, prev_scr[1:2, :], vec_ref[1:2, :])
    v = shift_lerp(v_raw, prev_scr[2:3, :], vec_ref[2:3, :])
    zwa = shift_lerp(zwa_raw, prev2_scr[0:1, :], vec2_ref[0:1, :])
    zg = shift_lerp(zg_raw, prev2_scr[1:2, :], vec2_ref[1:2, :])
    prev_scr[0:1, :] = r_raw[C - 1:C, :]
    prev_scr[1:2, :] = k_raw[C - 1:C, :]
    prev_scr[2:3, :] = v_raw[C - 1:C, :]
    prev2_scr[0:1, :] = zwa_raw[C - 1:C, :]
    prev2_scr[1:2, :] = zg_raw[C - 1:C, :]

    w0, a0 = vec_ref[3:4, :], vec_ref[4:5, :]
    k_k, k_a, r_k = vec_ref[5:6, :], vec_ref[6:7, :], vec_ref[7:8, :]
    ln_g, ln_b = vec_ref[8:9, :], vec_ref[9:10, :]

    lane = lax.broadcasted_iota(jnp.int32, (1, 128), 1)
    lora_in = jnp.where(lane < 64, jnp.tanh(zwa), zwa).astype(BF16)
    wa = _dot(lora_in, w2a2_ref[...])
    w_log = -_softplus(-(w0 + wa[:, :RWKV_W])) - 0.5
    logw = -jnp.exp(w_log)
    a = _sigmoid(a0 + wa[:, RWKV_W:])
    g = _dot(_sigmoid(zg).astype(BF16), g2_ref[...])

    eseg = eseg_ref[...]

    def segsum(x):
        hi = x.astype(BF16)
        lo = (x - hi.astype(F32)).astype(BF16)
        return _dot(hi, eseg) + _dot(lo, eseg)

    kk = k * k_k
    kk = kk * lax.rsqrt(jnp.maximum(segsum(kk * kk), 1e-24))
    k2 = k * (1.0 + (a - 1.0) * k_a)

    lg = _dot_sel_lhs(ltri_ref[...], logw)
    eg = jnp.exp(lg)
    egi = jnp.exp(-lg)
    lg_end = lg[C - 1:C, :]
    g_end = jnp.exp(lg_end)
    d_end = jnp.exp(lg_end - lg)
    kka = kk * a
    ab = -kk * jnp.exp(lg - logw)
    rb = r * eg
    bt = kka * egi
    kt = k2 * egi
    bg = kka * d_end
    kg = k2 * d_end

    ri = lax.broadcasted_iota(jnp.int32, (2 * C, 2 * C), 0)
    ci = lax.broadcasted_iota(jnp.int32, (2 * C, 2 * C), 1)
    t_i = jnp.where(ri >= C, ri - C, ri)
    s_i = jnp.where(ci >= C, ci - C, ci)
    mask = s_i < t_i + jnp.where(ri >= C, 1, 0)
    eye = jnp.where(lax.broadcasted_iota(jnp.int32, (C, C), 0) == lax.broadcasted_iota(jnp.int32, (C, C), 1),
                    1.0, 0.0)

    for h in range(RWKV_HEADS):
        sl = slice(N * h, N * (h + 1))
        AR = jnp.concatenate([ab[:, sl], rb[:, sl]], axis=0).astype(BF16)
        BK = jnp.concatenate([bt[:, sl], kt[:, sl]], axis=0).astype(BF16)
        BKg = jnp.concatenate([bg[:, sl], kg[:, sl]], axis=0).astype(BF16)
        vh = v[:, sl].astype(BF16)
        G = jnp.where(mask, _dot_nt(AR, BK), 0.0)
        A = G[:C, :C]
        P = eye + A
        Q = A
        for _ in range(int(math.log2(C)) - 1):
            Qb = Q.astype(BF16)
            Q = _dot(Qb, Qb)
            P = P + _dot(P.astype(BF16), Q.astype(BF16))
        HT = h_scr[h]
        ARH = _dot_nt(AR, HT.astype(BF16))
        rhs = ARH[:C] + _dot(G[:C, C:].astype(BF16), vh)
        U = _dot(P.astype(BF16), rhs.astype(BF16))
        UV = jnp.concatenate([U.astype(BF16), vh], axis=0)
        y_scr[:, sl] = ARH[C:] + _dot(G[C:, :].astype(BF16), UV)
        h_scr[h] = HT * g_end[:, sl] + _dot_tn(UV, BKg)

    y = y_scr[...]
    inv_n = 1.0 / N
    mean = segsum(y) * inv_n
    d = y - mean
    var = segsum(d * d) * inv_n
    yn = d * lax.rsqrt(var + RWKV_GN_EPS) * ln_g + ln_b
    bonus = segsum(r * k2 * r_k) * v
    o_ref[...] = ((yn + bonus) * g).astype(o_ref.dtype)


def _rwkv_group(p0, cols, batch, seq_len, vec, vec2, w2a2, g2):
    C = RWKV_CHUNK
    nc = seq_len // C
    T = batch * seq_len
    cr, ck, cv, czwa, czg = cols
    eseg = (np.arange(RWKV_W)[:, None] // RWKV_N == np.arange(RWKV_W)[None, :] // RWKV_N)
    eseg = jnp.asarray(eseg, BF16)
    ltri = jnp.asarray(np.tril(np.ones((C, C))), BF16)

    def col(width, idx):
        return pl.BlockSpec((C, width), lambda b, c: (b * nc + c, idx))

    def full(arr):
        return pl.BlockSpec(arr.shape, lambda b, c: (0,) * arr.ndim)

    return pl.pallas_call(
        _rwkv_kernel,
        grid=(batch, nc),
        in_specs=[col(RWKV_W, cr), col(RWKV_W, ck), col(RWKV_W, cv), col(128, czwa), col(128, czg),
                  full(vec), full(vec2), full(w2a2), full(g2), full(eseg), full(ltri)],
        out_specs=pl.BlockSpec((C, RWKV_W), lambda b, c: (b * nc + c, 0)),
        out_shape=jax.ShapeDtypeStruct((T, RWKV_W), BF16),
        scratch_shapes=[pltpu.VMEM((RWKV_HEADS, RWKV_N, RWKV_N), F32),
                        pltpu.VMEM((8, RWKV_W), F32),
                        pltpu.VMEM((8, 128), F32),
                        pltpu.VMEM((C, RWKV_W), F32)],
        compiler_params=_cparams(("arbitrary", "arbitrary")),
    )(p0, p0, p0, p0, p0, vec, vec2, w2a2, g2, eseg, ltri)


def _ssd_kernel(z_ref, xs_ref, bc_ref, dt_ref, cwx_ref, cwbc_ref, vecx_ref, vecbc_ref, vecd_ref,
                eh_ref, ltri_ref, o_ref, st_scr, tailx_scr, tailbc_scr, y_scr):
    c = pl.program_id(1)
    Q = xs_ref.shape[0]
    P, N = SSD_P, SSD_N
    GW = SSD_W // SSD_GROUPS

    @pl.when(c == 0)
    def _():
        st_scr[...] = jnp.zeros_like(st_scr)
        tailx_scr[...] = jnp.zeros_like(tailx_scr)
        tailbc_scr[...] = jnp.zeros_like(tailbc_scr)

    def conv_silu(x_raw, tail_scr, cw_ref, b):
        ext = jnp.concatenate([tail_scr[...], x_raw], axis=0)
        acc = b + cw_ref[SSD_CONV - 1:SSD_CONV, :] * x_raw
        for s in range(1, SSD_CONV):
            acc = acc + cw_ref[SSD_CONV - 1 - s:SSD_CONV - s, :] * pltpu.roll(ext, s, 0)[8:, :]
        tail_scr[...] = x_raw[Q - 8:, :]
        return _silu(acc)

    xs = conv_silu(xs_ref[...], tailx_scr, cwx_ref, vecx_ref[0:1, :])
    bc = conv_silu(bc_ref[...], tailbc_scr, cwbc_ref, vecbc_ref[0:1, :])

    eh = eh_ref[...]
    dt = _softplus(dt_ref[...] + vecd_ref[0:1, :])
    a = dt * vecd_ref[1:2, :]
    acum = _dot_sel_lhs(ltri_ref[...], a)
    dt_e = _dot_sel_rhs(dt, eh)
    acum_e = _dot_sel_rhs(acum, eh)
    acum_t = acum.T
    a_last = acum_e[Q - 1:Q, :]
    xdt = xs * dt_e
    xdte = (xdt * jnp.exp(a_last - acum_e)).astype(BF16)
    xdt_b = xdt.astype(BF16)
    chunk_decay = jnp.exp(a_last)
    ea = jnp.exp(acum_e)

    li = lax.broadcasted_iota(jnp.int32, (Q, Q), 0)
    si = lax.broadcasted_iota(jnp.int32, (Q, Q), 1)
    causal = si <= li

    for gi in range(SSD_GROUPS):
        gs = slice(gi * GW, (gi + 1) * GW)
        Bg = bc[:, gi * N:(gi + 1) * N].astype(BF16)
        Cg = bc[:, SSD_GROUPS * N + gi * N:SSD_GROUPS * N + (gi + 1) * N].astype(BF16)
        cb = _dot_nt(Cg, Bg)
        prev = st_scr[gi]
        y_off = _dot(Cg, prev.astype(BF16)) * ea[:, gs]
        st_scr[gi] = prev * chunk_decay[:, gs] + _dot(Bg.T, xdte[:, gs])
        for j in range(SSD_HEADS // SSD_GROUPS):
            hh = gi * (SSD_HEADS // SSD_GROUPS) + j
            hs = slice(hh * P, (hh + 1) * P)
            seg = acum_e[:, hs] - acum_t[hh:hh + 1, :]
            wm = jnp.where(causal, cb * jnp.exp(seg), 0.0).astype(BF16)
            y_scr[:, hs] = _dot(wm, xdt_b[:, hs])
        y_scr[:, gs] = y_scr[:, gs] + y_off

    y = (y_scr[...] + vecx_ref[2:3, :] * xs) * _silu(z_ref[...])
    outs = []
    for gi in range(SSD_GROUPS):
        yg = y[:, gi * GW:(gi + 1) * GW]
        outs.append(yg * lax.rsqrt(jnp.mean(yg * yg, axis=-1, keepdims=True) + EPS))
    o_ref[...] = (jnp.concatenate(outs, axis=-1) * vecx_ref[1:2, :]).astype(o_ref.dtype)


def _ssd_group(p0, cols, batch, seq_len, cwx, cwbc, vecx, vecbc, vecd):
    Q = SSD_CHUNK
    assert Q == SSD_P
    nc = seq_len // Q
    T = batch * seq_len
    cz, cx, cbc, cdt = cols
    eh = np.zeros((128, SSD_W), np.float32)
    for hh in range(SSD_HEADS):
        eh[hh, hh * SSD_P:(hh + 1) * SSD_P] = 1.0
    eh = jnp.asarray(eh, BF16)
    ltri = jnp.asarray(np.tril(np.ones((Q, Q))), BF16)

    def col(width, idx):
        return pl.BlockSpec((Q, width), lambda b, c: (b * nc + c, idx))

    def full(arr):
        return pl.BlockSpec(arr.shape, lambda b, c: (0,) * arr.ndim)

    return pl.pallas_call(
        _ssd_kernel,
        grid=(batch, nc),
        in_specs=[col(SSD_W, cz), col(SSD_W, cx), col(512, cbc), col(128, cdt),
                  full(cwx), full(cwbc), full(vecx), full(vecbc), full(vecd), full(eh), full(ltri)],
        out_specs=pl.BlockSpec((Q, SSD_W), lambda b, c: (b * nc + c, 0)),
        out_shape=jax.ShapeDtypeStruct((T, SSD_W), BF16),
        scratch_shapes=[pltpu.VMEM((SSD_GROUPS, SSD_N, SSD_W // SSD_GROUPS), F32),
                        pltpu.VMEM((8, SSD_W), F32),
                        pltpu.VMEM((8, 512), F32),
                        pltpu.VMEM((Q, SSD_W), F32)],
        compiler_params=_cparams(("arbitrary", "arbitrary")),
    )(p0, p0, p0, p0, cwx, cwbc, vecx, vecbc, vecd, eh, ltri)


def _s5_kernel(u_ref, toep_ref, bz_ref, cz_ref, lam_ref, d_ref, wglu_ref, bglu_ref, o_ref,
               st_scr, zre_scr, zim_scr, sre_scr, sim_scr, y_scr):
    i = pl.program_id(1)
    Mb = u_ref.shape[0]
    LC, NT = S5_LC, S5_TILES
    HW = S5_W // NT * (S5_P // S5_H)

    @pl.when(i == 0)
    def _():
        st_scr[...] = jnp.zeros_like(st_scr)

    u = u_ref[...]
    ucat = []
    for n in range(NT):
        un = jnp.concatenate([u[:, l * S5_W + n * 128:l * S5_W + (n + 1) * 128] for l in range(LC)],
                             axis=1).astype(BF16)
        ucat.append(un)
        z = _dot(un, bz_ref[n])
        zre_scr[:, n * HW:(n + 1) * HW] = z[:, :HW]
        zim_scr[:, n * HW:(n + 1) * HW] = z[:, HW:]

    lr = lam_ref[0:1, :]
    lim = lam_ref[1:2, :]

    def step(m, carry):
        sre, sim = carry
        sre_scr[pl.ds(m, 1), :] = sre
        sim_scr[pl.ds(m, 1), :] = sim
        zr = zre_scr[pl.ds(m, 1), :]
        zi = zim_scr[pl.ds(m, 1), :]
        return lr * sre - lim * sim + zr, lr * sim + lim * sre + zi

    sre, sim = lax.fori_loop(0, Mb, step, (st_scr[0:1, :], st_scr[1:2, :]))
    st_scr[0:1, :] = sre
    st_scr[1:2, :] = sim

    for n in range(NT):
        s_re = sre_scr[:, n * HW:(n + 1) * HW].astype(BF16)
        s_im = sim_scr[:, n * HW:(n + 1) * HW].astype(BF16)
        yn = _dot(ucat[n], toep_ref[n]) + _dot(s_re, cz_ref[n, :HW, :]) + _dot(s_im, cz_ref[n, HW:, :])
        for l in range(LC):
            y_scr[:, l * S5_W + n * 128:l * S5_W + (n + 1) * 128] = yn[:, l * 128:(l + 1) * 128]

    y = y_scr[...] + d_ref[...] * u
    y = 0.5 * y * (1.0 + jnp.tanh(math.sqrt(2.0 / math.pi) * (y + 0.044715 * (y * y * y))))
    wglu = wglu_ref[...]
    for l in range(LC):
        yl = y[:, l * S5_W:(l + 1) * S5_W]
        gate = _sigmoid(_dot(yl.astype(BF16), wglu) + bglu_ref[...])
        o_ref[:, l * S5_W:(l + 1) * S5_W] = (yl * gate).astype(o_ref.dtype)


def _s5_tables(lam_re, lam_im, log_dt, b_re, b_im, c_re, c_im):
    LC, NT, G, H, Pn = S5_LC, S5_TILES, S5_GROUPS, S5_H, S5_P
    GL = G // NT
    hp = lax.Precision.HIGHEST
    dt = jnp.exp(log_dt)[:, None]
    ar, ai = lam_re * dt, lam_im * dt
    jj = jnp.arange(LC + 1, dtype=F32)[:, None, None]
    mag = jnp.exp(jj * ar[None])
    pw_re, pw_im = mag * jnp.cos(jj * ai[None]), mag * jnp.sin(jj * ai[None])
    nr, ni = pw_re[1] - 1.0, pw_im[1]
    den = lam_re * lam_re + lam_im * lam_im
    qr, qi = (nr * lam_re + ni * lam_im) / den, (ni * lam_re - nr * lam_im) / den
    bb_re = qr[..., None] * b_re - qi[..., None] * b_im
    bb_im = qr[..., None] * b_im + qi[..., None] * b_re
    cl_re = c_re[None] * pw_re[:, :, None, :] - c_im[None] * pw_im[:, :, None, :]
    cl_im = c_re[None] * pw_im[:, :, None, :] + c_im[None] * pw_re[:, :, None, :]
    kd = (jnp.einsum("jghp,gpk->jghk", cl_re[:LC], bb_re, precision=hp)
          - jnp.einsum("jghp,gpk->jghk", cl_im[:LC], bb_im, precision=hp))
    lo = jnp.arange(LC)
    diff = lo[None, :] - lo[:, None]
    kt = jnp.where((diff >= 0)[:, :, None, None, None], kd[jnp.clip(diff, 0, LC - 1)], 0.0)
    eye = jnp.eye(GL, dtype=F32)
    kt = kt.reshape(LC, LC, NT, GL, H, H)
    toep = jnp.einsum("abngqk,gm->nagkbmq", kt, eye).reshape(NT, LC * 128, LC * 128)
    czr = cl_re[1:].reshape(LC, NT, GL, H, Pn)
    czi = cl_im[1:].reshape(LC, NT, GL, H, Pn)
    cz_re = jnp.einsum("lnghp,gm->ngplmh", czr, eye).reshape(NT, GL * Pn, LC * 128)
    cz_im = -jnp.einsum("lnghp,gm->ngplmh", czi, eye).reshape(NT, GL * Pn, LC * 128)
    cz = jnp.concatenate([cz_re, cz_im], axis=1)
    jr = (LC - 1.0) - jnp.arange(LC, dtype=F32)[:, None, None]
    mag_r = jnp.exp(jr * ar[None])
    rev_re, rev_im = mag_r * jnp.cos(jr * ai[None]), mag_r * jnp.sin(jr * ai[None])
    bl_re = rev_re[..., None] * bb_re[None] - rev_im[..., None] * bb_im[None]
    bl_im = rev_re[..., None] * bb_im[None] + rev_im[..., None] * bb_re[None]
    bl_re = bl_re.reshape(LC, NT, GL, Pn, H)
    bl_im = bl_im.reshape(LC, NT, GL, Pn, H)
    bz_re = jnp.einsum("lngpk,gm->nlgkmp", bl_re, eye).reshape(NT, LC * 128, GL * Pn)
    bz_im = jnp.einsum("lngpk,gm->nlgkmp", bl_im, eye).reshape(NT, LC * 128, GL * Pn)
    bz = jnp.concatenate([bz_re, bz_im], axis=2)
    lam = jnp.zeros((8, NT * GL * Pn), F32)
    lam = lam.at[0].set(pw_re[LC].reshape(-1)).at[1].set(pw_im[LC].reshape(-1))
    return toep.astype(BF16), bz.astype(BF16), cz.astype(BF16), lam


def _s5_group(p1, col_u, batch, seq_len, tables, d_skip, w_glu, b_glu):
    LC = S5_LC
    T = batch * seq_len
    toep, bz, cz, lam = tables
    rows_per_seq = seq_len // LC
    Mb = min(S5_ROWS, rows_per_seq)
    nb = rows_per_seq // Mb
    NPc = p1.shape[1]
    u2 = p1.reshape(T // LC, LC * NPc)
    W = LC * S5_W
    d_t = jnp.tile(d_skip.reshape(1, S5_W), (1, LC))
    SW = S5_TILES * (S5_GROUPS // S5_TILES) * S5_P

    def full(arr):
        return pl.BlockSpec(arr.shape, lambda b, i: (0,) * arr.ndim)

    def kern(*refs):
        u_refs = refs[:LC]
        rest = refs[LC:]
        ucat_scr = rest[-1]
        for l in range(LC):
            ucat_scr[:, l * S5_W:(l + 1) * S5_W] = u_refs[l][...]
        _s5_kernel(ucat_scr, *rest[:-1])

    blocks_per_row = NPc // S5_W
    u_specs = [pl.BlockSpec((Mb, S5_W), functools.partial(
        lambda b, i, l: (b * nb + i, l * blocks_per_row + col_u), l=l)) for l in range(LC)]
    return pl.pallas_call(
        kern,
        grid=(batch, nb),
        in_specs=u_specs + [full(toep), full(bz), full(cz), full(lam), full(d_t), full(w_glu),
                            pl.BlockSpec((1, S5_W), lambda b, i: (0, 0))],
        out_specs=pl.BlockSpec((Mb, W), lambda b, i: (b * nb + i, 0)),
        out_shape=jax.ShapeDtypeStruct((T // LC, W), BF16),
        scratch_shapes=[pltpu.VMEM((8, SW), F32),
                        pltpu.VMEM((Mb, SW), F32), pltpu.VMEM((Mb, SW), F32),
                        pltpu.VMEM((Mb, SW), F32), pltpu.VMEM((Mb, SW), F32),
                        pltpu.VMEM((Mb, W), F32),
                        pltpu.VMEM((Mb, W), F32)],
        compiler_params=_cparams(("arbitrary", "arbitrary")),
    )(*([u2] * LC), toep, bz, cz, lam, d_t, w_glu, b_glu.reshape(1, S5_W)).reshape(T, S5_W)


def _ret_kernel(q_ref, k_ref, v_ref, g_ref, cos_ref, sin_ref, intra_ref, kdec_ref, qdec_ref, o_ref,
                st_scr, *, chunk_decay):
    c = pl.program_id(1)
    DK, DV = RET_DK, RET_DV

    @pl.when(c == 0)
    def _():
        st_scr[...] = jnp.zeros_like(st_scr)

    cosf = cos_ref[...]
    sinf = sin_ref[...]
    q = q_ref[...]
    k = k_ref[...]
    v = v_ref[...]
    g = g_ref[...]
    scale = DK ** -0.5
    for h in range(RET_HEADS):
        ks = slice(h * DK, (h + 1) * DK)
        vs = slice(h * DV, (h + 1) * DV)
        qh = q[:, ks]
        kh = k[:, ks]
        qh = qh * cosf + pltpu.roll(qh, DK // 2, 1) * sinf
        kh = (kh * cosf + pltpu.roll(kh, DK // 2, 1) * sinf) * scale
        vh = v[:, vs].astype(BF16)
        scores = (_dot_nt(qh.astype(BF16), kh.astype(BF16)) * intra_ref[h]).astype(BF16)
        prev = st_scr[h]
        y = _dot(scores, vh) + _dot((qh * qdec_ref[:, ks]).astype(BF16), prev.astype(BF16))
        kd = (kh * kdec_ref[:, ks]).T.astype(BF16)
        st_scr[h] = prev * chunk_decay[h] + _dot(kd, vh)
        y = y * lax.rsqrt(jnp.mean(y * y, axis=-1, keepdims=True) + EPS)
        o_ref[:, vs] = (_silu(g[:, vs]) * y).astype(o_ref.dtype)


def _ret_group(p1, cols, batch, seq_len):
    Q = min(RET_CHUNK, seq_len)
    nc = seq_len // Q
    T = batch * seq_len
    cq, ck, cv, cg = cols
    H, DK, DV = RET_HEADS, RET_DK, RET_DV
    half = DK // 2
    theta = 1.0 / (10000.0 ** jnp.linspace(0.0, 1.0, half, dtype=F32))
    ang = jnp.arange(seq_len).astype(F32)[:, None] * theta[None, :]
    cos, sin = jnp.cos(ang), jnp.sin(ang)
    cosf = jnp.concatenate([cos, cos], axis=1)
    sinf = jnp.concatenate([-sin, sin], axis=1)
    log_gamma = jnp.log(1.0 - 2.0 ** (-5.0 - jnp.arange(H, dtype=F32)))
    t = jnp.arange(Q, dtype=F32)
    diff = t[:, None] - t[None, :]
    intra = jnp.where((diff >= 0)[None], jnp.exp(jnp.maximum(diff, 0.0)[None] * log_gamma[:, None, None]), 0.0)
    kdec = jnp.repeat(jnp.exp((Q - 1.0 - t)[:, None] * log_gamma[None, :]), DK, axis=1)
    qdec = jnp.repeat(jnp.exp((t + 1.0)[:, None] * log_gamma[None, :]), DK, axis=1)
    chunk_decay = tuple(float((1.0 - 2.0 ** (-5.0 - h)) ** Q) for h in range(H))

    def col(width, idx):
        return pl.BlockSpec((Q, width), lambda b, c: (b * nc + c, idx))

    def full(arr):
        return pl.BlockSpec(arr.shape, lambda b, c: (0,) * arr.ndim)

    return pl.pallas_call(
        functools.partial(_ret_kernel, chunk_decay=chunk_decay),
        grid=(batch, nc),
        in_specs=[col(H * DK, cq), col(H * DK, ck), col(H * DV, cv), col(H * DV, cg),
                  pl.BlockSpec((Q, DK), lambda b, c: (c, 0)), pl.BlockSpec((Q, DK), lambda b, c: (c, 0)),
                  full(intra), full(kdec), full(qdec)],
        out_specs=pl.BlockSpec((Q, H * DV), lambda b, c: (b * nc + c, 0)),
        out_shape=jax.ShapeDtypeStruct((T, H * DV), BF16),
        scratch_shapes=[pltpu.VMEM((H, DK, DV), F32)],
        compiler_params=_cparams(("arbitrary", "arbitrary")),
    )(p1, p1, p1, p1, cosf, sinf, intra, kdec, qdec)


def _pad_cols(w, n):
    return jnp.pad(w, ((0, 0), (0, n - w.shape[1])))


def kernel(x, l0_norm_mix, l0_w_in, l0_rwkv_mu, l0_rwkv_w0, l0_rwkv_w2, l0_rwkv_a0, l0_rwkv_a2, l0_rwkv_g2, l0_rwkv_k_k, l0_rwkv_k_a, l0_rwkv_r_k, l0_rwkv_ln_g, l0_rwkv_ln_b, l0_ssd_conv_w, l0_ssd_conv_b, l0_ssd_dt_bias, l0_ssd_a_log, l0_ssd_d, l0_ssd_norm_g, l0_w_out, l0_norm_ffn, l0_ffn_up, l0_ffn_conv_w, l0_ffn_conv_b, l0_ffn_down, l1_norm_mix, l1_w_in, l1_s5_lam_re, l1_s5_lam_im, l1_s5_log_dt, l1_s5_b_re, l1_s5_b_im, l1_s5_c_re, l1_s5_c_im, l1_s5_d, l1_s5_w_glu, l1_s5_b_glu, l1_w_out, l1_norm_ffn, l1_ffn_up, l1_ffn_conv_w, l1_ffn_conv_b, l1_ffn_down, final_norm):
    B, L, D = x.shape
    T = B * L
    x2 = x.reshape(T, D).astype(F32)

    RP = 3 * RWKV_W + 256
    o_z = RP
    o_xbc = o_z + SSD_W
    o_dt = o_xbc + SSD_W + 2 * SSD_GROUPS * SSD_N
    w = l0_w_in
    w0p = jnp.concatenate([w[:, o_z:o_z + SSD_W], w[:, o_xbc:o_dt], w[:, :RP], w[:, o_dt:]], axis=1)
    NP0 = 4608
    w0p = _pad_cols(w0p, NP0).astype(BF16)
    p0 = _norm_proj(x2, l0_norm_mix, w0p)

    mu = l0_rwkv_mu.astype(F32)
    vec = jnp.zeros((16, RWKV_W), F32)
    rows = [mu[0:512], mu[512:1024], mu[1024:1536], l0_rwkv_w0, l0_rwkv_a0, l0_rwkv_k_k, l0_rwkv_k_a,
            l0_rwkv_r_k.reshape(-1), l0_rwkv_ln_g, l0_rwkv_ln_b]
    vec = vec.at[:len(rows)].set(jnp.stack([r.astype(F32) for r in rows]))
    vec2 = jnp.zeros((8, 128), F32).at[0].set(mu[1536:1664]).at[1].set(mu[1664:1792])
    w2a2 = jnp.zeros((128, 2 * RWKV_W), F32)
    w2a2 = w2a2.at[:64, :RWKV_W].set(l0_rwkv_w2).at[64:, RWKV_W:].set(l0_rwkv_a2).astype(BF16)
    y_a = _rwkv_group(p0, (5, 6, 7, 32, 33), B, L, vec, vec2, w2a2, l0_rwkv_g2.astype(BF16))

    cw = l0_ssd_conv_w.astype(F32)
    cb = l0_ssd_conv_b.astype(F32)
    cwx = jnp.zeros((8, SSD_W), F32).at[:SSD_CONV].set(cw[:, :SSD_W])
    cwbc = jnp.zeros((8, 512), F32).at[:SSD_CONV].set(cw[:, SSD_W:])
    vecx = jnp.zeros((8, SSD_W), F32).at[0].set(cb[:SSD_W]).at[1].set(l0_ssd_norm_g)
    vecx = vecx.at[2].set(jnp.repeat(l0_ssd_d.astype(F32), SSD_P))
    vecbc = jnp.zeros((8, 512), F32).at[0].set(cb[SSD_W:])
    vecd = jnp.zeros((8, 128), F32).at[0, :SSD_HEADS].set(l0_ssd_dt_bias)
    vecd = vecd.at[1, :SSD_HEADS].set(-jnp.exp(l0_ssd_a_log.astype(F32)))
    y_b = _ssd_group(p0, (0, 1, 4, 34), B, L, cwx, cwbc, vecx, vecbc, vecd)

    wo = l0_w_out.astype(BF16)
    x2 = _out_proj(x2, y_a, y_b, wo[:RWKV_W], wo[RWKV_W:])
    x2 = _conv_ffn(x2, L, l0_norm_ffn, l0_ffn_up.astype(BF16), l0_ffn_conv_w.astype(F32),
                   l0_ffn_conv_b.astype(F32), l0_ffn_down.astype(BF16), final_norm, False)

    w = l1_w_in
    w1p = jnp.concatenate([w[:, 1536:2560], w[:, 2560:3584], w[:, :1536]], axis=1).astype(BF16)
    p1 = _norm_proj(x2, l1_norm_mix, w1p)
    tables = _s5_tables(*(t.astype(F32) for t in (l1_s5_lam_re, l1_s5_lam_im, l1_s5_log_dt, l1_s5_b_re,
                                                   l1_s5_b_im, l1_s5_c_re, l1_s5_c_im)))
    y_c = _s5_group(p1, 4, B, L, tables, l1_s5_d.astype(F32), l1_s5_w_glu.astype(BF16),
                    l1_s5_b_glu.astype(F32))
    y_d = _ret_group(p1, (5, 6, 0, 1), B, L)
    wo = l1_w_out.astype(BF16)
    x2 = _out_proj(x2, y_c, y_d, wo[:S5_W], wo[S5_W:])
    x2 = _conv_ffn(x2, L, l1_norm_ffn, l1_ffn_up.astype(BF16), l1_ffn_conv_w.astype(F32),
                   l1_ffn_conv_b.astype(F32), l1_ffn_down.astype(BF16), final_norm, True)
    return x2.reshape(B, L, D).astype(x.dtype)
```

```python
import functools
import math

import numpy as np
import jax
import jax.numpy as jnp
from jax import lax
from jax.experimental import pallas as pl
from jax.experimental.pallas import tpu as pltpu

F32 = jnp.float32
BF16 = jnp.bfloat16

EPS = 1e-6
D_MODEL = 1024

RWKV_HEADS = 8
RWKV_N = 64
RWKV_W = 512
RWKV_GN_EPS = 64e-5
RWKV_CHUNK = 64
RWKV_BLOCK_CHUNKS = 2

SSD_HEADS = 16
SSD_P = 64
SSD_GROUPS = 2
SSD_N = 128
SSD_W = 1024
SSD_CONV = 4
SSD_CHUNK = 64

S5_W = 512
S5_GROUPS = 32
S5_H = 16
S5_P = 64
S5_LC = 8
S5_TILES = 4
S5_ROWS = 64

RET_HEADS = 4
RET_DK = 128
RET_DV = 256
RET_CHUNK = 256

FFN_HIDDEN = 2816
FFN_TN = 256
FFN_HALO = 16

VMEM_LIMIT = 48 * 1024 * 1024


def _dot(a, b):
    return jnp.dot(a, b, preferred_element_type=F32)


def _dot_nt(a, b):
    return lax.dot_general(a, b, (((1,), (1,)), ((), ())), preferred_element_type=F32)


def _dot_tn(a, b):
    return lax.dot_general(a, b, (((0,), (0,)), ((), ())), preferred_element_type=F32)


def _split3(x):
    hi = x.astype(BF16)
    r1 = x - hi.astype(F32)
    mid = r1.astype(BF16)
    lo = (r1 - mid.astype(F32)).astype(BF16)
    return hi, mid, lo


def _dot_sel_rhs(x, sel):
    hi, mid, lo = _split3(x)
    return _dot(hi, sel) + _dot(mid, sel) + _dot(lo, sel)


def _dot_sel_lhs(sel, x):
    hi, mid, lo = _split3(x)
    return _dot(sel, hi) + _dot(sel, mid) + _dot(sel, lo)


def _sigmoid(x):
    return 1.0 / (1.0 + jnp.exp(-x))


def _silu(x):
    return x * _sigmoid(x)


def _softplus(x):
    return jnp.maximum(x, 0.0) + jnp.log(1.0 + jnp.exp(-jnp.abs(x)))


def _cparams(sem):
    return pltpu.CompilerParams(dimension_semantics=sem, vmem_limit_bytes=VMEM_LIMIT)


def _proj_kernel(x_ref, g_ref, w_ref, o_ref):
    x = x_ref[...]
    y = x * lax.rsqrt(jnp.mean(x * x, axis=-1, keepdims=True) + EPS)
    hn = (y * g_ref[...]).astype(BF16)
    o_ref[...] = _dot(hn, w_ref[...])


def _norm_proj(x2, g, w, tm=256):
    T, D = x2.shape
    N = w.shape[1]
    return pl.pallas_call(
        _proj_kernel,
        grid=(T // tm,),
        in_specs=[pl.BlockSpec((tm, D), lambda i: (i, 0)),
                  pl.BlockSpec((1, D), lambda i: (0, 0)),
                  pl.BlockSpec((D, N), lambda i: (0, 0))],
        out_specs=pl.BlockSpec((tm, N), lambda i: (i, 0)),
        out_shape=jax.ShapeDtypeStruct((T, N), F32),
        compiler_params=_cparams(("parallel",)),
    )(x2, g.reshape(1, D), w)


def _outproj_kernel(x_ref, ya_ref, yb_ref, wa_ref, wb_ref, o_ref):
    o_ref[...] = (x_ref[...] + _dot(ya_ref[...].astype(BF16), wa_ref[...])
                  + _dot(yb_ref[...].astype(BF16), wb_ref[...]))


def _out_proj(x2, ya, yb, wa, wb, tm=512):
    T, D = x2.shape
    Ka, Kb = ya.shape[1], yb.shape[1]
    return pl.pallas_call(
        _outproj_kernel,
        grid=(T // tm,),
        in_specs=[pl.BlockSpec((tm, D), lambda i: (i, 0)),
                  pl.BlockSpec((tm, Ka), lambda i: (i, 0)),
                  pl.BlockSpec((tm, Kb), lambda i: (i, 0)),
                  pl.BlockSpec((Ka, D), lambda i: (0, 0)),
                  pl.BlockSpec((Kb, D), lambda i: (0, 0))],
        out_specs=pl.BlockSpec((tm, D), lambda i: (i, 0)),
        out_shape=jax.ShapeDtypeStruct((T, D), F32),
        compiler_params=_cparams(("parallel",)),
    )(x2, ya, yb, wa, wb)


def _ffn_kernel(x_ref, xh_ref, g_ref, wup_ref, cw_ref, cb_ref, wd_ref, gf_ref, o_ref, act_scr,
                *, blocks_per_seq, final_norm):
    i = pl.program_id(0)
    Hd = wd_ref.shape[0]
    tn = FFN_TN

    def norm(x):
        y = x * lax.rsqrt(jnp.mean(x * x, axis=-1, keepdims=True) + EPS)
        return (y * g_ref[...]).astype(BF16)

    keep = jnp.where(i % blocks_per_seq == 0, 0.0, 1.0)
    hn = jnp.concatenate([norm(xh_ref[...] * keep), norm(x_ref[...])], axis=0)

    def conv(u, cols):
        u1 = pltpu.roll(u, 1, 0)
        u2 = pltpu.roll(u, 2, 0)
        c = cw_ref[2:3, cols] * u + cw_ref[1:2, cols] * u1 + cw_ref[0:1, cols] * u2 + cb_ref[:, cols]
        return c[FFN_HALO:, :]

    for j in range(Hd // tn):
        gs = slice(j * tn, (j + 1) * tn)
        vs = slice(Hd + j * tn, Hd + (j + 1) * tn)
        cg = conv(_dot(hn, wup_ref[:, gs]), gs)
        cv = conv(_dot(hn, wup_ref[:, vs]), vs)
        act_scr[:, gs] = (_silu(cg) * cv).astype(BF16)

    y = x_ref[...] + _dot(act_scr[...], wd_ref[...])
    if final_norm:
        y = y * lax.rsqrt(jnp.mean(y * y, axis=-1, keepdims=True) + EPS) * gf_ref[...]
    o_ref[...] = y


def _resident(shape):
    return pl.BlockSpec(shape, lambda *_: (0,) * len(shape), pipeline_mode=pl.Buffered(1))


def _conv_ffn(x2, seq_len, g, w_up, conv_w, conv_b, w_down, g_final, final_norm, tm=512):
    T, D = x2.shape
    Hd = w_down.shape[0]
    tm = min(tm, seq_len)
    hb = tm // FFN_HALO
    kern = functools.partial(_ffn_kernel, blocks_per_seq=seq_len // tm, final_norm=final_norm)
    return pl.pallas_call(
        kern,
        grid=(T // tm,),
        in_specs=[pl.BlockSpec((tm, D), lambda i: (i, 0)),
                  pl.BlockSpec((FFN_HALO, D), lambda i: (jnp.maximum(i * hb - 1, 0), 0)),
                  _resident((1, D)), _resident((D, 2 * Hd)), _resident((3, 2 * Hd)), _resident((1, 2 * Hd)),
                  _resident((Hd, D)), _resident((1, D))],
        out_specs=pl.BlockSpec((tm, D), lambda i: (i, 0)),
        out_shape=jax.ShapeDtypeStruct((T, D), F32),
        scratch_shapes=[pltpu.VMEM((tm, Hd), BF16)],
        compiler_params=_cparams(("parallel",)),
    )(x2, x2, g.reshape(1, D), w_up, conv_w, conv_b.reshape(1, -1), w_down, g_final.reshape(1, D))


def _rwkv_kernel(r_ref, k_ref, v_ref, zwa_ref, zg_ref, vec_ref, vec2_ref, w2a2_ref, g2_ref,
                 eseg_ref, csum_ref, o_ref, h_scr, prev_scr, prev2_scr, y_scr):
    c = pl.program_id(1)
    RB = r_ref.shape[0]
    C = RWKV_CHUNK
    N = RWKV_N

    @pl.when(c == 0)
    def _():
        h_scr[...] = jnp.zeros_like(h_scr)
        prev_scr[...] = jnp.zeros_like(prev_scr)
        prev2_scr[...] = jnp.zeros_like(prev2_scr)

    row = lax.broadcasted_iota(jnp.int32, (RB, 1), 0)

    def shift_lerp(x, prev_row, mu):
        xs = jnp.where(row == 0, prev_row, pltpu.roll(x, 1, 0))
        return x + (xs - x) * mu

    r_raw, k_raw, v_raw = r_ref[...], k_ref[...], v_ref[...]
    zwa_raw, zg_raw = zwa_ref[...], zg_ref[...]
    r = shift_lerp(r_raw, prev_scr[0:1, :], vec_ref[0:1, :])
    k = shift_lerp(k_raw, prev_scr[1:2, :], vec_ref[1:2, :])
    v = shift_lerp(v_raw, prev_scr[2:3, :], vec_ref[2:3, :])
    zwa = shift_lerp(zwa_raw, prev2_scr[0:1, :], vec2_ref[0:1, :])
    zg = shift_lerp(zg_raw, prev2_scr[1:2, :], vec2_ref[1:2, :])
    prev_scr[0:1, :] = r_raw[RB - 1:RB, :]
    prev_scr[1:2, :] = k_raw[RB - 1:RB, :]
    prev_scr[2:3, :] = v_raw[RB - 1:RB, :]
    prev2_scr[0:1, :] = zwa_raw[RB - 1:RB, :]
    prev2_scr[1:2, :] = zg_raw[RB - 1:RB, :]

    w0, a0 = vec_ref[3:4, :], vec_ref[4:5, :]
    k_k, k_a, r_k = vec_ref[5:6, :], vec_ref[6:7, :], vec_ref[7:8, :]
    ln_g, ln_b = vec_ref[8:9, :], vec_ref[9:10, :]

    lane = lax.broadcasted_iota(jnp.int32, (1, 128), 1)
    lora_in = jnp.where(lane < 64, jnp.tanh(zwa), zwa).astype(BF16)
    wa = _dot(lora_in, w2a2_ref[...])
    w_log = -_softplus(-(w0 + wa[:, :RWKV_W])) - 0.5
    logw = -jnp.exp(w_log)
    a = _sigmoid(a0 + wa[:, RWKV_W:])
    g = _dot(_sigmoid(zg).astype(BF16), g2_ref[...])

    eseg = eseg_ref[...]

    def segsum(x):
        hi = x.astype(BF16)
        lo = (x - hi.astype(F32)).astype(BF16)
        return _dot(hi, eseg) + _dot(lo, eseg)

    kk = k * k_k
    kk = kk * lax.rsqrt(jnp.maximum(segsum(kk * kk), 1e-24))
    k2 = k * (1.0 + (a - 1.0) * k_a)

    cs = _dot_sel_lhs(csum_ref[...], logw)
    lg, lg_end = cs[:RB], cs[RB:]
    eg = jnp.exp(lg)
    egi = jnp.exp(-lg)
    g_end = jnp.exp(lg_end)
    d_end = jnp.exp(lg_end - lg)
    kka = kk * a
    ab = -kk * jnp.exp(lg - logw)
    rb = r * eg
    bt = kka * egi
    kt = k2 * egi
    bg = kka * d_end
    kg = k2 * d_end

    W2 = 2 * N
    left = lax.broadcasted_iota(jnp.int32, (1, W2), 1) < N

    def bd(x):
        return jnp.concatenate([jnp.where(left, x, 0.0), jnp.where(left, 0.0, x)], axis=0)

    ri = lax.broadcasted_iota(jnp.int32, (2 * W2, 2 * W2), 0)
    ci = lax.broadcasted_iota(jnp.int32, (2 * W2, 2 * W2), 1)
    mask = (ci & (N - 1)) < (ri & (N - 1)) + jnp.where(ri >= W2, 1, 0)
    e_r = lax.broadcasted_iota(jnp.int32, (W2, W2), 0)
    e_c = lax.broadcasted_iota(jnp.int32, (W2, W2), 1)
    eye = jnp.where(e_r == e_c, 1.0, 0.0)
    zeros_b = jnp.zeros((W2, W2), BF16)

    units = [(j, p) for j in range(RB // C) for p in range(RWKV_HEADS // 2)]
    U = len(units)

    def piece(x, j, p):
        return x[j * C:(j + 1) * C, p * W2:(p + 1) * W2]

    ab_u = [bd(piece(ab, j, p)).astype(BF16) for j, p in units]
    AR = [jnp.concatenate([ab_u[i], bd(piece(rb, j, p)).astype(BF16)], axis=0) for i, (j, p) in enumerate(units)]
    BK = [jnp.concatenate([bd(piece(bt, j, p)), bd(piece(kt, j, p))], axis=0).astype(BF16) for j, p in units]
    v_u = [bd(piece(v, j, p)).astype(BF16) for j, p in units]
    G = [jnp.where(mask, _dot_nt(AR[i], BK[i]), 0.0) for i in range(U)]
    A = [G[i][:W2, :W2] for i in range(U)]
    S = [eye + A[i] for i in range(U)]
    Ab16 = [A[i].astype(BF16) for i in range(U)]
    Q = [_dot(Ab16[i], Ab16[i]) for i in range(U)]
    for _ in range(int(math.log2(C)) - 2):
        Qb = [Q[i].astype(BF16) for i in range(U)]
        QS = [_dot(jnp.concatenate([Qb[i], S[i].astype(BF16)], axis=0), Qb[i]) for i in range(U)]
        Q = [QS[i][:W2] for i in range(U)]
        S = [S[i] + QS[i][W2:] for i in range(U)]
    S = [(S[i] + _dot(S[i].astype(BF16), Q[i].astype(BF16))).astype(BF16) for i in range(U)]
    akv = [_dot(G[i][:W2, W2:].astype(BF16), v_u[i]) for i in range(U)]
    WU = [_dot(S[i], jnp.concatenate([ab_u[i], akv[i].astype(BF16)], axis=1)).astype(BF16) for i in range(U)]
    Z = [jnp.concatenate([WU[i], jnp.concatenate([zeros_b, v_u[i]], axis=1)], axis=0) for i in range(U)]
    QY = [_dot(G[i][W2:, :].astype(BF16), Z[i]) for i in range(U)]
    BKgT = [jnp.concatenate([bd(piece(bg, j, p)).T, bd(piece(kg, j, p)).T], axis=1).astype(BF16) for j, p in units]
    MN = [_dot(BKgT[i], Z[i]) for i in range(U)]
    QM = [jnp.concatenate([AR[i][W2:].astype(F32) + QY[i][:, :W2], MN[i][:, :W2]], axis=0).astype(BF16)
          for i in range(U)]
    gcol = [jnp.sum(eye * piece(g_end, j, p)[0:1, :], axis=1, keepdims=True) for j, p in units]

    for i, (j, p) in enumerate(units):
        H = h_scr[p]
        YH = _dot(QM[i], H.astype(BF16))
        ybd = YH[:W2] + QY[i][:, W2:]
        y_scr[j * C:(j + 1) * C, p * W2:(p + 1) * W2] = ybd[:C] + ybd[C:]
        h_scr[p] = H * gcol[i] + YH[W2:] + MN[i][:, W2:]

    y = y_scr[...]
    inv_n = 1.0 / N
    mean = segsum(y) * inv_n
    d = y - mean
    var = segsum(d * d) * inv_n
    yn = d * lax.rsqrt(var + RWKV_GN_EPS) * ln_g + ln_b
    bonus = segsum(r * k2 * r_k) * v
    o_ref[...] = ((yn + bonus) * g).astype(o_ref.dtype)


def _rwkv_group(p0, cols, batch, seq_len, vec, vec2, w2a2, g2):
    C = RWKV_CHUNK
    RB = C * RWKV_BLOCK_CHUNKS
    nc = seq_len // RB
    T = batch * seq_len
    cr, ck, cv, czwa, czg = cols
    eseg = (np.arange(RWKV_W)[:, None] // RWKV_N == np.arange(RWKV_W)[None, :] // RWKV_N)
    eseg = jnp.asarray(eseg, BF16)
    same_chunk = np.arange(RB)[:, None] // C == np.arange(RB)[None, :] // C
    csum = np.concatenate([np.tril(np.ones((RB, RB))) * same_chunk, same_chunk], axis=0)
    csum = jnp.asarray(csum, BF16)

    def col(width, idx):
        return pl.BlockSpec((RB, width), lambda b, c: (b * nc + c, idx))

    def full(arr):
        return pl.BlockSpec(arr.shape, lambda b, c: (0,) * arr.ndim)

    return pl.pallas_call(
        _rwkv_kernel,
        grid=(batch, nc),
        in_specs=[col(RWKV_W, cr), col(RWKV_W, ck), col(RWKV_W, cv), col(128, czwa), col(128, czg),
                  full(vec), full(vec2), full(w2a2), full(g2), full(eseg), full(csum)],
        out_specs=pl.BlockSpec((RB, RWKV_W), lambda b, c: (b * nc + c, 0)),
        out_shape=jax.ShapeDtypeStruct((T, RWKV_W), BF16),
        scratch_shapes=[pltpu.VMEM((RWKV_HEADS // 2, 2 * RWKV_N, 2 * RWKV_N), F32),
                        pltpu.VMEM((8, RWKV_W), F32),
                        pltpu.VMEM((8, 128), F32),
                        pltpu.VMEM((RB, RWKV_W), F32)],
        compiler_params=_cparams(("arbitrary", "arbitrary")),
    )(p0, p0, p0, p0, p0, vec, vec2, w2a2, g2, eseg, csum)


def _ssd_kernel(z_ref, xs_ref, bc_ref, dt_ref, cwx_ref, cwbc_ref, vecx_ref, vecbc_ref, vecd_ref,
                eh_ref, ltri_ref, o_ref, st_scr, tailx_scr, tailbc_scr, y_scr):
    c = pl.program_id(1)
    Q = xs_ref.shape[0]
    P, N = SSD_P, SSD_N
    GW = SSD_W // SSD_GROUPS

    @pl.when(c == 0)
    def _():
        st_scr[...] = jnp.zeros_like(st_scr)
        tailx_scr[...] = jnp.zeros_like(tailx_scr)
        tailbc_scr[...] = jnp.zeros_like(tailbc_scr)

    def conv_silu(x_raw, tail_scr, cw_ref, b):
        ext = jnp.concatenate([tail_scr[...], x_raw], axis=0)
        acc = b + cw_ref[SSD_CONV - 1:SSD_CONV, :] * x_raw
        for s in range(1, SSD_CONV):
            acc = acc + cw_ref[SSD_CONV - 1 - s:SSD_CONV - s, :] * pltpu.roll(ext, s, 0)[8:, :]
        tail_scr[...] = x_raw[Q - 8:, :]
        return _silu(acc)

    xs = conv_silu(xs_ref[...], tailx_scr, cwx_ref, vecx_ref[0:1, :])
    bc = conv_silu(bc_ref[...], tailbc_scr, cwbc_ref, vecbc_ref[0:1, :])

    eh = eh_ref[...]
    dt = _softplus(dt_ref[...] + vecd_ref[0:1, :])
    a = dt * vecd_ref[1:2, :]
    acum = _dot_sel_lhs(ltri_ref[...], a)
    dt_e = _dot_sel_rhs(dt, eh)
    acum_e = _dot_sel_rhs(acum, eh)
    acum_t = acum.T
    a_last = acum_e[Q - 1:Q, :]
    xdt = xs * dt_e
    xdte = (xdt * jnp.exp(a_last - acum_e)).astype(BF16)
    xdt_b = xdt.astype(BF16)
    chunk_decay = jnp.exp(a_last)
    ea = jnp.exp(acum_e)

    li = lax.broadcasted_iota(jnp.int32, (Q, Q), 0)
    si = lax.broadcasted_iota(jnp.int32, (Q, Q), 1)
    causal = si <= li

    for gi in range(SSD_GROUPS):
        gs = slice(gi * GW, (gi + 1) * GW)
        Bg = bc[:, gi * N:(gi + 1) * N].astype(BF16)
        Cg = bc[:, SSD_GROUPS * N + gi * N:SSD_GROUPS * N + (gi + 1) * N].astype(BF16)
        cb = _dot_nt(Cg, Bg)
        prev = st_scr[gi]
        y_off = _dot(Cg, prev.astype(BF16)) * ea[:, gs]
        st_scr[gi] = prev * chunk_decay[:, gs] + _dot(Bg.T, xdte[:, gs])
        for j in range(SSD_HEADS // SSD_GROUPS):
            hh = gi * (SSD_HEADS // SSD_GROUPS) + j
            hs = slice(hh * P, (hh + 1) * P)
            seg = acum_e[:, hs] - acum_t[hh:hh + 1, :]
            wm = jnp.where(causal, cb * jnp.exp(seg), 0.0).astype(BF16)
            y_scr[:, hs] = _dot(wm, xdt_b[:, hs])
        y_scr[:, gs] = y_scr[:, gs] + y_off

    y = (y_scr[...] + vecx_ref[2:3, :] * xs) * _silu(z_ref[...])
    outs = []
    for gi in range(SSD_GROUPS):
        yg = y[:, gi * GW:(gi + 1) * GW]
        outs.append(yg * lax.rsqrt(jnp.mean(yg * yg, axis=-1, keepdims=True) + EPS))
    o_ref[...] = (jnp.concatenate(outs, axis=-1) * vecx_ref[1:2, :]).astype(o_ref.dtype)


def _ssd_group(p0, cols, batch, seq_len, cwx, cwbc, vecx, vecbc, vecd):
    Q = SSD_CHUNK
    assert Q == SSD_P
    nc = seq_len // Q
    T = batch * seq_len
    cz, cx, cbc, cdt = cols
    eh = np.zeros((128, SSD_W), np.float32)
    for hh in range(SSD_HEADS):
        eh[hh, hh * SSD_P:(hh + 1) * SSD_P] = 1.0
    eh = jnp.asarray(eh, BF16)
    ltri = jnp.asarray(np.tril(np.ones((Q, Q))), BF16)

    def col(width, idx):
        return pl.BlockSpec((Q, width), lambda b, c: (b * nc + c, idx))

    def full(arr):
        return pl.BlockSpec(arr.shape, lambda b, c: (0,) * arr.ndim)

    return pl.pallas_call(
        _ssd_kernel,
        grid=(batch, nc),
        in_specs=[col(SSD_W, cz), col(SSD_W, cx), col(512, cbc), col(128, cdt),
                  full(cwx), full(cwbc), full(vecx), full(vecbc), full(vecd), full(eh), full(ltri)],
        out_specs=pl.BlockSpec((Q, SSD_W), lambda b, c: (b * nc + c, 0)),
        out_shape=jax.ShapeDtypeStruct((T, SSD_W), BF16),
        scratch_shapes=[pltpu.VMEM((SSD_GROUPS, SSD_N, SSD_W // SSD_GROUPS), F32),
                        pltpu.VMEM((8, SSD_W), F32),
                        pltpu.VMEM((8, 512), F32),
                        pltpu.VMEM((Q, SSD_W), F32)],
        compiler_params=_cparams(("arbitrary", "arbitrary")),
    )(p0, p0, p0, p0, cwx, cwbc, vecx, vecbc, vecd, eh, ltri)


def _s5_kernel(u_ref, toep_ref, bz_ref, cz_ref, lam_ref, d_ref, wglu_ref, bglu_ref, o_ref,
               st_scr, zre_scr, zim_scr, sre_scr, sim_scr, y_scr, fold_scr):
    i = pl.program_id(1)
    Mb = u_ref.shape[0]
    LC, NT = S5_LC, S5_TILES
    HW = S5_W // NT * (S5_P // S5_H)

    @pl.when(i == 0)
    def _():
        st_scr[...] = jnp.zeros_like(st_scr)

    u = u_ref[...]
    ucat = []
    for n in range(NT):
        un = jnp.concatenate([u[:, l * S5_W + n * 128:l * S5_W + (n + 1) * 128] for l in range(LC)],
                             axis=1).astype(BF16)
        ucat.append(un)
        z = _dot(un, bz_ref[n])
        zre_scr[:, n * HW:(n + 1) * HW] = z[:, :HW]
        zim_scr[:, n * HW:(n + 1) * HW] = z[:, HW:]

    lr = lam_ref[0:1, :]
    lim = lam_ref[1:2, :]

    def step(m, carry):
        sre, sim = carry
        sre_scr[pl.ds(m, 1), :] = sre
        sim_scr[pl.ds(m, 1), :] = sim
        zr = zre_scr[pl.ds(m, 1), :]
        zi = zim_scr[pl.ds(m, 1), :]
        return lr * sre - lim * sim + zr, lr * sim + lim * sre + zi

    sre, sim = lax.fori_loop(0, Mb, step, (st_scr[0:1, :], st_scr[1:2, :]))
    st_scr[0:1, :] = sre
    st_scr[1:2, :] = sim

    for n in range(NT):
        s_re = sre_scr[:, n * HW:(n + 1) * HW].astype(BF16)
        s_im = sim_scr[:, n * HW:(n + 1) * HW].astype(BF16)
        yn = _dot(ucat[n], toep_ref[n]) + _dot(s_re, cz_ref[n, :HW, :]) + _dot(s_im, cz_ref[n, HW:, :])
        for l in range(LC):
            y_scr[:, l * S5_W + n * 128:l * S5_W + (n + 1) * 128] = yn[:, l * 128:(l + 1) * 128]

    y = y_scr[...] + d_ref[...] * u
    y = 0.5 * y * (1.0 + jnp.tanh(math.sqrt(2.0 / math.pi) * (y + 0.044715 * (y * y * y))))
    wglu = wglu_ref[...]
    for l in range(LC):
        yl = y[:, l * S5_W:(l + 1) * S5_W]
        gate = _sigmoid(_dot(yl.astype(BF16), wglu) + bglu_ref[...])
        out_l = yl * gate
        for n in range(NT):
            fold_scr[n, pl.ds(l, Mb, stride=LC), :] = out_l[:, n * 128:(n + 1) * 128]
    for n in range(NT):
        o_ref[:, n * 128:(n + 1) * 128] = fold_scr[n]


def _s5_tables(lam_re, lam_im, log_dt, b_re, b_im, c_re, c_im):
    LC, NT, G, H, Pn = S5_LC, S5_TILES, S5_GROUPS, S5_H, S5_P
    GL = G // NT
    hp = lax.Precision.HIGHEST
    dt = jnp.exp(log_dt)[:, None]
    ar, ai = lam_re * dt, lam_im * dt
    jj = jnp.arange(LC + 1, dtype=F32)[:, None, None]
    mag = jnp.exp(jj * ar[None])
    pw_re, pw_im = mag * jnp.cos(jj * ai[None]), mag * jnp.sin(jj * ai[None])
    nr, ni = pw_re[1] - 1.0, pw_im[1]
    den = lam_re * lam_re + lam_im * lam_im
    qr, qi = (nr * lam_re + ni * lam_im) / den, (ni * lam_re - nr * lam_im) / den
    bb_re = qr[..., None] * b_re - qi[..., None] * b_im
    bb_im = qr[..., None] * b_im + qi[..., None] * b_re
    cl_re = c_re[None] * pw_re[:, :, None, :] - c_im[None] * pw_im[:, :, None, :]
    cl_im = c_re[None] * pw_im[:, :, None, :] + c_im[None] * pw_re[:, :, None, :]
    kd = (jnp.einsum("jghp,gpk->jghk", cl_re[:LC], bb_re, precision=hp)
          - jnp.einsum("jghp,gpk->jghk", cl_im[:LC], bb_im, precision=hp))
    lo = jnp.arange(LC)
    diff = lo[None, :] - lo[:, None]
    kt = jnp.where((diff >= 0)[:, :, None, None, None], kd[jnp.clip(diff, 0, LC - 1)], 0.0)
    eye = jnp.eye(GL, dtype=F32)
    kt = kt.reshape(LC, LC, NT, GL, H, H)
    toep = jnp.einsum("abngqk,gm->nagkbmq", kt, eye).reshape(NT, LC * 128, LC * 128)
    czr = cl_re[1:].reshape(LC, NT, GL, H, Pn)
    czi = cl_im[1:].reshape(LC, NT, GL, H, Pn)
    cz_re = jnp.einsum("lnghp,gm->ngplmh", czr, eye).reshape(NT, GL * Pn, LC * 128)
    cz_im = -jnp.einsum("lnghp,gm->ngplmh", czi, eye).reshape(NT, GL * Pn, LC * 128)
    cz = jnp.concatenate([cz_re, cz_im], axis=1)
    jr = (LC - 1.0) - jnp.arange(LC, dtype=F32)[:, None, None]
    mag_r = jnp.exp(jr * ar[None])
    rev_re, rev_im = mag_r * jnp.cos(jr * ai[None]), mag_r * jnp.sin(jr * ai[None])
    bl_re = rev_re[..., None] * bb_re[None] - rev_im[..., None] * bb_im[None]
    bl_im = rev_re[..., None] * bb_im[None] + rev_im[..., None] * bb_re[None]
    bl_re = bl_re.reshape(LC, NT, GL, Pn, H)
    bl_im = bl_im.reshape(LC, NT, GL, Pn, H)
    bz_re = jnp.einsum("lngpk,gm->nlgkmp", bl_re, eye).reshape(NT, LC * 128, GL * Pn)
    bz_im = jnp.einsum("lngpk,gm->nlgkmp", bl_im, eye).reshape(NT, LC * 128, GL * Pn)
    bz = jnp.concatenate([bz_re, bz_im], axis=2)
    lam = jnp.zeros((8, NT * GL * Pn), F32)
    lam = lam.at[0].set(pw_re[LC].reshape(-1)).at[1].set(pw_im[LC].reshape(-1))
    return toep.astype(BF16), bz.astype(BF16), cz.astype(BF16), lam


def _s5_group(p1, col_u, batch, seq_len, tables, d_skip, w_glu, b_glu):
    LC = S5_LC
    T = batch * seq_len
    toep, bz, cz, lam = tables
    rows_per_seq = seq_len // LC
    Mb = min(S5_ROWS, rows_per_seq)
    nb = rows_per_seq // Mb
    NPc = p1.shape[1]
    W = LC * S5_W
    d_t = jnp.tile(d_skip.reshape(1, S5_W), (1, LC))
    SW = S5_TILES * (S5_GROUPS // S5_TILES) * S5_P

    def full(arr):
        return pl.BlockSpec(arr.shape, lambda b, i: (0,) * arr.ndim)

    def kern(u_ref, *rest):
        ucat_scr, fold_scr = rest[-2], rest[-1]
        for n in range(S5_TILES):
            fold_scr[n] = u_ref[:, n * 128:(n + 1) * 128]
        for n in range(S5_TILES):
            for l in range(LC):
                ucat_scr[:, l * S5_W + n * 128:l * S5_W + (n + 1) * 128] = fold_scr[n, pl.ds(l, Mb, stride=LC), :]
        _s5_kernel(ucat_scr, *rest[:-2], fold_scr)

    return pl.pallas_call(
        kern,
        grid=(batch, nb),
        in_specs=[pl.BlockSpec((Mb * LC, S5_W), lambda b, i: (b * nb + i, col_u)),
                  full(toep), full(bz), full(cz), full(lam), full(d_t), full(w_glu),
                  pl.BlockSpec((1, S5_W), lambda b, i: (0, 0))],
        out_specs=pl.BlockSpec((Mb * LC, S5_W), lambda b, i: (b * nb + i, 0)),
        out_shape=jax.ShapeDtypeStruct((T, S5_W), F32),
        scratch_shapes=[pltpu.VMEM((8, SW), F32),
                        pltpu.VMEM((Mb, SW), F32), pltpu.VMEM((Mb, SW), F32),
                        pltpu.VMEM((Mb, SW), F32), pltpu.VMEM((Mb, SW), F32),
                        pltpu.VMEM((Mb, W), F32),
                        pltpu.VMEM((Mb, W), F32),
                        pltpu.VMEM((S5_TILES, Mb * LC, 128), F32)],
        compiler_params=_cparams(("arbitrary", "arbitrary")),
    )(p1, toep, bz, cz, lam, d_t, w_glu, b_glu.reshape(1, S5_W))


def _ret_kernel(q_ref, k_ref, v_ref, g_ref, cos_ref, sin_ref, intra_ref, kdec_ref, qdec_ref, o_ref,
                st_scr, *, chunk_decay):
    c = pl.program_id(1)
    DK, DV = RET_DK, RET_DV

    @pl.when(c == 0)
    def _():
        st_scr[...] = jnp.zeros_like(st_scr)

    cosf = cos_ref[...]
    sinf = sin_ref[...]
    q = q_ref[...]
    k = k_ref[...]
    v = v_ref[...]
    g = g_ref[...]
    scale = DK ** -0.5
    for h in range(RET_HEADS):
        ks = slice(h * DK, (h + 1) * DK)
        vs = slice(h * DV, (h + 1) * DV)
        qh = q[:, ks]
        kh = k[:, ks]
        qh = qh * cosf + pltpu.roll(qh, DK // 2, 1) * sinf
        kh = (kh * cosf + pltpu.roll(kh, DK // 2, 1) * sinf) * scale
        vh = v[:, vs].astype(BF16)
        scores = (_dot_nt(qh.astype(BF16), kh.astype(BF16)) * intra_ref[h]).astype(BF16)
        prev = st_scr[h]
        y = _dot(scores, vh) + _dot((qh * qdec_ref[:, ks]).astype(BF16), prev.astype(BF16))
        kd = (kh * kdec_ref[:, ks]).T.astype(BF16)
        st_scr[h] = prev * chunk_decay[h] + _dot(kd, vh)
        y = y * lax.rsqrt(jnp.mean(y * y, axis=-1, keepdims=True) + EPS)
        o_ref[:, vs] = (_silu(g[:, vs]) * y).astype(o_ref.dtype)


def _ret_group(p1, cols, batch, seq_len):
    Q = min(RET_CHUNK, seq_len)
    nc = seq_len // Q
    T = batch * seq_len
    cq, ck, cv, cg = cols
    H, DK, DV = RET_HEADS, RET_DK, RET_DV
    half = DK // 2
    theta = 1.0 / (10000.0 ** jnp.linspace(0.0, 1.0, half, dtype=F32))
    ang = jnp.arange(seq_len).astype(F32)[:, None] * theta[None, :]
    cos, sin = jnp.cos(ang), jnp.sin(ang)
    cosf = jnp.concatenate([cos, cos], axis=1)
    sinf = jnp.concatenate([-sin, sin], axis=1)
    log_gamma = jnp.log(1.0 - 2.0 ** (-5.0 - jnp.arange(H, dtype=F32)))
    t = jnp.arange(Q, dtype=F32)
    diff = t[:, None] - t[None, :]
    intra = jnp.where((diff >= 0)[None], jnp.exp(jnp.maximum(diff, 0.0)[None] * log_gamma[:, None, None]), 0.0)
    kdec = jnp.repeat(jnp.exp((Q - 1.0 - t)[:, None] * log_gamma[None, :]), DK, axis=1)
    qdec = jnp.repeat(jnp.exp((t + 1.0)[:, None] * log_gamma[None, :]), DK, axis=1)
    chunk_decay = tuple(float((1.0 - 2.0 ** (-5.0 - h)) ** Q) for h in range(H))

    def col(width, idx):
        return pl.BlockSpec((Q, width), lambda b, c: (b * nc + c, idx))

    def full(arr):
        return pl.BlockSpec(arr.shape, lambda b, c: (0,) * arr.ndim)

    return pl.pallas_call(
        functools.partial(_ret_kernel, chunk_decay=chunk_decay),
        grid=(batch, nc),
        in_specs=[col(H * DK, cq), col(H * DK, ck), col(H * DV, cv), col(H * DV, cg),
                  pl.BlockSpec((Q, DK), lambda b, c: (c, 0)), pl.BlockSpec((Q, DK), lambda b, c: (c, 0)),
                  full(intra), full(kdec), full(qdec)],
        out_specs=pl.BlockSpec((Q, H * DV), lambda b, c: (b * nc + c, 0)),
        out_shape=jax.ShapeDtypeStruct((T, H * DV), BF16),
        scratch_shapes=[pltpu.VMEM((H, DK, DV), F32)],
        compiler_params=_cparams(("arbitrary", "arbitrary")),
    )(p1, p1, p1, p1, cosf, sinf, intra, kdec, qdec)


def _pad_cols(w, n):
    return jnp.pad(w, ((0, 0), (0, n - w.shape[1])))


def kernel(x, l0_norm_mix, l0_w_in, l0_rwkv_mu, l0_rwkv_w0, l0_rwkv_w2, l0_rwkv_a0, l0_rwkv_a2, l0_rwkv_g2, l0_rwkv_k_k, l0_rwkv_k_a, l0_rwkv_r_k, l0_rwkv_ln_g, l0_rwkv_ln_b, l0_ssd_conv_w, l0_ssd_conv_b, l0_ssd_dt_bias, l0_ssd_a_log, l0_ssd_d, l0_ssd_norm_g, l0_w_out, l0_norm_ffn, l0_ffn_up, l0_ffn_conv_w, l0_ffn_conv_b, l0_ffn_down, l1_norm_mix, l1_w_in, l1_s5_lam_re, l1_s5_lam_im, l1_s5_log_dt, l1_s5_b_re, l1_s5_b_im, l1_s5_c_re, l1_s5_c_im, l1_s5_d, l1_s5_w_glu, l1_s5_b_glu, l1_w_out, l1_norm_ffn, l1_ffn_up, l1_ffn_conv_w, l1_ffn_conv_b, l1_ffn_down, final_norm):
    B, L, D = x.shape
    T = B * L
    x2 = x.reshape(T, D).astype(F32)

    RP = 3 * RWKV_W + 256
    o_z = RP
    o_xbc = o_z + SSD_W
    o_dt = o_xbc + SSD_W + 2 * SSD_GROUPS * SSD_N
    w = l0_w_in
    w0p = jnp.concatenate([w[:, o_z:o_z + SSD_W], w[:, o_xbc:o_dt], w[:, :RP], w[:, o_dt:]], axis=1)
    NP0 = 4608
    w0p = _pad_cols(w0p, NP0).astype(BF16)
    p0 = _norm_proj(x2, l0_norm_mix, w0p)

    mu = l0_rwkv_mu.astype(F32)
    vec = jnp.zeros((16, RWKV_W), F32)
    rows = [mu[0:512], mu[512:1024], mu[1024:1536], l0_rwkv_w0, l0_rwkv_a0, l0_rwkv_k_k, l0_rwkv_k_a,
            l0_rwkv_r_k.reshape(-1), l0_rwkv_ln_g, l0_rwkv_ln_b]
    vec = vec.at[:len(rows)].set(jnp.stack([r.astype(F32) for r in rows]))
    vec2 = jnp.zeros((8, 128), F32).at[0].set(mu[1536:1664]).at[1].set(mu[1664:1792])
    w2a2 = jnp.zeros((128, 2 * RWKV_W), F32)
    w2a2 = w2a2.at[:64, :RWKV_W].set(l0_rwkv_w2).at[64:, RWKV_W:].set(l0_rwkv_a2).astype(BF16)
    y_a = _rwkv_group(p0, (5, 6, 7, 32, 33), B, L, vec, vec2, w2a2, l0_rwkv_g2.astype(BF16))

    cw = l0_ssd_conv_w.astype(F32)
    cb = l0_ssd_conv_b.astype(F32)
    cwx = jnp.zeros((8, SSD_W), F32).at[:SSD_CONV].set(cw[:, :SSD_W])
    cwbc = jnp.zeros((8, 512), F32).at[:SSD_CONV].set(cw[:, SSD_W:])
    vecx = jnp.zeros((8, SSD_W), F32).at[0].set(cb[:SSD_W]).at[1].set(l0_ssd_norm_g)
    vecx = vecx.at[2].set(jnp.repeat(l0_ssd_d.astype(F32), SSD_P))
    vecbc = jnp.zeros((8, 512), F32).at[0].set(cb[SSD_W:])
    vecd = jnp.zeros((8, 128), F32).at[0, :SSD_HEADS].set(l0_ssd_dt_bias)
    vecd = vecd.at[1, :SSD_HEADS].set(-jnp.exp(l0_ssd_a_log.astype(F32)))
    y_b = _ssd_group(p0, (0, 1, 4, 34), B, L, cwx, cwbc, vecx, vecbc, vecd)

    wo = l0_w_out.astype(BF16)
    x2 = _out_proj(x2, y_a, y_b, wo[:RWKV_W], wo[RWKV_W:])
    x2 = _conv_ffn(x2, L, l0_norm_ffn, l0_ffn_up.astype(BF16), l0_ffn_conv_w.astype(F32),
                   l0_ffn_conv_b.astype(F32), l0_ffn_down.astype(BF16), final_norm, False)

    w = l1_w_in
    w1p = jnp.concatenate([w[:, 1536:2560], w[:, 2560:3584], w[:, :1536]], axis=1).astype(BF16)
    p1 = _norm_proj(x2, l1_norm_mix, w1p)
    tables = _s5_tables(*(t.astype(F32) for t in (l1_s5_lam_re, l1_s5_lam_im, l1_s5_log_dt, l1_s5_b_re,
                                                   l1_s5_b_im, l1_s5_c_re, l1_s5_c_im)))
    y_c = _s5_group(p1, 4, B, L, tables, l1_s5_d.astype(F32), l1_s5_w_glu.astype(BF16),
                    l1_s5_b_glu.astype(F32))
    y_d = _ret_group(p1, (5, 6, 0, 1), B, L)
    wo = l1_w_out.astype(BF16)
    x2 = _out_proj(x2, y_c, y_d, wo[:S5_W], wo[S5_W:])
    x2 = _conv_ffn(x2, L, l1_norm_ffn, l1_ffn_up.astype(BF16), l1_ffn_conv_w.astype(F32),
                   l1_ffn_conv_b.astype(F32), l1_ffn_down.astype(BF16), final_norm, True)
    return x2.reshape(B, L, D).astype(x.dtype)
```

```python
import functools
import math

import numpy as np
import jax
import jax.numpy as jnp
from jax import lax
from jax.experimental import pallas as pl
from jax.experimental.pallas import tpu as pltpu

F32 = jnp.float32
BF16 = jnp.bfloat16

EPS = 1e-6
D_MODEL = 1024

RWKV_HEADS = 8
RWKV_N = 64
RWKV_W = 512
RWKV_GN_EPS = 64e-5
RWKV_CHUNK = 64
RWKV_BLOCK_CHUNKS = 4

SSD_HEADS = 16
SSD_P = 64
SSD_GROUPS = 2
SSD_N = 128
SSD_W = 1024
SSD_CONV = 4
SSD_CHUNK = 64
SSD_BLOCK_CHUNKS = 4

S5_W = 512
S5_GROUPS = 32
S5_H = 16
S5_P = 64
S5_LC = 8
S5_TILES = 4
S5_ROWS = 128

RET_HEADS = 4
RET_DK = 128
RET_DV = 256
RET_CHUNK = 256

FFN_HIDDEN = 2816
FFN_TN = 256
FFN_HALO = 16

VMEM_LIMIT = 48 * 1024 * 1024


def _dot(a, b):
    return jnp.dot(a, b, preferred_element_type=F32)


def _dot_nt(a, b):
    return lax.dot_general(a, b, (((1,), (1,)), ((), ())), preferred_element_type=F32)


def _dot_tn(a, b):
    return lax.dot_general(a, b, (((0,), (0,)), ((), ())), preferred_element_type=F32)


def _split3(x):
    hi = x.astype(BF16)
    r1 = x - hi.astype(F32)
    mid = r1.astype(BF16)
    lo = (r1 - mid.astype(F32)).astype(BF16)
    return hi, mid, lo


def _dot_sel_rhs(x, sel):
    hi, mid, lo = _split3(x)
    return _dot(hi, sel) + _dot(mid, sel) + _dot(lo, sel)


def _dot_sel_lhs(sel, x):
    hi, mid, lo = _split3(x)
    return _dot(sel, hi) + _dot(sel, mid) + _dot(sel, lo)


def _sigmoid(x):
    return 1.0 / (1.0 + jnp.exp(-x))


def _silu(x):
    return x * _sigmoid(x)


def _softplus(x):
    return jnp.maximum(x, 0.0) + jnp.log(1.0 + jnp.exp(-jnp.abs(x)))


def _cparams(sem):
    return pltpu.CompilerParams(dimension_semantics=sem, vmem_limit_bytes=VMEM_LIMIT)


def _proj_kernel(x_ref, g_ref, w_ref, o_ref):
    x = x_ref[...]
    y = x * lax.rsqrt(jnp.mean(x * x, axis=-1, keepdims=True) + EPS)
    hn = (y * g_ref[...]).astype(BF16)
    o_ref[...] = _dot(hn, w_ref[...])


def _norm_proj(x2, g, w, tm=256):
    T, D = x2.shape
    N = w.shape[1]
    return pl.pallas_call(
        _proj_kernel,
        grid=(T // tm,),
        in_specs=[pl.BlockSpec((tm, D), lambda i: (i, 0)),
                  pl.BlockSpec((1, D), lambda i: (0, 0)),
                  pl.BlockSpec((D, N), lambda i: (0, 0))],
        out_specs=pl.BlockSpec((tm, N), lambda i: (i, 0)),
        out_shape=jax.ShapeDtypeStruct((T, N), F32),
        compiler_params=_cparams(("parallel",)),
    )(x2, g.reshape(1, D), w)


def _ffn_kernel(x_ref, xh_ref, ya_ref, yah_ref, yb_ref, ybh_ref, wa_ref, wb_ref, g_ref, wup_ref,
                cw_ref, cb_ref, wd_ref, gf_ref, o_ref, act_scr, *, blocks_per_seq, final_norm):
    i = pl.program_id(0)
    Hd = wd_ref.shape[0]
    tn = FFN_TN

    def mixed(x, ya, yb):
        return x + _dot(ya.astype(BF16), wa_ref[...]) + _dot(yb.astype(BF16), wb_ref[...])

    def norm(x):
        y = x * lax.rsqrt(jnp.mean(x * x, axis=-1, keepdims=True) + EPS)
        return (y * g_ref[...]).astype(BF16)

    x1 = mixed(x_ref[...], ya_ref[...], yb_ref[...])
    keep = jnp.where(i % blocks_per_seq == 0, 0.0, 1.0)
    x1h = mixed(xh_ref[...], yah_ref[...], ybh_ref[...]) * keep
    hn = jnp.concatenate([norm(x1h), norm(x1)], axis=0)

    def conv(u, cols):
        u1 = pltpu.roll(u, 1, 0)
        u2 = pltpu.roll(u, 2, 0)
        c = cw_ref[2:3, cols] * u + cw_ref[1:2, cols] * u1 + cw_ref[0:1, cols] * u2 + cb_ref[:, cols]
        return c[FFN_HALO:, :]

    for j in range(Hd // tn):
        gs = slice(j * tn, (j + 1) * tn)
        vs = slice(Hd + j * tn, Hd + (j + 1) * tn)
        cg = conv(_dot(hn, wup_ref[:, gs]), gs)
        cv = conv(_dot(hn, wup_ref[:, vs]), vs)
        act_scr[:, gs] = (_silu(cg) * cv).astype(BF16)

    y = x1 + _dot(act_scr[...], wd_ref[...])
    if final_norm:
        y = y * lax.rsqrt(jnp.mean(y * y, axis=-1, keepdims=True) + EPS) * gf_ref[...]
    o_ref[...] = y


def _resident(shape):
    return pl.BlockSpec(shape, lambda *_: (0,) * len(shape), pipeline_mode=pl.Buffered(1))


def _mix_ffn(x2, ya, yb, wa, wb, seq_len, g, w_up, conv_w, conv_b, w_down, g_final, final_norm, tm=512):
    T, D = x2.shape
    Hd = w_down.shape[0]
    Ka, Kb = ya.shape[1], yb.shape[1]
    tm = min(tm, seq_len)
    hb = tm // FFN_HALO
    kern = functools.partial(_ffn_kernel, blocks_per_seq=seq_len // tm, final_norm=final_norm)

    def rows(width):
        return pl.BlockSpec((tm, width), lambda i: (i, 0))

    def halo(width):
        return pl.BlockSpec((FFN_HALO, width), lambda i: (jnp.maximum(i * hb - 1, 0), 0))

    return pl.pallas_call(
        kern,
        grid=(T // tm,),
        in_specs=[rows(D), halo(D), rows(Ka), halo(Ka), rows(Kb), halo(Kb),
                  _resident((Ka, D)), _resident((Kb, D)),
                  _resident((1, D)), _resident((D, 2 * Hd)), _resident((3, 2 * Hd)), _resident((1, 2 * Hd)),
                  _resident((Hd, D)), _resident((1, D))],
        out_specs=pl.BlockSpec((tm, D), lambda i: (i, 0)),
        out_shape=jax.ShapeDtypeStruct((T, D), F32),
        scratch_shapes=[pltpu.VMEM((tm, Hd), BF16)],
        compiler_params=_cparams(("parallel",)),
    )(x2, x2, ya, ya, yb, yb, wa, wb, g.reshape(1, D), w_up, conv_w, conv_b.reshape(1, -1), w_down,
      g_final.reshape(1, D))


def _rwkv_kernel(r_ref, k_ref, v_ref, zwa_ref, zg_ref, vec_ref, vec2_ref, w2a2_ref, g2_ref,
                 eseg_ref, csum_ref, o_ref, h_scr, prev_scr, prev2_scr, y_scr):
    c = pl.program_id(1)
    RB = r_ref.shape[0]
    C = RWKV_CHUNK
    N = RWKV_N

    @pl.when(c == 0)
    def _():
        h_scr[...] = jnp.zeros_like(h_scr)
        prev_scr[...] = jnp.zeros_like(prev_scr)
        prev2_scr[...] = jnp.zeros_like(prev2_scr)

    row = lax.broadcasted_iota(jnp.int32, (RB, 1), 0)

    def shift_lerp(x, prev_row, mu):
        xs = jnp.where(row == 0, prev_row, pltpu.roll(x, 1, 0))
        return x + (xs - x) * mu

    r_raw, k_raw, v_raw = r_ref[...], k_ref[...], v_ref[...]
    zwa_raw, zg_raw = zwa_ref[...], zg_ref[...]
    r = shift_lerp(r_raw, prev_scr[0:1, :], vec_ref[0:1, :])
    k = shift_lerp(k_raw, prev_scr[1:2, :], vec_ref[1:2, :])
    v = shift_lerp(v_raw, prev_scr[2:3, :], vec_ref[2:3, :])
    zwa = shift_lerp(zwa_raw, prev2_scr[0:1, :], vec2_ref[0:1, :])
    zg = shift_lerp(zg_raw, prev2_scr[1:2, :], vec2_ref[1:2, :])
    prev_scr[0:1, :] = r_raw[RB - 1:RB, :]
    prev_scr[1:2, :] = k_raw[RB - 1:RB, :]
    prev_scr[2:3, :] = v_raw[RB - 1:RB, :]
    prev2_scr[0:1, :] = zwa_raw[RB - 1:RB, :]
    prev2_scr[1:2, :] = zg_raw[RB - 1:RB, :]

    w0, a0 = vec_ref[3:4, :], vec_ref[4:5, :]
    k_k, k_a, r_k = vec_ref[5:6, :], vec_ref[6:7, :], vec_ref[7:8, :]
    ln_g, ln_b = vec_ref[8:9, :], vec_ref[9:10, :]

    lane = lax.broadcasted_iota(jnp.int32, (1, 128), 1)
    lora_in = jnp.where(lane < 64, jnp.tanh(zwa), zwa).astype(BF16)
    wa = _dot(lora_in, w2a2_ref[...])
    w_log = -_softplus(-(w0 + wa[:, :RWKV_W])) - 0.5
    logw = -jnp.exp(w_log)
    a = _sigmoid(a0 + wa[:, RWKV_W:])
    g = _dot(_sigmoid(zg).astype(BF16), g2_ref[...])

    eseg = eseg_ref[...]

    def segsum(x, split=True):
        EW = eseg.shape[0]
        hi = x.astype(BF16)
        lo = (x - hi.astype(F32)).astype(BF16) if split else None
        parts = []
        for q in range(RWKV_W // EW):
            qs = slice(q * EW, (q + 1) * EW)
            sq = _dot(hi[:, qs], eseg)
            parts.append(sq + _dot(lo[:, qs], eseg) if split else sq)
        return jnp.concatenate(parts, axis=1)

    kk = k * k_k
    kk = kk * lax.rsqrt(jnp.maximum(segsum(kk * kk), 1e-24))
    k2 = k * (1.0 + (a - 1.0) * k_a)

    cs = _dot_sel_lhs(csum_ref[...], logw)
    lg, lg_end = cs[:RB], cs[RB:]
    eg = jnp.exp(lg)
    egi = jnp.exp(-lg)
    g_end = jnp.exp(lg_end)
    d_end = jnp.exp(lg_end - lg)
    kka = kk * a
    ab = -kk * jnp.exp(lg - logw)
    rb = r * eg
    bt = kka * egi
    kt = k2 * egi
    bg = kka * d_end
    kg = k2 * d_end

    W2 = 2 * N
    left = lax.broadcasted_iota(jnp.int32, (1, W2), 1) < N

    def bd(x):
        return jnp.concatenate([jnp.where(left, x, 0.0), jnp.where(left, 0.0, x)], axis=0)

    ri = lax.broadcasted_iota(jnp.int32, (2 * W2, 2 * W2), 0)
    ci = lax.broadcasted_iota(jnp.int32, (2 * W2, 2 * W2), 1)
    mask = (ci & (N - 1)) < (ri & (N - 1)) + jnp.where(ri >= W2, 1, 0)
    e_r = lax.broadcasted_iota(jnp.int32, (W2, W2), 0)
    e_c = lax.broadcasted_iota(jnp.int32, (W2, W2), 1)
    eye = jnp.where(e_r == e_c, 1.0, 0.0)
    zeros_b = jnp.zeros((W2, W2), BF16)

    units = [(j, p) for j in range(RB // C) for p in range(RWKV_HEADS // 2)]
    U = len(units)

    def piece(x, j, p):
        return x[j * C:(j + 1) * C, p * W2:(p + 1) * W2]

    ab_u = [bd(piece(ab, j, p)).astype(BF16) for j, p in units]
    AR = [jnp.concatenate([ab_u[i], bd(piece(rb, j, p)).astype(BF16)], axis=0) for i, (j, p) in enumerate(units)]
    BK = [jnp.concatenate([bd(piece(bt, j, p)), bd(piece(kt, j, p))], axis=0).astype(BF16) for j, p in units]
    v_u = [bd(piece(v, j, p)).astype(BF16) for j, p in units]
    G = [jnp.where(mask, _dot_nt(AR[i], BK[i]), 0.0) for i in range(U)]
    def diag2(ya, yb):
        z = jnp.zeros_like(ya)
        return jnp.concatenate([jnp.concatenate([ya, z], axis=1), jnp.concatenate([z, yb], axis=1)], axis=0)

    def diag2_of(xx):
        return diag2(xx[:, :W2], xx[:, W2:])

    SU = U // 2
    AA = [jnp.concatenate([G[2 * s][:W2, :W2], G[2 * s + 1][:W2, :W2]], axis=1) for s in range(SU)]
    SS = [jnp.concatenate([eye, eye], axis=1) + AA[s] for s in range(SU)]
    AAb = [AA[s].astype(BF16) for s in range(SU)]
    QQ = [_dot(AAb[s], diag2_of(AAb[s])) for s in range(SU)]
    for _ in range(int(math.log2(C)) - 2):
        QQb = [QQ[s].astype(BF16) for s in range(SU)]
        QS = [_dot(jnp.concatenate([QQb[s], SS[s].astype(BF16)], axis=0), diag2_of(QQb[s])) for s in range(SU)]
        QQ = [QS[s][:W2] for s in range(SU)]
        SS = [SS[s] + QS[s][W2:] for s in range(SU)]
    SS = [(SS[s] + _dot(SS[s].astype(BF16), diag2_of(QQ[s].astype(BF16)))).astype(BF16) for s in range(SU)]
    S = [SS[i // 2][:, (i % 2) * W2:(i % 2 + 1) * W2] for i in range(U)]
    akv2 = [_dot(jnp.concatenate([G[2 * s][:W2, W2:], G[2 * s + 1][:W2, W2:]], axis=1).astype(BF16),
                 diag2(v_u[2 * s], v_u[2 * s + 1])) for s in range(SU)]
    akv = [akv2[i // 2][:, (i % 2) * W2:(i % 2 + 1) * W2] for i in range(U)]
    WU = [_dot(S[i], jnp.concatenate([ab_u[i], akv[i].astype(BF16)], axis=1)).astype(BF16) for i in range(U)]
    Z = [jnp.concatenate([WU[i], jnp.concatenate([zeros_b, v_u[i]], axis=1)], axis=0) for i in range(U)]
    QY = [_dot(G[i][W2:, :].astype(BF16), Z[i]) for i in range(U)]
    BKgT = [jnp.concatenate([bd(piece(bg, j, p)).T, bd(piece(kg, j, p)).T], axis=1).astype(BF16) for j, p in units]
    MN = [_dot(BKgT[i], Z[i]) for i in range(U)]
    QM = [jnp.concatenate([AR[i][W2:].astype(F32) + QY[i][:, :W2], MN[i][:, :W2]], axis=0).astype(BF16)
          for i in range(U)]
    gcol = [jnp.sum(eye * piece(g_end, j, p)[0:1, :], axis=1, keepdims=True) for j, p in units]

    for s in range(SU):
        ia, ib = 2 * s, 2 * s + 1
        Ha, Hb = h_scr[units[ia][1]], h_scr[units[ib][1]]
        YH2 = _dot(jnp.concatenate([QM[ia], QM[ib]], axis=1), diag2(Ha.astype(BF16), Hb.astype(BF16)))
        for i, H, YH in ((ia, Ha, YH2[:, :W2]), (ib, Hb, YH2[:, W2:])):
            j, p = units[i]
            ybd = YH[:W2] + QY[i][:, W2:]
            y_scr[j * C:(j + 1) * C, p * W2:(p + 1) * W2] = ybd[:C] + ybd[C:]
            h_scr[p] = H * gcol[i] + YH[W2:] + MN[i][:, W2:]

    y = y_scr[...]
    inv_n = 1.0 / N
    mean = segsum(y) * inv_n
    d = y - mean
    var = segsum(d * d, split=False) * inv_n
    yn = d * lax.rsqrt(var + RWKV_GN_EPS) * ln_g + ln_b
    bonus = segsum(r * k2 * r_k, split=False) * v
    o_ref[...] = ((yn + bonus) * g).astype(o_ref.dtype)


def _rwkv_group(p0, cols, batch, seq_len, vec, vec2, w2a2, g2):
    C = RWKV_CHUNK
    RB = C * RWKV_BLOCK_CHUNKS
    nc = seq_len // RB
    T = batch * seq_len
    cr, ck, cv, czwa, czg = cols
    eseg = jnp.asarray(np.arange(256)[:, None] // RWKV_N == np.arange(256)[None, :] // RWKV_N, BF16)
    same_chunk = np.arange(RB)[:, None] // C == np.arange(RB)[None, :] // C
    csum = np.concatenate([np.tril(np.ones((RB, RB))) * same_chunk, same_chunk], axis=0)
    csum = jnp.asarray(csum, BF16)

    def col(width, idx):
        return pl.BlockSpec((RB, width), lambda b, c: (b * nc + c, idx))

    def full(arr):
        return pl.BlockSpec(arr.shape, lambda b, c: (0,) * arr.ndim)

    return pl.pallas_call(
        _rwkv_kernel,
        grid=(batch, nc),
        in_specs=[col(RWKV_W, cr), col(RWKV_W, ck), col(RWKV_W, cv), col(128, czwa), col(128, czg),
                  full(vec), full(vec2), full(w2a2), full(g2), full(eseg), full(csum)],
        out_specs=pl.BlockSpec((RB, RWKV_W), lambda b, c: (b * nc + c, 0)),
        out_shape=jax.ShapeDtypeStruct((T, RWKV_W), BF16),
        scratch_shapes=[pltpu.VMEM((RWKV_HEADS // 2, 2 * RWKV_N, 2 * RWKV_N), F32),
                        pltpu.VMEM((8, RWKV_W), F32),
                        pltpu.VMEM((8, 128), F32),
                        pltpu.VMEM((RB, RWKV_W), F32)],
        compiler_params=_cparams(("arbitrary", "arbitrary")),
    )(p0, p0, p0, p0, p0, vec, vec2, w2a2, g2, eseg, csum)


def _ssd_kernel(z_ref, xs_ref, bc_ref, dt_ref, cwx_ref, cwbc_ref, vecx_ref, vecbc_ref, vecd_ref,
                eh_ref, ltri_ref, o_ref, st_scr, tailx_scr, tailbc_scr, y_scr):
    c = pl.program_id(1)
    RB = xs_ref.shape[0]
    Q = SSD_CHUNK
    P, N = SSD_P, SSD_N
    GW = SSD_W // SSD_GROUPS

    @pl.when(c == 0)
    def _():
        st_scr[...] = jnp.zeros_like(st_scr)
        tailx_scr[...] = jnp.zeros_like(tailx_scr)
        tailbc_scr[...] = jnp.zeros_like(tailbc_scr)

    def conv_silu(x_raw, tail_scr, cw_ref, b):
        ext = jnp.concatenate([tail_scr[...], x_raw], axis=0)
        acc = b + cw_ref[SSD_CONV - 1:SSD_CONV, :] * x_raw
        for s in range(1, SSD_CONV):
            acc = acc + cw_ref[SSD_CONV - 1 - s:SSD_CONV - s, :] * pltpu.roll(ext, s, 0)[8:, :]
        tail_scr[...] = x_raw[RB - 8:, :]
        return _silu(acc)

    xs = conv_silu(xs_ref[...], tailx_scr, cwx_ref, vecx_ref[0:1, :])
    bc = conv_silu(bc_ref[...], tailbc_scr, cwbc_ref, vecbc_ref[0:1, :])

    eh = eh_ref[...]
    dt = _softplus(dt_ref[...] + vecd_ref[0:1, :])
    a = dt * vecd_ref[1:2, :]
    acum = _dot_sel_lhs(ltri_ref[...], a)
    dt_e = _dot_sel_rhs(dt, eh)
    acum_e = _dot_sel_rhs(acum, eh)
    acum_t = acum.T
    xdt = xs * dt_e
    xdt_b = xdt.astype(BF16)
    ea = jnp.exp(acum_e)
    bc_b = bc.astype(BF16)

    li = lax.broadcasted_iota(jnp.int32, (Q, Q), 0)
    si = lax.broadcasted_iota(jnp.int32, (Q, Q), 1)
    causal = si <= li

    for ch in range(RB // Q):
        rs = slice(ch * Q, (ch + 1) * Q)
        a_last = acum_e[(ch + 1) * Q - 1:(ch + 1) * Q, :]
        xdte = (xdt[rs] * jnp.exp(a_last - acum_e[rs])).astype(BF16)
        chunk_decay = jnp.exp(a_last)
        for gi in range(SSD_GROUPS):
            gs = slice(gi * GW, (gi + 1) * GW)
            Bg = bc_b[rs, gi * N:(gi + 1) * N]
            Cg = bc_b[rs, SSD_GROUPS * N + gi * N:SSD_GROUPS * N + (gi + 1) * N]
            cb = _dot_nt(Cg, Bg)
            prev = st_scr[gi]
            y_off = _dot(Cg, prev.astype(BF16)) * ea[rs, gs]
            st_scr[gi] = prev * chunk_decay[:, gs] + _dot(bc[rs, gi * N:(gi + 1) * N].T.astype(BF16), xdte[:, gs])
            ys = []
            for j in range(SSD_HEADS // SSD_GROUPS):
                hh = gi * (SSD_HEADS // SSD_GROUPS) + j
                hs = slice(hh * P, (hh + 1) * P)
                seg = acum_e[rs, hs] - acum_t[hh:hh + 1, rs]
                wm = jnp.where(causal, cb * jnp.exp(seg), 0.0).astype(BF16)
                ys.append(_dot(wm, xdt_b[rs, hs]))
            y_scr[rs, gs] = jnp.concatenate(ys, axis=1) + y_off

    y = (y_scr[...] + vecx_ref[2:3, :] * xs) * _silu(z_ref[...])
    outs = []
    for gi in range(SSD_GROUPS):
        yg = y[:, gi * GW:(gi + 1) * GW]
        outs.append(yg * lax.rsqrt(jnp.mean(yg * yg, axis=-1, keepdims=True) + EPS))
    o_ref[...] = (jnp.concatenate(outs, axis=-1) * vecx_ref[1:2, :]).astype(o_ref.dtype)


def _ssd_group(p0, cols, batch, seq_len, cwx, cwbc, vecx, vecbc, vecd):
    Q = SSD_CHUNK
    assert Q == SSD_P
    RB = Q * SSD_BLOCK_CHUNKS
    nc = seq_len // RB
    T = batch * seq_len
    cz, cx, cbc, cdt = cols
    eh = np.zeros((128, SSD_W), np.float32)
    for hh in range(SSD_HEADS):
        eh[hh, hh * SSD_P:(hh + 1) * SSD_P] = 1.0
    eh = jnp.asarray(eh, BF16)
    same_chunk = np.arange(RB)[:, None] // Q == np.arange(RB)[None, :] // Q
    ltri = jnp.asarray(np.tril(np.ones((RB, RB))) * same_chunk, BF16)

    def col(width, idx):
        return pl.BlockSpec((RB, width), lambda b, c: (b * nc + c, idx))

    def full(arr):
        return pl.BlockSpec(arr.shape, lambda b, c: (0,) * arr.ndim)

    return pl.pallas_call(
        _ssd_kernel,
        grid=(batch, nc),
        in_specs=[col(SSD_W, cz), col(SSD_W, cx), col(512, cbc), col(128, cdt),
                  full(cwx), full(cwbc), full(vecx), full(vecbc), full(vecd), full(eh), full(ltri)],
        out_specs=pl.BlockSpec((RB, SSD_W), lambda b, c: (b * nc + c, 0)),
        out_shape=jax.ShapeDtypeStruct((T, SSD_W), BF16),
        scratch_shapes=[pltpu.VMEM((SSD_GROUPS, SSD_N, SSD_W // SSD_GROUPS), F32),
                        pltpu.VMEM((8, SSD_W), F32),
                        pltpu.VMEM((8, 512), F32),
                        pltpu.VMEM((RB, SSD_W), F32)],
        compiler_params=_cparams(("arbitrary", "arbitrary")),
    )(p0, p0, p0, p0, cwx, cwbc, vecx, vecbc, vecd, eh, ltri)


def _s5_kernel(u_ref, toep_ref, bz_ref, cz_ref, lam_ref, d_ref, wglu_ref, bglu_ref, o_ref,
               st_scr, zre_scr, zim_scr, sre_scr, sim_scr, y_scr, fold_scr):
    i = pl.program_id(1)
    Mb = u_ref.shape[0]
    LC, NT = S5_LC, S5_TILES
    HW = S5_W // NT * (S5_P // S5_H)

    @pl.when(i == 0)
    def _():
        st_scr[...] = jnp.zeros_like(st_scr)

    u = u_ref[...]
    ucat = []
    for n in range(NT):
        un = jnp.concatenate([u[:, l * S5_W + n * 128:l * S5_W + (n + 1) * 128] for l in range(LC)],
                             axis=1).astype(BF16)
        ucat.append(un)
        z = _dot(un, bz_ref[n])
        zre_scr[:, n * HW:(n + 1) * HW] = z[:, :HW]
        zim_scr[:, n * HW:(n + 1) * HW] = z[:, HW:]

    lr = lam_ref[0:1, :]
    lim = lam_ref[1:2, :]

    def step(m, carry):
        sre, sim = carry
        sre_scr[pl.ds(m, 1), :] = sre
        sim_scr[pl.ds(m, 1), :] = sim
        zr = zre_scr[pl.ds(m, 1), :]
        zi = zim_scr[pl.ds(m, 1), :]
        return lr * sre - lim * sim + zr, lr * sim + lim * sre + zi

    sre, sim = lax.fori_loop(0, Mb, step, (st_scr[0:1, :], st_scr[1:2, :]))
    st_scr[0:1, :] = sre
    st_scr[1:2, :] = sim

    for n in range(NT):
        s_re = sre_scr[:, n * HW:(n + 1) * HW].astype(BF16)
        s_im = sim_scr[:, n * HW:(n + 1) * HW].astype(BF16)
        yn = _dot(ucat[n], toep_ref[n]) + _dot(s_re, cz_ref[n, :HW, :]) + _dot(s_im, cz_ref[n, HW:, :])
        for l in range(LC):
            y_scr[:, l * S5_W + n * 128:l * S5_W + (n + 1) * 128] = yn[:, l * 128:(l + 1) * 128]

    y = y_scr[...] + d_ref[...] * u
    y = 0.5 * y * (1.0 + jnp.tanh(math.sqrt(2.0 / math.pi) * (y + 0.044715 * (y * y * y))))
    wglu = wglu_ref[...]
    for l in range(LC):
        yl = y[:, l * S5_W:(l + 1) * S5_W]
        gate = _sigmoid(_dot(yl.astype(BF16), wglu) + bglu_ref[...])
        out_l = yl * gate
        for n in range(NT):
            fold_scr[n, pl.ds(l, Mb, stride=LC), :] = out_l[:, n * 128:(n + 1) * 128]
    for n in range(NT):
        o_ref[:, n * 128:(n + 1) * 128] = fold_scr[n]


def _s5_tables(lam_re, lam_im, log_dt, b_re, b_im, c_re, c_im):
    LC, NT, G, H, Pn = S5_LC, S5_TILES, S5_GROUPS, S5_H, S5_P
    GL = G // NT
    hp = lax.Precision.HIGHEST
    dt = jnp.exp(log_dt)[:, None]
    ar, ai = lam_re * dt, lam_im * dt
    jj = jnp.arange(LC + 1, dtype=F32)[:, None, None]
    mag = jnp.exp(jj * ar[None])
    pw_re, pw_im = mag * jnp.cos(jj * ai[None]), mag * jnp.sin(jj * ai[None])
    nr, ni = pw_re[1] - 1.0, pw_im[1]
    den = lam_re * lam_re + lam_im * lam_im
    qr, qi = (nr * lam_re + ni * lam_im) / den, (ni * lam_re - nr * lam_im) / den
    bb_re = qr[..., None] * b_re - qi[..., None] * b_im
    bb_im = qr[..., None] * b_im + qi[..., None] * b_re
    cl_re = c_re[None] * pw_re[:, :, None, :] - c_im[None] * pw_im[:, :, None, :]
    cl_im = c_re[None] * pw_im[:, :, None, :] + c_im[None] * pw_re[:, :, None, :]
    kd = (jnp.einsum("jghp,gpk->jghk", cl_re[:LC], bb_re, precision=hp)
          - jnp.einsum("jghp,gpk->jghk", cl_im[:LC], bb_im, precision=hp))
    def spread(x, rows_per_group, cols_per_group):
        sel = np.tile(np.eye(cols_per_group, dtype=np.float32), (1, GL))
        keep = (np.arange(GL * rows_per_group)[:, None] // rows_per_group
                == np.arange(GL * cols_per_group)[None, :] // cols_per_group)
        return jnp.where(keep, jnp.einsum("...rc,cd->...rd", x, jnp.asarray(sel), precision=hp), 0.0)

    blk = spread(jnp.swapaxes(kd, -1, -2).reshape(LC, NT, GL * H, H), H, H).astype(BF16)
    zp = jnp.concatenate([jnp.zeros((LC - 1,) + blk.shape[1:], BF16), blk], axis=0)
    toep = jnp.concatenate(
        [jnp.concatenate([zp[lo - li + LC - 1] for lo in range(LC)], axis=-1) for li in range(LC)], axis=1)
    def cz_part(cl):
        x = spread(jnp.swapaxes(cl[1:], -1, -2).reshape(LC, NT, GL * Pn, H), Pn, H).astype(BF16)
        return jnp.concatenate([x[l] for l in range(LC)], axis=-1)
    cz = jnp.concatenate([cz_part(cl_re), cz_part(-cl_im)], axis=1)
    jr = (LC - 1.0) - jnp.arange(LC, dtype=F32)[:, None, None]
    mag_r = jnp.exp(jr * ar[None])
    rev_re, rev_im = mag_r * jnp.cos(jr * ai[None]), mag_r * jnp.sin(jr * ai[None])
    bl_re = rev_re[..., None] * bb_re[None] - rev_im[..., None] * bb_im[None]
    bl_im = rev_re[..., None] * bb_im[None] + rev_im[..., None] * bb_re[None]

    def bz_part(bl):
        x = spread(jnp.swapaxes(bl, -1, -2).reshape(LC, NT, GL * H, Pn), H, Pn).astype(BF16)
        return jnp.swapaxes(x, 0, 1).reshape(NT, LC * GL * H, GL * Pn)
    bz = jnp.concatenate([bz_part(bl_re), bz_part(bl_im)], axis=2)
    lam = _rows([pw_re[LC].reshape(-1), pw_im[LC].reshape(-1)], 8)
    return toep, bz, cz, lam


def _s5_group(p1, col_u, batch, seq_len, tables, d_skip, w_glu, b_glu):
    LC = S5_LC
    T = batch * seq_len
    toep, bz, cz, lam = tables
    rows_per_seq = seq_len // LC
    Mb = min(S5_ROWS, rows_per_seq)
    nb = rows_per_seq // Mb
    NPc = p1.shape[1]
    W = LC * S5_W
    d_t = jnp.tile(d_skip.reshape(1, S5_W), (1, LC))
    SW = S5_TILES * (S5_GROUPS // S5_TILES) * S5_P

    def full(arr):
        return pl.BlockSpec(arr.shape, lambda b, i: (0,) * arr.ndim)

    def kern(u_ref, *rest):
        ucat_scr, fold_scr = rest[-2], rest[-1]
        for n in range(S5_TILES):
            fold_scr[n] = u_ref[:, n * 128:(n + 1) * 128]
        for n in range(S5_TILES):
            for l in range(LC):
                ucat_scr[:, l * S5_W + n * 128:l * S5_W + (n + 1) * 128] = fold_scr[n, pl.ds(l, Mb, stride=LC), :]
        _s5_kernel(ucat_scr, *rest[:-2], fold_scr)

    return pl.pallas_call(
        kern,
        grid=(batch, nb),
        in_specs=[pl.BlockSpec((Mb * LC, S5_W), lambda b, i: (b * nb + i, col_u)),
                  _resident(toep.shape), _resident(bz.shape), _resident(cz.shape), full(lam), full(d_t),
                  full(w_glu),
                  pl.BlockSpec((1, S5_W), lambda b, i: (0, 0))],
        out_specs=pl.BlockSpec((Mb * LC, S5_W), lambda b, i: (b * nb + i, 0)),
        out_shape=jax.ShapeDtypeStruct((T, S5_W), F32),
        scratch_shapes=[pltpu.VMEM((8, SW), F32),
                        pltpu.VMEM((Mb, SW), F32), pltpu.VMEM((Mb, SW), F32),
                        pltpu.VMEM((Mb, SW), F32), pltpu.VMEM((Mb, SW), F32),
                        pltpu.VMEM((Mb, W), F32),
                        pltpu.VMEM((Mb, W), F32),
                        pltpu.VMEM((S5_TILES, Mb * LC, 128), F32)],
        compiler_params=_cparams(("arbitrary", "arbitrary")),
    )(p1, toep, bz, cz, lam, d_t, w_glu, b_glu.reshape(1, S5_W))


def _ret_kernel(q_ref, k_ref, v_ref, g_ref, cos_ref, sin_ref, intra_ref, kdec_ref, qdec_ref, o_ref,
                st_scr, *, chunk_decay):
    c = pl.program_id(1)
    DK, DV = RET_DK, RET_DV

    @pl.when(c == 0)
    def _():
        st_scr[...] = jnp.zeros_like(st_scr)

    cosf = cos_ref[...]
    sinf = sin_ref[...]
    q = q_ref[...]
    k = k_ref[...]
    v = v_ref[...]
    g = g_ref[...]
    scale = DK ** -0.5
    for h in range(RET_HEADS):
        ks = slice(h * DK, (h + 1) * DK)
        vs = slice(h * DV, (h + 1) * DV)
        qh = q[:, ks]
        kh = k[:, ks]
        qh = qh * cosf + pltpu.roll(qh, DK // 2, 1) * sinf
        kh = (kh * cosf + pltpu.roll(kh, DK // 2, 1) * sinf) * scale
        vh = v[:, vs].astype(BF16)
        scores = (_dot_nt(qh.astype(BF16), kh.astype(BF16)) * intra_ref[h]).astype(BF16)
        prev = st_scr[h]
        y = _dot(scores, vh) + _dot((qh * qdec_ref[:, ks]).astype(BF16), prev.astype(BF16))
        kd = (kh * kdec_ref[:, ks]).T.astype(BF16)
        st_scr[h] = prev * chunk_decay[h] + _dot(kd, vh)
        y = y * lax.rsqrt(jnp.mean(y * y, axis=-1, keepdims=True) + EPS)
        o_ref[:, vs] = (_silu(g[:, vs]) * y).astype(o_ref.dtype)


def _ret_group(p1, cols, batch, seq_len):
    Q = min(RET_CHUNK, seq_len)
    nc = seq_len // Q
    T = batch * seq_len
    cq, ck, cv, cg = cols
    H, DK, DV = RET_HEADS, RET_DK, RET_DV
    half = DK // 2
    theta = 1.0 / (10000.0 ** jnp.linspace(0.0, 1.0, half, dtype=F32))
    ang = jnp.arange(seq_len).astype(F32)[:, None] * theta[None, :]
    cos, sin = jnp.cos(ang), jnp.sin(ang)
    cosf = jnp.concatenate([cos, cos], axis=1)
    sinf = jnp.concatenate([-sin, sin], axis=1)
    log_gamma = jnp.log(1.0 - 2.0 ** (-5.0 - jnp.arange(H, dtype=F32)))
    t = jnp.arange(Q, dtype=F32)
    diff = t[:, None] - t[None, :]
    intra = jnp.where((diff >= 0)[None], jnp.exp(jnp.maximum(diff, 0.0)[None] * log_gamma[:, None, None]), 0.0)
    kdec = jnp.repeat(jnp.exp((Q - 1.0 - t)[:, None] * log_gamma[None, :]), DK, axis=1)
    qdec = jnp.repeat(jnp.exp((t + 1.0)[:, None] * log_gamma[None, :]), DK, axis=1)
    chunk_decay = tuple(float((1.0 - 2.0 ** (-5.0 - h)) ** Q) for h in range(H))

    def col(width, idx):
        return pl.BlockSpec((Q, width), lambda b, c: (b * nc + c, idx))

    def full(arr):
        return pl.BlockSpec(arr.shape, lambda b, c: (0,) * arr.ndim)

    return pl.pallas_call(
        functools.partial(_ret_kernel, chunk_decay=chunk_decay),
        grid=(batch, nc),
        in_specs=[col(H * DK, cq), col(H * DK, ck), col(H * DV, cv), col(H * DV, cg),
                  pl.BlockSpec((Q, DK), lambda b, c: (c, 0)), pl.BlockSpec((Q, DK), lambda b, c: (c, 0)),
                  full(intra), full(kdec), full(qdec)],
        out_specs=pl.BlockSpec((Q, H * DV), lambda b, c: (b * nc + c, 0)),
        out_shape=jax.ShapeDtypeStruct((T, H * DV), BF16),
        scratch_shapes=[pltpu.VMEM((H, DK, DV), F32)],
        compiler_params=_cparams(("arbitrary", "arbitrary")),
    )(p1, p1, p1, p1, cosf, sinf, intra, kdec, qdec)


def _pad_cols(w, n):
    return jnp.pad(w, ((0, 0), (0, n - w.shape[1])))


def _rows(vectors, n_rows):
    width = vectors[0].shape[0]
    pad = [jnp.zeros((n_rows - len(vectors), width), F32)] if n_rows > len(vectors) else []
    return jnp.concatenate([v.astype(F32).reshape(1, width) for v in vectors] + pad, axis=0)


def kernel(x, l0_norm_mix, l0_w_in, l0_rwkv_mu, l0_rwkv_w0, l0_rwkv_w2, l0_rwkv_a0, l0_rwkv_a2, l0_rwkv_g2, l0_rwkv_k_k, l0_rwkv_k_a, l0_rwkv_r_k, l0_rwkv_ln_g, l0_rwkv_ln_b, l0_ssd_conv_w, l0_ssd_conv_b, l0_ssd_dt_bias, l0_ssd_a_log, l0_ssd_d, l0_ssd_norm_g, l0_w_out, l0_norm_ffn, l0_ffn_up, l0_ffn_conv_w, l0_ffn_conv_b, l0_ffn_down, l1_norm_mix, l1_w_in, l1_s5_lam_re, l1_s5_lam_im, l1_s5_log_dt, l1_s5_b_re, l1_s5_b_im, l1_s5_c_re, l1_s5_c_im, l1_s5_d, l1_s5_w_glu, l1_s5_b_glu, l1_w_out, l1_norm_ffn, l1_ffn_up, l1_ffn_conv_w, l1_ffn_conv_b, l1_ffn_down, final_norm):
    B, L, D = x.shape
    T = B * L
    x2 = x.reshape(T, D).astype(F32)

    RP = 3 * RWKV_W + 256
    o_z = RP
    o_xbc = o_z + SSD_W
    o_dt = o_xbc + SSD_W + 2 * SSD_GROUPS * SSD_N
    w = l0_w_in
    w0p = jnp.concatenate([w[:, o_z:o_z + SSD_W], w[:, o_xbc:o_dt], w[:, :RP], w[:, o_dt:]], axis=1)
    NP0 = 4608
    w0p = _pad_cols(w0p, NP0).astype(BF16)
    p0 = _norm_proj(x2, l0_norm_mix, w0p)

    mu = l0_rwkv_mu
    vec = _rows([mu[0:512], mu[512:1024], mu[1024:1536], l0_rwkv_w0, l0_rwkv_a0, l0_rwkv_k_k, l0_rwkv_k_a,
                 l0_rwkv_r_k.reshape(-1), l0_rwkv_ln_g, l0_rwkv_ln_b], 16)
    vec2 = _rows([mu[1536:1664], mu[1664:1792]], 8)
    zl = jnp.zeros((64, RWKV_W), F32)
    w2a2 = jnp.concatenate([jnp.concatenate([l0_rwkv_w2.astype(F32), zl], axis=1),
                            jnp.concatenate([zl, l0_rwkv_a2.astype(F32)], axis=1)], axis=0).astype(BF16)
    y_a = _rwkv_group(p0, (5, 6, 7, 32, 33), B, L, vec, vec2, w2a2, l0_rwkv_g2.astype(BF16))

    cw = l0_ssd_conv_w
    cb = l0_ssd_conv_b
    cwx = _rows([cw[k, :SSD_W] for k in range(SSD_CONV)], 8)
    cwbc = _rows([cw[k, SSD_W:] for k in range(SSD_CONV)], 8)
    vecx = _rows([cb[:SSD_W], l0_ssd_norm_g, jnp.repeat(l0_ssd_d, SSD_P)], 8)
    vecbc = _rows([cb[SSD_W:]], 8)
    hpad = jnp.zeros((128 - SSD_HEADS,), F32)
    vecd = _rows([jnp.concatenate([l0_ssd_dt_bias.astype(F32), hpad]),
                  jnp.concatenate([-jnp.exp(l0_ssd_a_log.astype(F32)), hpad])], 8)
    y_b = _ssd_group(p0, (0, 1, 4, 34), B, L, cwx, cwbc, vecx, vecbc, vecd)

    wo = l0_w_out.astype(BF16)
    x2 = _mix_ffn(x2, y_a, y_b, wo[:RWKV_W], wo[RWKV_W:], L, l0_norm_ffn, l0_ffn_up.astype(BF16),
                  l0_ffn_conv_w.astype(F32), l0_ffn_conv_b.astype(F32), l0_ffn_down.astype(BF16),
                  final_norm, False)

    w = l1_w_in
    w1p = jnp.concatenate([w[:, 1536:2560], w[:, 2560:3584], w[:, :1536]], axis=1).astype(BF16)
    p1 = _norm_proj(x2, l1_norm_mix, w1p)
    tables = _s5_tables(*(t.astype(F32) for t in (l1_s5_lam_re, l1_s5_lam_im, l1_s5_log_dt, l1_s5_b_re,
                                                   l1_s5_b_im, l1_s5_c_re, l1_s5_c_im)))
    y_c = _s5_group(p1, 4, B, L, tables, l1_s5_d.astype(F32), l1_s5_w_glu.astype(BF16),
                    l1_s5_b_glu.astype(F32))
    y_d = _ret_group(p1, (5, 6, 0, 1), B, L)
    wo = l1_w_out.astype(BF16)
    x2 = _mix_ffn(x2, y_c, y_d, wo[:S5_W], wo[S5_W:], L, l1_norm_ffn, l1_ffn_up.astype(BF16),
                  l1_ffn_conv_w.astype(F32), l1_ffn_conv_b.astype(F32), l1_ffn_down.astype(BF16),
                  final_norm, True)
    return x2.reshape(B, L, D).astype(x.dtype)
```

```python
import functools
import math

import numpy as np
import jax
import jax.numpy as jnp
from jax import lax
from jax.experimental import pallas as pl
from jax.experimental.pallas import tpu as pltpu

F32 = jnp.float32
BF16 = jnp.bfloat16

EPS = 1e-6
D_MODEL = 1024

RWKV_HEADS = 8
RWKV_N = 64
RWKV_W = 512
RWKV_GN_EPS = 64e-5
RWKV_CHUNK = 64
RWKV_BLOCK_CHUNKS = 4

SSD_HEADS = 16
SSD_P = 64
SSD_GROUPS = 2
SSD_N = 128
SSD_W = 1024
SSD_CONV = 4
SSD_CHUNK = 64
SSD_BLOCK_CHUNKS = 4

S5_W = 512
S5_GROUPS = 32
S5_H = 16
S5_P = 64
S5_LC = 8
S5_TILES = 4
S5_ROWS = 128

RET_HEADS = 4
RET_DK = 128
RET_DV = 256
RET_CHUNK = 256
RET_BLOCK_CHUNKS = 2

FFN_HIDDEN = 2816
FFN_TN = 256
FFN_HALO = 16

VMEM_LIMIT = 48 * 1024 * 1024


def _dot(a, b):
    return jnp.dot(a, b, preferred_element_type=F32)


def _dot_nt(a, b):
    return lax.dot_general(a, b, (((1,), (1,)), ((), ())), preferred_element_type=F32)


def _dot_tn(a, b):
    return lax.dot_general(a, b, (((0,), (0,)), ((), ())), preferred_element_type=F32)


def _split3(x):
    hi = x.astype(BF16)
    r1 = x - hi.astype(F32)
    mid = r1.astype(BF16)
    lo = (r1 - mid.astype(F32)).astype(BF16)
    return hi, mid, lo


def _dot_sel_rhs(x, sel):
    hi, mid, lo = _split3(x)
    return _dot(hi, sel) + _dot(mid, sel) + _dot(lo, sel)


def _dot_sel_lhs(sel, x):
    hi, mid, lo = _split3(x)
    return _dot(sel, hi) + _dot(sel, mid) + _dot(sel, lo)


def _sigmoid(x):
    return 1.0 / (1.0 + jnp.exp(-x))


def _silu(x):
    return x * _sigmoid(x)


def _softplus(x):
    return jnp.maximum(x, 0.0) + jnp.log(1.0 + jnp.exp(-jnp.abs(x)))


def _cparams(sem):
    return pltpu.CompilerParams(dimension_semantics=sem, vmem_limit_bytes=VMEM_LIMIT)


def _proj_kernel(x_ref, g_ref, w_ref, o_ref):
    x = x_ref[...]
    y = x * lax.rsqrt(jnp.mean(x * x, axis=-1, keepdims=True) + EPS)
    hn = (y * g_ref[...]).astype(BF16)
    o_ref[...] = _dot(hn, w_ref[...])


def _norm_proj(x2, g, w, tm=512):
    T, D = x2.shape
    N = w.shape[1]
    return pl.pallas_call(
        _proj_kernel,
        grid=(T // tm,),
        in_specs=[pl.BlockSpec((tm, D), lambda i: (i, 0)), _resident((1, D)), _resident((D, N))],
        out_specs=pl.BlockSpec((tm, N), lambda i: (i, 0)),
        out_shape=jax.ShapeDtypeStruct((T, N), F32),
        compiler_params=_cparams(("parallel",)),
    )(x2, g.reshape(1, D), w)


def _ffn_kernel(x_ref, xh_ref, ya_ref, yah_ref, yb_ref, ybh_ref, wa_ref, wb_ref, g_ref, wup_ref,
                cw_ref, cb_ref, wd_ref, gf_ref, o_ref, act_scr, *, blocks_per_seq, final_norm):
    i = pl.program_id(0)
    Hd = wd_ref.shape[0]
    tn = FFN_TN

    def mixed(x, ya, yb):
        return x + _dot(ya.astype(BF16), wa_ref[...]) + _dot(yb.astype(BF16), wb_ref[...])

    def norm(x):
        y = x * lax.rsqrt(jnp.mean(x * x, axis=-1, keepdims=True) + EPS)
        return (y * g_ref[...]).astype(BF16)

    x1 = mixed(x_ref[...], ya_ref[...], yb_ref[...])
    keep = jnp.where(i % blocks_per_seq == 0, 0.0, 1.0)
    x1h = mixed(xh_ref[...], yah_ref[...], ybh_ref[...]) * keep
    hn = jnp.concatenate([norm(x1h), norm(x1)], axis=0)

    def conv(u, cols):
        u1 = pltpu.roll(u, 1, 0)
        u2 = pltpu.roll(u, 2, 0)
        c = cw_ref[2:3, cols] * u + cw_ref[1:2, cols] * u1 + cw_ref[0:1, cols] * u2 + cb_ref[:, cols]
        return c[FFN_HALO:, :]

    for j in range(Hd // tn):
        gs = slice(j * tn, (j + 1) * tn)
        vs = slice(Hd + j * tn, Hd + (j + 1) * tn)
        cg = conv(_dot(hn, wup_ref[:, gs]), gs)
        cv = conv(_dot(hn, wup_ref[:, vs]), vs)
        act_scr[:, gs] = (_silu(cg) * cv).astype(BF16)

    y = x1 + _dot(act_scr[...], wd_ref[...])
    if final_norm:
        y = y * lax.rsqrt(jnp.mean(y * y, axis=-1, keepdims=True) + EPS) * gf_ref[...]
    o_ref[...] = y


def _resident(shape):
    return pl.BlockSpec(shape, lambda *_: (0,) * len(shape), pipeline_mode=pl.Buffered(1))


def _mix_ffn(x2, ya, yb, wa, wb, seq_len, g, w_up, conv_w, conv_b, w_down, g_final, final_norm, tm=512):
    T, D = x2.shape
    Hd = w_down.shape[0]
    Ka, Kb = ya.shape[1], yb.shape[1]
    tm = min(tm, seq_len)
    hb = tm // FFN_HALO
    kern = functools.partial(_ffn_kernel, blocks_per_seq=seq_len // tm, final_norm=final_norm)

    def rows(width):
        return pl.BlockSpec((tm, width), lambda i: (i, 0))

    def halo(width):
        return pl.BlockSpec((FFN_HALO, width), lambda i: (jnp.maximum(i * hb - 1, 0), 0))

    return pl.pallas_call(
        kern,
        grid=(T // tm,),
        in_specs=[rows(D), halo(D), rows(Ka), halo(Ka), rows(Kb), halo(Kb),
                  _resident((Ka, D)), _resident((Kb, D)),
                  _resident((1, D)), _resident((D, 2 * Hd)), _resident((3, 2 * Hd)), _resident((1, 2 * Hd)),
                  _resident((Hd, D)), _resident((1, D))],
        out_specs=pl.BlockSpec((tm, D), lambda i: (i, 0)),
        out_shape=jax.ShapeDtypeStruct((T, D), F32),
        scratch_shapes=[pltpu.VMEM((tm, Hd), BF16)],
        compiler_params=_cparams(("parallel",)),
    )(x2, x2, ya, ya, yb, yb, wa, wb, g.reshape(1, D), w_up, conv_w, conv_b.reshape(1, -1), w_down,
      g_final.reshape(1, D))


def _rwkv_body(r_ref, k_ref, v_ref, zwa_ref, zg_ref, vec_ref, vec2_ref, w2a2_ref, g2_ref,
               eseg_ref, csum_ref, o_ref, h_scr, prev_scr, prev2_scr, y_scr):
    RB = r_ref.shape[0]
    C = RWKV_CHUNK
    N = RWKV_N

    row = lax.broadcasted_iota(jnp.int32, (RB, 1), 0)

    def shift_lerp(x, prev_row, mu):
        xs = jnp.where(row == 0, prev_row, pltpu.roll(x, 1, 0))
        return x + (xs - x) * mu

    r_raw, k_raw, v_raw = r_ref[...], k_ref[...], v_ref[...]
    zwa_raw, zg_raw = zwa_ref[...], zg_ref[...]
    r = shift_lerp(r_raw, prev_scr[0:1, :], vec_ref[0:1, :])
    k = shift_lerp(k_raw, prev_scr[1:2, :], vec_ref[1:2, :])
    v = shift_lerp(v_raw, prev_scr[2:3, :], vec_ref[2:3, :])
    zwa = shift_lerp(zwa_raw, prev2_scr[0:1, :], vec2_ref[0:1, :])
    zg = shift_lerp(zg_raw, prev2_scr[1:2, :], vec2_ref[1:2, :])
    prev_scr[0:1, :] = r_raw[RB - 1:RB, :]
    prev_scr[1:2, :] = k_raw[RB - 1:RB, :]
    prev_scr[2:3, :] = v_raw[RB - 1:RB, :]
    prev2_scr[0:1, :] = zwa_raw[RB - 1:RB, :]
    prev2_scr[1:2, :] = zg_raw[RB - 1:RB, :]

    w0, a0 = vec_ref[3:4, :], vec_ref[4:5, :]
    k_k, k_a, r_k = vec_ref[5:6, :], vec_ref[6:7, :], vec_ref[7:8, :]
    ln_g, ln_b = vec_ref[8:9, :], vec_ref[9:10, :]

    lane = lax.broadcasted_iota(jnp.int32, (1, 128), 1)
    lora_in = jnp.where(lane < 64, jnp.tanh(zwa), zwa).astype(BF16)
    wa = _dot(lora_in, w2a2_ref[...])
    w_log = -_softplus(-(w0 + wa[:, :RWKV_W])) - 0.5
    logw = -jnp.exp(w_log)
    a = _sigmoid(a0 + wa[:, RWKV_W:])
    g = _dot(_sigmoid(zg).astype(BF16), g2_ref[...])

    eseg = eseg_ref[...]

    def segsum(x, split=True):
        EW = eseg.shape[0]
        hi = x.astype(BF16)
        lo = (x - hi.astype(F32)).astype(BF16) if split else None
        parts = []
        for q in range(RWKV_W // EW):
            qs = slice(q * EW, (q + 1) * EW)
            sq = _dot(hi[:, qs], eseg)
            parts.append(sq + _dot(lo[:, qs], eseg) if split else sq)
        return jnp.concatenate(parts, axis=1)

    yield
    kk = k * k_k
    kk = kk * lax.rsqrt(jnp.maximum(segsum(kk * kk), 1e-24))
    k2 = k * (1.0 + (a - 1.0) * k_a)
    yield

    lg = _dot_sel_lhs(csum_ref[...], logw)
    lg_end = jnp.concatenate([jnp.broadcast_to(lg[(j + 1) * C - 1:(j + 1) * C, :], (C, RWKV_W))
                              for j in range(RB // C)], axis=0)
    eg = jnp.exp(lg)
    egi = jnp.exp(-lg)
    g_end = jnp.exp(lg_end)
    d_end = jnp.exp(lg_end - lg)
    yield
    kka = kk * a
    ab = -kk * jnp.exp(lg - logw)
    rb = r * eg
    bt = kka * egi
    kt = k2 * egi
    bg = kka * d_end
    kg = k2 * d_end
    yield

    W2 = 2 * N
    left = lax.broadcasted_iota(jnp.int32, (1, W2), 1) < N

    def bd(x):
        return jnp.concatenate([jnp.where(left, x, 0.0), jnp.where(left, 0.0, x)], axis=0)

    ri = lax.broadcasted_iota(jnp.int32, (2 * W2, 2 * W2), 0)
    ci = lax.broadcasted_iota(jnp.int32, (2 * W2, 2 * W2), 1)
    mask = (ci & (N - 1)) < (ri & (N - 1)) + jnp.where(ri >= W2, 1, 0)
    e_r = lax.broadcasted_iota(jnp.int32, (W2, W2), 0)
    e_c = lax.broadcasted_iota(jnp.int32, (W2, W2), 1)
    eye = jnp.where(e_r == e_c, 1.0, 0.0)
    zeros_b = jnp.zeros((W2, W2), BF16)

    units = [(j, p) for j in range(RB // C) for p in range(RWKV_HEADS // 2)]
    U = len(units)

    def piece(x, j, p):
        return x[j * C:(j + 1) * C, p * W2:(p + 1) * W2]

    def each(fn, n):
        out = []
        for i in range(n):
            out.append(fn(i))
            yield
        return out

    ab_u = yield from each(lambda i: bd(piece(ab, *units[i])).astype(BF16), U)
    AR = yield from each(lambda i: jnp.concatenate([ab_u[i], bd(piece(rb, *units[i])).astype(BF16)], axis=0), U)
    BK = yield from each(lambda i: jnp.concatenate([bd(piece(bt, *units[i])), bd(piece(kt, *units[i]))],
                                                   axis=0).astype(BF16), U)
    v_u = yield from each(lambda i: bd(piece(v, *units[i])).astype(BF16), U)
    G = yield from each(lambda i: jnp.where(mask, _dot_nt(AR[i], BK[i]), 0.0), U)
    def diag2(ya, yb):
        z = jnp.zeros_like(ya)
        return jnp.concatenate([jnp.concatenate([ya, z], axis=1), jnp.concatenate([z, yb], axis=1)], axis=0)

    def diag2_of(xx):
        return diag2(xx[:, :W2], xx[:, W2:])

    SU = U // 2
    AA = [jnp.concatenate([G[2 * s][:W2, :W2], G[2 * s + 1][:W2, :W2]], axis=1) for s in range(SU)]
    SS = [jnp.concatenate([eye, eye], axis=1) + AA[s] for s in range(SU)]
    AAb = [AA[s].astype(BF16) for s in range(SU)]
    QQ = yield from each(lambda s: _dot(AAb[s], diag2_of(AAb[s])), SU)
    for _ in range(int(math.log2(C)) - 2):
        QQb = [QQ[s].astype(BF16) for s in range(SU)]
        QS = yield from each(lambda s: _dot(jnp.concatenate([QQb[s], SS[s].astype(BF16)], axis=0),
                                            diag2_of(QQb[s])), SU)
        QQ = [QS[s][:W2] for s in range(SU)]
        SS = [SS[s] + QS[s][W2:] for s in range(SU)]
    SS = yield from each(lambda s: (SS[s] + _dot(SS[s].astype(BF16), diag2_of(QQ[s].astype(BF16)))).astype(BF16),
                         SU)
    S = [SS[i // 2][:, (i % 2) * W2:(i % 2 + 1) * W2] for i in range(U)]
    akv2 = yield from each(
        lambda s: _dot(jnp.concatenate([G[2 * s][:W2, W2:], G[2 * s + 1][:W2, W2:]], axis=1).astype(BF16),
                       diag2(v_u[2 * s], v_u[2 * s + 1])), SU)
    akv = [akv2[i // 2][:, (i % 2) * W2:(i % 2 + 1) * W2] for i in range(U)]
    WU = yield from each(
        lambda i: _dot(S[i], jnp.concatenate([ab_u[i], akv[i].astype(BF16)], axis=1)).astype(BF16), U)
    Z = [jnp.concatenate([WU[i], jnp.concatenate([zeros_b, v_u[i]], axis=1)], axis=0) for i in range(U)]
    QY = yield from each(lambda i: _dot(G[i][W2:, :].astype(BF16), Z[i]), U)
    BKgT = yield from each(lambda i: jnp.concatenate([bd(piece(bg, *units[i])).T, bd(piece(kg, *units[i])).T],
                                                     axis=1).astype(BF16), U)
    MN = yield from each(lambda i: _dot(BKgT[i], Z[i]), U)
    QM = [jnp.concatenate([AR[i][W2:].astype(F32) + QY[i][:, :W2], MN[i][:, :W2]], axis=0).astype(BF16)
          for i in range(U)]
    gcol = [jnp.sum(eye * piece(g_end, j, p)[0:1, :], axis=1, keepdims=True) for j, p in units]

    for s in range(SU):
        ia, ib = 2 * s, 2 * s + 1
        Ha, Hb = h_scr[units[ia][1]], h_scr[units[ib][1]]
        YH2 = _dot(jnp.concatenate([QM[ia], QM[ib]], axis=1), diag2(Ha.astype(BF16), Hb.astype(BF16)))
        for i, H, YH in ((ia, Ha, YH2[:, :W2]), (ib, Hb, YH2[:, W2:])):
            j, p = units[i]
            ybd = YH[:W2] + QY[i][:, W2:]
            y_scr[j * C:(j + 1) * C, p * W2:(p + 1) * W2] = ybd[:C] + ybd[C:]
            h_scr[p] = H * gcol[i] + YH[W2:] + MN[i][:, W2:]
        yield

    y = y_scr[...]
    inv_n = 1.0 / N
    mean = segsum(y) * inv_n
    d = y - mean
    yield
    var = segsum(d * d, split=False) * inv_n
    yn = d * lax.rsqrt(var + RWKV_GN_EPS) * ln_g + ln_b
    yield
    bonus = segsum(r * k2 * r_k, split=False) * v
    o_ref[...] = ((yn + bonus) * g).astype(o_ref.dtype)


def _ssd_body(z_ref, xs_ref, bc_ref, dt_ref, cwx_ref, cwbc_ref, vecx_ref, vecbc_ref, vecd_ref,
              eh_ref, ltri_ref, o_ref, st_scr, tailx_scr, tailbc_scr, y_scr):
    RB = xs_ref.shape[0]
    Q = SSD_CHUNK
    P, N = SSD_P, SSD_N
    GW = SSD_W // SSD_GROUPS

    def conv_silu(x_ref, tail_scr, cw_ref, b_ref):
        parts = []
        for q in range(x_ref.shape[1] // 256):
            cs = slice(q * 256, (q + 1) * 256)
            x_raw = x_ref[:, cs]
            ext = jnp.concatenate([tail_scr[:, cs], x_raw], axis=0)
            acc = b_ref[0:1, cs] + cw_ref[SSD_CONV - 1:SSD_CONV, cs] * x_raw
            for s in range(1, SSD_CONV):
                acc = acc + cw_ref[SSD_CONV - 1 - s:SSD_CONV - s, cs] * pltpu.roll(ext, s, 0)[8:, :]
            tail_scr[:, cs] = x_raw[RB - 8:, :]
            parts.append(_silu(acc))
            yield
        return jnp.concatenate(parts, axis=1)

    xs = yield from conv_silu(xs_ref, tailx_scr, cwx_ref, vecx_ref)
    bc = yield from conv_silu(bc_ref, tailbc_scr, cwbc_ref, vecbc_ref)

    eh = eh_ref[...]
    dt = _softplus(dt_ref[...] + vecd_ref[0:1, :])
    a = dt * vecd_ref[1:2, :]
    acum = _dot_sel_lhs(ltri_ref[...], a)
    dt_e = _dot_sel_rhs(dt, eh)
    acum_e = _dot_sel_rhs(acum, eh)
    acum_t = acum.T
    yield
    xdt = xs * dt_e
    xdt_b = xdt.astype(BF16)
    ea = jnp.exp(acum_e)
    bc_b = bc.astype(BF16)
    yield

    li = lax.broadcasted_iota(jnp.int32, (Q, Q), 0)
    si = lax.broadcasted_iota(jnp.int32, (Q, Q), 1)
    causal = si <= li

    for ch in range(RB // Q):
        rs = slice(ch * Q, (ch + 1) * Q)
        a_last = acum_e[(ch + 1) * Q - 1:(ch + 1) * Q, :]
        xdte = (xdt[rs] * jnp.exp(a_last - acum_e[rs])).astype(BF16)
        chunk_decay = jnp.exp(a_last)
        for gi in range(SSD_GROUPS):
            gs = slice(gi * GW, (gi + 1) * GW)
            Bg = bc_b[rs, gi * N:(gi + 1) * N]
            Cg = bc_b[rs, SSD_GROUPS * N + gi * N:SSD_GROUPS * N + (gi + 1) * N]
            cb = _dot_nt(Cg, Bg)
            prev = st_scr[gi]
            y_off = _dot(Cg, prev.astype(BF16)) * ea[rs, gs]
            st_scr[gi] = prev * chunk_decay[:, gs] + _dot(bc[rs, gi * N:(gi + 1) * N].T.astype(BF16), xdte[:, gs])
            yield
            ys = []
            for j in range(SSD_HEADS // SSD_GROUPS):
                hh = gi * (SSD_HEADS // SSD_GROUPS) + j
                hs = slice(hh * P, (hh + 1) * P)
                seg = acum_e[rs, hs] - acum_t[hh:hh + 1, rs]
                wm = jnp.where(causal, cb * jnp.exp(seg), 0.0).astype(BF16)
                ys.append(_dot(wm, xdt_b[rs, hs]))
                if j % 2 == 1:
                    yield
            y_scr[rs, gs] = jnp.concatenate(ys, axis=1) + y_off

    for gi in range(SSD_GROUPS):
        gs = slice(gi * GW, (gi + 1) * GW)
        yg = (y_scr[:, gs] + vecx_ref[2:3, gs] * xs[:, gs]) * _silu(z_ref[:, gs])
        yg = yg * lax.rsqrt(jnp.mean(yg * yg, axis=-1, keepdims=True) + EPS)
        o_ref[:, gs] = (yg * vecx_ref[1:2, gs]).astype(o_ref.dtype)
        yield


N_RWKV_IN, N_SSD_IN = 11, 11
N_RWKV_SCR, N_SSD_SCR = 4, 4
RWKV_TURN, SSD_TURN = 4, 1


def _rwkv_ssd_kernel(*refs):
    n_in = N_RWKV_IN + N_SSD_IN
    rwkv_in, ssd_in = refs[:N_RWKV_IN], refs[N_RWKV_IN:n_in]
    o_rwkv, o_ssd = refs[n_in], refs[n_in + 1]
    scr = refs[n_in + 2:]
    rwkv_scr, ssd_scr = scr[:N_RWKV_SCR], scr[N_RWKV_SCR:]

    @pl.when(pl.program_id(1) == 0)
    def _():
        for s in rwkv_scr[:3] + ssd_scr[:3]:
            s[...] = jnp.zeros_like(s)

    _interleave((_rwkv_body(*rwkv_in, o_rwkv, *rwkv_scr), RWKV_TURN), (_ssd_body(*ssd_in, o_ssd, *ssd_scr), SSD_TURN))


def _interleave(*weighted):
    live = list(weighted)
    while live:
        for item in list(live):
            gen, turn = item
            for _ in range(turn):
                try:
                    next(gen)
                except StopIteration:
                    live.remove(item)
                    break


def _rwkv_ssd_group(p0, rwkv_cols, ssd_cols, batch, seq_len, vec, vec2, w2a2, g2, cwx, cwbc, vecx, vecbc, vecd):
    C, Q = RWKV_CHUNK, SSD_CHUNK
    assert Q == SSD_P
    RB = C * RWKV_BLOCK_CHUNKS
    assert RB == Q * SSD_BLOCK_CHUNKS
    nc = seq_len // RB
    T = batch * seq_len
    cr, ck, cv, czwa, czg = rwkv_cols
    cz, cx, cbc, cdt = ssd_cols
    eseg = jnp.asarray(np.arange(256)[:, None] // RWKV_N == np.arange(256)[None, :] // RWKV_N, BF16)
    same_chunk = np.arange(RB)[:, None] // C == np.arange(RB)[None, :] // C
    ltri = jnp.asarray(np.tril(np.ones((RB, RB))) * same_chunk, BF16)
    csum = ltri
    eh = np.zeros((128, SSD_W), np.float32)
    for hh in range(SSD_HEADS):
        eh[hh, hh * SSD_P:(hh + 1) * SSD_P] = 1.0
    eh = jnp.asarray(eh, BF16)

    def col(width, idx):
        return pl.BlockSpec((RB, width), lambda b, c: (b * nc + c, idx))

    def full(arr):
        return pl.BlockSpec(arr.shape, lambda b, c: (0,) * arr.ndim)

    rwkv_in = [col(RWKV_W, cr), col(RWKV_W, ck), col(RWKV_W, cv), col(128, czwa), col(128, czg),
               full(vec), full(vec2), full(w2a2), full(g2), full(eseg), full(csum)]
    ssd_in = [col(SSD_W, cz), col(SSD_W, cx), col(512, cbc), col(128, cdt),
              full(cwx), full(cwbc), full(vecx), full(vecbc), full(vecd), full(eh), full(ltri)]
    assert len(rwkv_in) == N_RWKV_IN and len(ssd_in) == N_SSD_IN
    rwkv_scr = [pltpu.VMEM((RWKV_HEADS // 2, 2 * RWKV_N, 2 * RWKV_N), F32),
                pltpu.VMEM((8, RWKV_W), F32), pltpu.VMEM((8, 128), F32), pltpu.VMEM((RB, RWKV_W), F32)]
    ssd_scr = [pltpu.VMEM((SSD_GROUPS, SSD_N, SSD_W // SSD_GROUPS), F32),
               pltpu.VMEM((8, SSD_W), F32), pltpu.VMEM((8, 512), F32), pltpu.VMEM((RB, SSD_W), F32)]
    return pl.pallas_call(
        _rwkv_ssd_kernel,
        grid=(batch, nc),
        in_specs=rwkv_in + ssd_in,
        out_specs=[pl.BlockSpec((RB, RWKV_W), lambda b, c: (b * nc + c, 0)),
                   pl.BlockSpec((RB, SSD_W), lambda b, c: (b * nc + c, 0))],
        out_shape=[jax.ShapeDtypeStruct((T, RWKV_W), BF16), jax.ShapeDtypeStruct((T, SSD_W), BF16)],
        scratch_shapes=rwkv_scr + ssd_scr,
        compiler_params=_cparams(("arbitrary", "arbitrary")),
    )(p0, p0, p0, p0, p0, vec, vec2, w2a2, g2, eseg, csum,
      p0, p0, p0, p0, cwx, cwbc, vecx, vecbc, vecd, eh, ltri)


def _s5_kernel(u_ref, toep_ref, bz_ref, cz_ref, lam_ref, d_ref, wglu_ref, bglu_ref, o_ref,
               st_scr, zre_scr, zim_scr, sre_scr, sim_scr, y_scr, fold_scr):
    i = pl.program_id(1)
    Mb = u_ref.shape[0]
    LC, NT = S5_LC, S5_TILES
    HW = S5_W // NT * (S5_P // S5_H)

    @pl.when(i == 0)
    def _():
        st_scr[...] = jnp.zeros_like(st_scr)

    u = u_ref[...]
    ucat = []
    for n in range(NT):
        un = jnp.concatenate([u[:, l * S5_W + n * 128:l * S5_W + (n + 1) * 128] for l in range(LC)],
                             axis=1).astype(BF16)
        ucat.append(un)
        z = _dot(un, bz_ref[n])
        zre_scr[:, n * HW:(n + 1) * HW] = z[:, :HW]
        zim_scr[:, n * HW:(n + 1) * HW] = z[:, HW:]

    lr = lam_ref[0:1, :]
    lim = lam_ref[1:2, :]

    def step(m, carry):
        sre, sim = carry
        sre_scr[pl.ds(m, 1), :] = sre
        sim_scr[pl.ds(m, 1), :] = sim
        zr = zre_scr[pl.ds(m, 1), :]
        zi = zim_scr[pl.ds(m, 1), :]
        return lr * sre - lim * sim + zr, lr * sim + lim * sre + zi

    sre, sim = lax.fori_loop(0, Mb, step, (st_scr[0:1, :], st_scr[1:2, :]))
    st_scr[0:1, :] = sre
    st_scr[1:2, :] = sim

    for n in range(NT):
        s_re = sre_scr[:, n * HW:(n + 1) * HW].astype(BF16)
        s_im = sim_scr[:, n * HW:(n + 1) * HW].astype(BF16)
        yn = _dot(ucat[n], toep_ref[n]) + _dot(s_re, cz_ref[n, :HW, :]) + _dot(s_im, cz_ref[n, HW:, :])
        for l in range(LC):
            y_scr[:, l * S5_W + n * 128:l * S5_W + (n + 1) * 128] = yn[:, l * 128:(l + 1) * 128]

    y = y_scr[...] + d_ref[...] * u
    y = 0.5 * y * (1.0 + jnp.tanh(math.sqrt(2.0 / math.pi) * (y + 0.044715 * (y * y * y))))
    wglu = wglu_ref[...]
    for l in range(LC):
        yl = y[:, l * S5_W:(l + 1) * S5_W]
        gate = _sigmoid(_dot(yl.astype(BF16), wglu) + bglu_ref[...])
        out_l = yl * gate
        for n in range(NT):
            fold_scr[n, pl.ds(l, Mb, stride=LC), :] = out_l[:, n * 128:(n + 1) * 128]
    for n in range(NT):
        o_ref[:, n * 128:(n + 1) * 128] = fold_scr[n]


def _s5_tables(lam_re, lam_im, log_dt, b_re, b_im, c_re, c_im):
    LC, NT, G, H, Pn = S5_LC, S5_TILES, S5_GROUPS, S5_H, S5_P
    GL = G // NT
    hp = lax.Precision.HIGHEST
    dt = jnp.exp(log_dt)[:, None]
    ar, ai = lam_re * dt, lam_im * dt
    jj = jnp.arange(LC + 1, dtype=F32)[:, None, None]
    mag = jnp.exp(jj * ar[None])
    pw_re, pw_im = mag * jnp.cos(jj * ai[None]), mag * jnp.sin(jj * ai[None])
    nr, ni = pw_re[1] - 1.0, pw_im[1]
    den = lam_re * lam_re + lam_im * lam_im
    qr, qi = (nr * lam_re + ni * lam_im) / den, (ni * lam_re - nr * lam_im) / den
    bb_re = qr[..., None] * b_re - qi[..., None] * b_im
    bb_im = qr[..., None] * b_im + qi[..., None] * b_re
    cl_re = c_re[None] * pw_re[:, :, None, :] - c_im[None] * pw_im[:, :, None, :]
    cl_im = c_re[None] * pw_im[:, :, None, :] + c_im[None] * pw_re[:, :, None, :]
    kd = (jnp.einsum("jghp,gpk->jghk", cl_re[:LC], bb_re, precision=hp)
          - jnp.einsum("jghp,gpk->jghk", cl_im[:LC], bb_im, precision=hp))
    def spread(x, rows_per_group, cols_per_group):
        sel = np.tile(np.eye(cols_per_group, dtype=np.float32), (1, GL))
        keep = (np.arange(GL * rows_per_group)[:, None] // rows_per_group
                == np.arange(GL * cols_per_group)[None, :] // cols_per_group)
        return jnp.where(keep, jnp.einsum("...rc,cd->...rd", x, jnp.asarray(sel), precision=hp), 0.0)

    blk = spread(jnp.swapaxes(kd, -1, -2).reshape(LC, NT, GL * H, H), H, H).astype(BF16)
    zp = jnp.concatenate([jnp.zeros((LC - 1,) + blk.shape[1:], BF16), blk], axis=0)
    toep = jnp.concatenate(
        [jnp.concatenate([zp[lo - li + LC - 1] for lo in range(LC)], axis=-1) for li in range(LC)], axis=1)
    def cz_part(cl):
        x = spread(jnp.swapaxes(cl[1:], -1, -2).reshape(LC, NT, GL * Pn, H), Pn, H).astype(BF16)
        return jnp.concatenate([x[l] for l in range(LC)], axis=-1)
    cz = jnp.concatenate([cz_part(cl_re), cz_part(-cl_im)], axis=1)
    jr = (LC - 1.0) - jnp.arange(LC, dtype=F32)[:, None, None]
    mag_r = jnp.exp(jr * ar[None])
    rev_re, rev_im = mag_r * jnp.cos(jr * ai[None]), mag_r * jnp.sin(jr * ai[None])
    bl_re = rev_re[..., None] * bb_re[None] - rev_im[..., None] * bb_im[None]
    bl_im = rev_re[..., None] * bb_im[None] + rev_im[..., None] * bb_re[None]

    def bz_part(bl):
        x = spread(jnp.swapaxes(bl, -1, -2).reshape(LC, NT, GL * H, Pn), H, Pn).astype(BF16)
        return jnp.swapaxes(x, 0, 1).reshape(NT, LC * GL * H, GL * Pn)
    bz = jnp.concatenate([bz_part(bl_re), bz_part(bl_im)], axis=2)
    lam = _rows([pw_re[LC].reshape(-1), pw_im[LC].reshape(-1)], 8)
    return toep, bz, cz, lam


def _s5_group(p1, col_u, batch, seq_len, tables, d_skip, w_glu, b_glu):
    LC = S5_LC
    T = batch * seq_len
    toep, bz, cz, lam = tables
    rows_per_seq = seq_len // LC
    Mb = min(S5_ROWS, rows_per_seq)
    nb = rows_per_seq // Mb
    NPc = p1.shape[1]
    W = LC * S5_W
    d_t = jnp.tile(d_skip.reshape(1, S5_W), (1, LC))
    SW = S5_TILES * (S5_GROUPS // S5_TILES) * S5_P

    def full(arr):
        return pl.BlockSpec(arr.shape, lambda b, i: (0,) * arr.ndim)

    def kern(u_ref, *rest):
        ucat_scr, fold_scr = rest[-2], rest[-1]
        for n in range(S5_TILES):
            fold_scr[n] = u_ref[:, n * 128:(n + 1) * 128]
        for n in range(S5_TILES):
            for l in range(LC):
                ucat_scr[:, l * S5_W + n * 128:l * S5_W + (n + 1) * 128] = fold_scr[n, pl.ds(l, Mb, stride=LC), :]
        _s5_kernel(ucat_scr, *rest[:-2], fold_scr)

    return pl.pallas_call(
        kern,
        grid=(batch, nb),
        in_specs=[pl.BlockSpec((Mb * LC, S5_W), lambda b, i: (b * nb + i, col_u)),
                  _resident(toep.shape), _resident(bz.shape), _resident(cz.shape), full(lam), full(d_t),
                  full(w_glu),
                  pl.BlockSpec((1, S5_W), lambda b, i: (0, 0))],
        out_specs=pl.BlockSpec((Mb * LC, S5_W), lambda b, i: (b * nb + i, 0)),
        out_shape=jax.ShapeDtypeStruct((T, S5_W), F32),
        scratch_shapes=[pltpu.VMEM((8, SW), F32),
                        pltpu.VMEM((Mb, SW), F32), pltpu.VMEM((Mb, SW), F32),
                        pltpu.VMEM((Mb, SW), F32), pltpu.VMEM((Mb, SW), F32),
                        pltpu.VMEM((Mb, W), F32),
                        pltpu.VMEM((Mb, W), F32),
                        pltpu.VMEM((S5_TILES, Mb * LC, 128), F32)],
        compiler_params=_cparams(("arbitrary", "arbitrary")),
    )(p1, toep, bz, cz, lam, d_t, w_glu, b_glu.reshape(1, S5_W))


def _ret_kernel(q_ref, k_ref, v_ref, g_ref, cos_ref, sin_ref, intra_ref, kdec_ref, qdec_ref, o_ref,
                st_scr, *, chunk_decay):
    c = pl.program_id(1)
    DK, DV = RET_DK, RET_DV

    @pl.when(c == 0)
    def _():
        st_scr[...] = jnp.zeros_like(st_scr)

    Q = intra_ref.shape[1]
    scale = DK ** -0.5
    for ch in range(q_ref.shape[0] // Q):
        rs = slice(ch * Q, (ch + 1) * Q)
        cosf = cos_ref[rs, :]
        sinf = sin_ref[rs, :]
        for h in range(RET_HEADS):
            ks = slice(h * DK, (h + 1) * DK)
            vs = slice(h * DV, (h + 1) * DV)
            qh = q_ref[rs, ks]
            kh = k_ref[rs, ks]
            qh = qh * cosf + pltpu.roll(qh, DK // 2, 1) * sinf
            kh = (kh * cosf + pltpu.roll(kh, DK // 2, 1) * sinf) * scale
            vh = v_ref[rs, vs].astype(BF16)
            scores = (_dot_nt(qh.astype(BF16), kh.astype(BF16)) * intra_ref[h]).astype(BF16)
            prev = st_scr[h]
            y = _dot(scores, vh) + _dot((qh * qdec_ref[:, ks]).astype(BF16), prev.astype(BF16))
            kd = (kh * kdec_ref[:, ks]).T.astype(BF16)
            st_scr[h] = prev * chunk_decay[h] + _dot(kd, vh)
            y = y * lax.rsqrt(jnp.mean(y * y, axis=-1, keepdims=True) + EPS)
            o_ref[rs, vs] = (_silu(g_ref[rs, vs]) * y).astype(o_ref.dtype)


def _ret_group(p1, cols, batch, seq_len):
    Q = min(RET_CHUNK, seq_len)
    RB = min(Q * RET_BLOCK_CHUNKS, seq_len)
    nc = seq_len // RB
    T = batch * seq_len
    cq, ck, cv, cg = cols
    H, DK, DV = RET_HEADS, RET_DK, RET_DV
    half = DK // 2
    theta = 1.0 / (10000.0 ** jnp.linspace(0.0, 1.0, half, dtype=F32))
    ang = jnp.arange(seq_len).astype(F32)[:, None] * theta[None, :]
    cos, sin = jnp.cos(ang), jnp.sin(ang)
    cosf = jnp.concatenate([cos, cos], axis=1)
    sinf = jnp.concatenate([-sin, sin], axis=1)
    log_gamma = jnp.log(1.0 - 2.0 ** (-5.0 - jnp.arange(H, dtype=F32)))
    t = jnp.arange(Q, dtype=F32)
    diff = t[:, None] - t[None, :]
    intra = jnp.where((diff >= 0)[None], jnp.exp(jnp.maximum(diff, 0.0)[None] * log_gamma[:, None, None]), 0.0)
    kdec = jnp.repeat(jnp.exp((Q - 1.0 - t)[:, None] * log_gamma[None, :]), DK, axis=1)
    qdec = jnp.repeat(jnp.exp((t + 1.0)[:, None] * log_gamma[None, :]), DK, axis=1)
    chunk_decay = tuple(float((1.0 - 2.0 ** (-5.0 - h)) ** Q) for h in range(H))

    def col(width, idx):
        return pl.BlockSpec((RB, width), lambda b, c: (b * nc + c, idx))

    def full(arr):
        return pl.BlockSpec(arr.shape, lambda b, c: (0,) * arr.ndim)

    return pl.pallas_call(
        functools.partial(_ret_kernel, chunk_decay=chunk_decay),
        grid=(batch, nc),
        in_specs=[col(H * DK, cq), col(H * DK, ck), col(H * DV, cv), col(H * DV, cg),
                  pl.BlockSpec((RB, DK), lambda b, c: (c, 0)), pl.BlockSpec((RB, DK), lambda b, c: (c, 0)),
                  full(intra), full(kdec), full(qdec)],
        out_specs=pl.BlockSpec((RB, H * DV), lambda b, c: (b * nc + c, 0)),
        out_shape=jax.ShapeDtypeStruct((T, H * DV), BF16),
        scratch_shapes=[pltpu.VMEM((H, DK, DV), F32)],
        compiler_params=_cparams(("arbitrary", "arbitrary")),
    )(p1, p1, p1, p1, cosf, sinf, intra, kdec, qdec)


def _pad_cols(w, n):
    return jnp.pad(w, ((0, 0), (0, n - w.shape[1])))


def _rows(vectors, n_rows):
    width = vectors[0].shape[0]
    pad = [jnp.zeros((n_rows - len(vectors), width), F32)] if n_rows > len(vectors) else []
    return jnp.concatenate([v.astype(F32).reshape(1, width) for v in vectors] + pad, axis=0)


def kernel(x, l0_norm_mix, l0_w_in, l0_rwkv_mu, l0_rwkv_w0, l0_rwkv_w2, l0_rwkv_a0, l0_rwkv_a2, l0_rwkv_g2, l0_rwkv_k_k, l0_rwkv_k_a, l0_rwkv_r_k, l0_rwkv_ln_g, l0_rwkv_ln_b, l0_ssd_conv_w, l0_ssd_conv_b, l0_ssd_dt_bias, l0_ssd_a_log, l0_ssd_d, l0_ssd_norm_g, l0_w_out, l0_norm_ffn, l0_ffn_up, l0_ffn_conv_w, l0_ffn_conv_b, l0_ffn_down, l1_norm_mix, l1_w_in, l1_s5_lam_re, l1_s5_lam_im, l1_s5_log_dt, l1_s5_b_re, l1_s5_b_im, l1_s5_c_re, l1_s5_c_im, l1_s5_d, l1_s5_w_glu, l1_s5_b_glu, l1_w_out, l1_norm_ffn, l1_ffn_up, l1_ffn_conv_w, l1_ffn_conv_b, l1_ffn_down, final_norm):
    B, L, D = x.shape
    T = B * L
    x2 = x.reshape(T, D).astype(F32)

    RP = 3 * RWKV_W + 256
    o_z = RP
    o_xbc = o_z + SSD_W
    o_dt = o_xbc + SSD_W + 2 * SSD_GROUPS * SSD_N
    w = l0_w_in
    w0p = jnp.concatenate([w[:, o_z:o_z + SSD_W], w[:, o_xbc:o_dt], w[:, :RP], w[:, o_dt:]], axis=1)
    NP0 = 4608
    w0p = _pad_cols(w0p, NP0).astype(BF16)
    p0 = _norm_proj(x2, l0_norm_mix, w0p)

    mu = l0_rwkv_mu
    vec = _rows([mu[0:512], mu[512:1024], mu[1024:1536], l0_rwkv_w0, l0_rwkv_a0, l0_rwkv_k_k, l0_rwkv_k_a,
                 l0_rwkv_r_k.reshape(-1), l0_rwkv_ln_g, l0_rwkv_ln_b], 16)
    vec2 = _rows([mu[1536:1664], mu[1664:1792]], 8)
    zl = jnp.zeros((64, RWKV_W), F32)
    w2a2 = jnp.concatenate([jnp.concatenate([l0_rwkv_w2.astype(F32), zl], axis=1),
                            jnp.concatenate([zl, l0_rwkv_a2.astype(F32)], axis=1)], axis=0).astype(BF16)

    cw = l0_ssd_conv_w
    cb = l0_ssd_conv_b
    cwx = _rows([cw[k, :SSD_W] for k in range(SSD_CONV)], 8)
    cwbc = _rows([cw[k, SSD_W:] for k in range(SSD_CONV)], 8)
    vecx = _rows([cb[:SSD_W], l0_ssd_norm_g, jnp.repeat(l0_ssd_d, SSD_P)], 8)
    vecbc = _rows([cb[SSD_W:]], 8)
    hpad = jnp.zeros((128 - SSD_HEADS,), F32)
    vecd = _rows([jnp.concatenate([l0_ssd_dt_bias.astype(F32), hpad]),
                  jnp.concatenate([-jnp.exp(l0_ssd_a_log.astype(F32)), hpad])], 8)
    y_a, y_b = _rwkv_ssd_group(p0, (5, 6, 7, 32, 33), (0, 1, 4, 34), B, L, vec, vec2, w2a2,
                               l0_rwkv_g2.astype(BF16), cwx, cwbc, vecx, vecbc, vecd)

    wo = l0_w_out.astype(BF16)
    x2 = _mix_ffn(x2, y_a, y_b, wo[:RWKV_W], wo[RWKV_W:], L, l0_norm_ffn, l0_ffn_up.astype(BF16),
                  l0_ffn_conv_w.astype(F32), l0_ffn_conv_b.astype(F32), l0_ffn_down.astype(BF16),
                  final_norm, False)

    w = l1_w_in
    w1p = jnp.concatenate([w[:, 1536:2560], w[:, 2560:3584], w[:, :1536]], axis=1).astype(BF16)
    p1 = _norm_proj(x2, l1_norm_mix, w1p)
    tables = _s5_tables(*(t.astype(F32) for t in (l1_s5_lam_re, l1_s5_lam_im, l1_s5_log_dt, l1_s5_b_re,
                                                   l1_s5_b_im, l1_s5_c_re, l1_s5_c_im)))
    y_c = _s5_group(p1, 4, B, L, tables, l1_s5_d.astype(F32), l1_s5_w_glu.astype(BF16),
                    l1_s5_b_glu.astype(F32))
    y_d = _ret_group(p1, (5, 6, 0, 1), B, L)
    wo = l1_w_out.astype(BF16)
    x2 = _mix_ffn(x2, y_c, y_d, wo[:S5_W], wo[S5_W:], L, l1_norm_ffn, l1_ffn_up.astype(BF16),
                  l1_ffn_conv_w.astype(F32), l1_ffn_conv_b.astype(F32), l1_ffn_down.astype(BF16),
                  final_norm, True)
    return x2.reshape(B, L, D).astype(x.dtype)
```

```python
import functools
import math

import numpy as np
import jax
import jax.numpy as jnp
from jax import lax
from jax.experimental import pallas as pl
from jax.experimental.pallas import tpu as pltpu

F32 = jnp.float32
BF16 = jnp.bfloat16

EPS = 1e-6
D_MODEL = 1024

RWKV_HEADS = 8
RWKV_N = 64
RWKV_W = 512
RWKV_GN_EPS = 64e-5
RWKV_CHUNK = 64
RWKV_BLOCK_CHUNKS = 4

SSD_HEADS = 16
SSD_P = 64
SSD_GROUPS = 2
SSD_N = 128
SSD_W = 1024
SSD_CONV = 4
SSD_CHUNK = 64
SSD_BLOCK_CHUNKS = 4

S5_W = 512
S5_GROUPS = 32
S5_H = 16
S5_P = 64
S5_LC = 8
S5_TILES = 4
S5_ROWS = 128

RET_HEADS = 4
RET_DK = 128
RET_DV = 256
RET_CHUNK = 256
RET_BLOCK_CHUNKS = 2

FFN_HIDDEN = 2816
FFN_TN = 256
FFN_HALO = 16

VMEM_LIMIT = 48 * 1024 * 1024


def _dot(a, b):
    return jnp.dot(a, b, preferred_element_type=F32)


def _dot_nt(a, b):
    return lax.dot_general(a, b, (((1,), (1,)), ((), ())), preferred_element_type=F32)


def _dot_tn(a, b):
    return lax.dot_general(a, b, (((0,), (0,)), ((), ())), preferred_element_type=F32)


def _split3(x):
    hi = x.astype(BF16)
    r1 = x - hi.astype(F32)
    mid = r1.astype(BF16)
    lo = (r1 - mid.astype(F32)).astype(BF16)
    return hi, mid, lo


def _dot_sel_rhs(x, sel):
    hi, mid, lo = _split3(x)
    return _dot(hi, sel) + _dot(mid, sel) + _dot(lo, sel)


def _dot_sel_lhs(sel, x):
    hi, mid, lo = _split3(x)
    return _dot(sel, hi) + _dot(sel, mid) + _dot(sel, lo)


def _sigmoid(x):
    return 1.0 / (1.0 + jnp.exp(-x))


def _silu(x):
    return x * _sigmoid(x)


def _softplus(x):
    return jnp.maximum(x, 0.0) + jnp.log(1.0 + jnp.exp(-jnp.abs(x)))


def _cparams(sem):
    return pltpu.CompilerParams(dimension_semantics=sem, vmem_limit_bytes=VMEM_LIMIT)


def _proj_kernel(x_ref, g_ref, w_ref, o_ref, *, pieces):
    x = x_ref[...]
    y = x * lax.rsqrt(jnp.mean(x * x, axis=-1, keepdims=True) + EPS)
    hn = (y * g_ref[...]).astype(BF16)
    for src, dst, width in pieces:
        o_ref[:, dst:dst + width] = _dot(hn, w_ref[:, src:src + width])


def _norm_proj(x2, g, w, pieces, tm=512):
    T, D = x2.shape
    N = sum(p[2] for p in pieces)
    return pl.pallas_call(
        functools.partial(_proj_kernel, pieces=pieces),
        grid=(T // tm,),
        in_specs=[pl.BlockSpec((tm, D), lambda i: (i, 0)), _resident((1, D)), _resident(w.shape)],
        out_specs=pl.BlockSpec((tm, N), lambda i: (i, 0)),
        out_shape=jax.ShapeDtypeStruct((T, N), F32),
        compiler_params=_cparams(("parallel",)),
    )(x2, g.reshape(1, D), w)


def _ffn_kernel(x_ref, xh_ref, ya_ref, yah_ref, yb_ref, ybh_ref, wo_ref, g_ref, wup_ref,
                cw_ref, cb_ref, wd_ref, gf_ref, o_ref, act_scr, *, blocks_per_seq, final_norm):
    i = pl.program_id(0)
    Hd = wd_ref.shape[0]
    tn = FFN_TN
    Ka = ya_ref.shape[1]

    def mixed(x, ya, yb):
        return x + _dot(ya.astype(BF16), wo_ref[0:Ka, :]) + _dot(yb.astype(BF16), wo_ref[Ka:, :])

    def norm(x):
        y = x * lax.rsqrt(jnp.mean(x * x, axis=-1, keepdims=True) + EPS)
        return (y * g_ref[...]).astype(BF16)

    x1 = mixed(x_ref[...], ya_ref[...], yb_ref[...])
    keep = jnp.where(i % blocks_per_seq == 0, 0.0, 1.0)
    x1h = mixed(xh_ref[...], yah_ref[...], ybh_ref[...]) * keep
    hn = jnp.concatenate([norm(x1h), norm(x1)], axis=0)

    def conv(u, cols):
        u1 = pltpu.roll(u, 1, 0)
        u2 = pltpu.roll(u, 2, 0)
        c = cw_ref[2:3, cols] * u + cw_ref[1:2, cols] * u1 + cw_ref[0:1, cols] * u2 + cb_ref[:, cols]
        return c[FFN_HALO:, :]

    for j in range(Hd // tn):
        gs = slice(j * tn, (j + 1) * tn)
        vs = slice(Hd + j * tn, Hd + (j + 1) * tn)
        cg = conv(_dot(hn, wup_ref[:, gs]), gs)
        cv = conv(_dot(hn, wup_ref[:, vs]), vs)
        act_scr[:, gs] = (_silu(cg) * cv).astype(BF16)

    y = x1 + _dot(act_scr[...], wd_ref[...])
    if final_norm:
        y = y * lax.rsqrt(jnp.mean(y * y, axis=-1, keepdims=True) + EPS) * gf_ref[...]
    o_ref[...] = y


def _resident(shape):
    return pl.BlockSpec(shape, lambda *_: (0,) * len(shape), pipeline_mode=pl.Buffered(1))


def _mix_ffn(x2, ya, yb, wo, seq_len, g, w_up, conv_w, conv_b, w_down, g_final, final_norm, tm=512):
    T, D = x2.shape
    Hd = w_down.shape[0]
    Ka, Kb = ya.shape[1], yb.shape[1]
    tm = min(tm, seq_len)
    hb = tm // FFN_HALO
    kern = functools.partial(_ffn_kernel, blocks_per_seq=seq_len // tm, final_norm=final_norm)

    def rows(width):
        return pl.BlockSpec((tm, width), lambda i: (i, 0))

    def halo(width):
        return pl.BlockSpec((FFN_HALO, width), lambda i: (jnp.maximum(i * hb - 1, 0), 0))

    return pl.pallas_call(
        kern,
        grid=(T // tm,),
        in_specs=[rows(D), halo(D), rows(Ka), halo(Ka), rows(Kb), halo(Kb),
                  _resident((Ka + Kb, D)),
                  _resident((1, D)), _resident((D, 2 * Hd)), _resident((3, 2 * Hd)), _resident((1, 2 * Hd)),
                  _resident((Hd, D)), _resident((1, D))],
        out_specs=pl.BlockSpec((tm, D), lambda i: (i, 0)),
        out_shape=jax.ShapeDtypeStruct((T, D), F32),
        scratch_shapes=[pltpu.VMEM((tm, Hd), BF16)],
        compiler_params=_cparams(("parallel",)),
    )(x2, x2, ya, ya, yb, yb, wo, g.reshape(1, D), w_up, conv_w, conv_b.reshape(1, -1), w_down,
      g_final.reshape(1, D))


def _rwkv_body(r_ref, k_ref, v_ref, zwa_ref, zg_ref, vec_ref, vec2_ref, w2a2_ref, g2_ref,
               eseg_ref, csum_ref, o_ref, h_scr, prev_scr, prev2_scr, y_scr):
    RB = r_ref.shape[0]
    C = RWKV_CHUNK
    N = RWKV_N

    row = lax.broadcasted_iota(jnp.int32, (RB, 1), 0)

    def shift_lerp(x, prev_row, mu):
        xs = jnp.where(row == 0, prev_row, pltpu.roll(x, 1, 0))
        return x + (xs - x) * mu

    r_raw, k_raw, v_raw = r_ref[...], k_ref[...], v_ref[...]
    zwa_raw, zg_raw = zwa_ref[...], zg_ref[...]
    r = shift_lerp(r_raw, prev_scr[0:1, :], vec_ref[0:1, :])
    k = shift_lerp(k_raw, prev_scr[1:2, :], vec_ref[1:2, :])
    v = shift_lerp(v_raw, prev_scr[2:3, :], vec_ref[2:3, :])
    zwa = shift_lerp(zwa_raw, prev2_scr[0:1, :], vec2_ref[0:1, :])
    zg = shift_lerp(zg_raw, prev2_scr[1:2, :], vec2_ref[1:2, :])
    prev_scr[0:1, :] = r_raw[RB - 1:RB, :]
    prev_scr[1:2, :] = k_raw[RB - 1:RB, :]
    prev_scr[2:3, :] = v_raw[RB - 1:RB, :]
    prev2_scr[0:1, :] = zwa_raw[RB - 1:RB, :]
    prev2_scr[1:2, :] = zg_raw[RB - 1:RB, :]

    w0, a0 = vec_ref[3:4, :], vec_ref[4:5, :]
    k_k, k_a, r_k = vec_ref[5:6, :], vec_ref[6:7, :], vec_ref[7:8, :]
    ln_g, ln_b = vec_ref[8:9, :], vec_ref[9:10, :]

    lane = lax.broadcasted_iota(jnp.int32, (1, 128), 1)
    lora_in = jnp.where(lane < 64, jnp.tanh(zwa), zwa).astype(BF16)
    wa = _dot(lora_in, w2a2_ref[...])
    w_log = -_softplus(-(w0 + wa[:, :RWKV_W])) - 0.5
    logw = -jnp.exp(w_log)
    a = _sigmoid(a0 + wa[:, RWKV_W:])
    g = _dot(_sigmoid(zg).astype(BF16), g2_ref[...])

    eseg = eseg_ref[...]

    def segsum(x, split=True):
        EW = eseg.shape[0]
        hi = x.astype(BF16)
        lo = (x - hi.astype(F32)).astype(BF16) if split else None
        parts = []
        for q in range(RWKV_W // EW):
            qs = slice(q * EW, (q + 1) * EW)
            sq = _dot(hi[:, qs], eseg)
            parts.append(sq + _dot(lo[:, qs], eseg) if split else sq)
        return jnp.concatenate(parts, axis=1)

    yield
    kk = k * k_k
    kk = kk * lax.rsqrt(jnp.maximum(segsum(kk * kk), 1e-24))
    k2 = k * (1.0 + (a - 1.0) * k_a)
    yield

    lg = _dot_sel_lhs(csum_ref[...], logw)
    lg_end = jnp.concatenate([jnp.broadcast_to(lg[(j + 1) * C - 1:(j + 1) * C, :], (C, RWKV_W))
                              for j in range(RB // C)], axis=0)
    eg = jnp.exp(lg)
    egi = jnp.exp(-lg)
    g_end = jnp.exp(lg_end)
    d_end = jnp.exp(lg_end - lg)
    yield
    kka = kk * a
    ab = -kk * jnp.exp(lg - logw)
    rb = r * eg
    bt = kka * egi
    kt = k2 * egi
    bg = kka * d_end
    kg = k2 * d_end
    yield

    W2 = 2 * N
    left = lax.broadcasted_iota(jnp.int32, (1, W2), 1) < N

    def bd(x):
        return jnp.concatenate([jnp.where(left, x, 0.0), jnp.where(left, 0.0, x)], axis=0)

    ri = lax.broadcasted_iota(jnp.int32, (2 * W2, 2 * W2), 0)
    ci = lax.broadcasted_iota(jnp.int32, (2 * W2, 2 * W2), 1)
    mask = (ci & (N - 1)) < (ri & (N - 1)) + jnp.where(ri >= W2, 1, 0)
    e_r = lax.broadcasted_iota(jnp.int32, (W2, W2), 0)
    e_c = lax.broadcasted_iota(jnp.int32, (W2, W2), 1)
    eye = jnp.where(e_r == e_c, 1.0, 0.0)
    zeros_b = jnp.zeros((W2, W2), BF16)

    units = [(j, p) for j in range(RB // C) for p in range(RWKV_HEADS // 2)]
    U = len(units)

    def piece(x, j, p):
        return x[j * C:(j + 1) * C, p * W2:(p + 1) * W2]

    def each(fn, n):
        out = []
        for i in range(n):
            out.append(fn(i))
            yield
        return out

    ab_u = yield from each(lambda i: bd(piece(ab, *units[i])).astype(BF16), U)
    AR = yield from each(lambda i: jnp.concatenate([ab_u[i], bd(piece(rb, *units[i])).astype(BF16)], axis=0), U)
    BK = yield from each(lambda i: jnp.concatenate([bd(piece(bt, *units[i])), bd(piece(kt, *units[i]))],
                                                   axis=0).astype(BF16), U)
    v_u = yield from each(lambda i: bd(piece(v, *units[i])).astype(BF16), U)
    G = yield from each(lambda i: jnp.where(mask, _dot_nt(AR[i], BK[i]), 0.0), U)
    def diag2(ya, yb):
        z = jnp.zeros_like(ya)
        return jnp.concatenate([jnp.concatenate([ya, z], axis=1), jnp.concatenate([z, yb], axis=1)], axis=0)

    def diag2_of(xx):
        return diag2(xx[:, :W2], xx[:, W2:])

    SU = U // 2
    AA = [jnp.concatenate([G[2 * s][:W2, :W2], G[2 * s + 1][:W2, :W2]], axis=1) for s in range(SU)]
    SS = [jnp.concatenate([eye, eye], axis=1) + AA[s] for s in range(SU)]
    AAb = [AA[s].astype(BF16) for s in range(SU)]
    QQ = yield from each(lambda s: _dot(AAb[s], diag2_of(AAb[s])), SU)
    for _ in range(int(math.log2(C)) - 2):
        QQb = [QQ[s].astype(BF16) for s in range(SU)]
        QS = yield from each(lambda s: _dot(jnp.concatenate([QQb[s], SS[s].astype(BF16)], axis=0),
                                            diag2_of(QQb[s])), SU)
        QQ = [QS[s][:W2] for s in range(SU)]
        SS = [SS[s] + QS[s][W2:] for s in range(SU)]
    SS = yield from each(lambda s: (SS[s] + _dot(SS[s].astype(BF16), diag2_of(QQ[s].astype(BF16)))).astype(BF16),
                         SU)
    S = [SS[i // 2][:, (i % 2) * W2:(i % 2 + 1) * W2] for i in range(U)]
    akv2 = yield from each(
        lambda s: _dot(jnp.concatenate([G[2 * s][:W2, W2:], G[2 * s + 1][:W2, W2:]], axis=1).astype(BF16),
                       diag2(v_u[2 * s], v_u[2 * s + 1])), SU)
    akv = [akv2[i // 2][:, (i % 2) * W2:(i % 2 + 1) * W2] for i in range(U)]
    WU = yield from each(
        lambda i: _dot(S[i], jnp.concatenate([ab_u[i], akv[i].astype(BF16)], axis=1)).astype(BF16), U)
    Z = [jnp.concatenate([WU[i], jnp.concatenate([zeros_b, v_u[i]], axis=1)], axis=0) for i in range(U)]
    QY = yield from each(lambda i: _dot(G[i][W2:, :].astype(BF16), Z[i]), U)
    BKgT = yield from each(lambda i: jnp.concatenate([bd(piece(bg, *units[i])).T, bd(piece(kg, *units[i])).T],
                                                     axis=1).astype(BF16), U)
    MN = yield from each(lambda i: _dot(BKgT[i], Z[i]), U)
    QM = [jnp.concatenate([AR[i][W2:].astype(F32) + QY[i][:, :W2], MN[i][:, :W2]], axis=0).astype(BF16)
          for i in range(U)]
    gcol = [jnp.sum(eye * piece(g_end, j, p)[0:1, :], axis=1, keepdims=True) for j, p in units]

    for s in range(SU):
        ia, ib = 2 * s, 2 * s + 1
        Ha, Hb = h_scr[units[ia][1]], h_scr[units[ib][1]]
        YH2 = _dot(jnp.concatenate([QM[ia], QM[ib]], axis=1), diag2(Ha.astype(BF16), Hb.astype(BF16)))
        for i, H, YH in ((ia, Ha, YH2[:, :W2]), (ib, Hb, YH2[:, W2:])):
            j, p = units[i]
            ybd = YH[:W2] + QY[i][:, W2:]
            y_scr[j * C:(j + 1) * C, p * W2:(p + 1) * W2] = ybd[:C] + ybd[C:]
            h_scr[p] = H * gcol[i] + YH[W2:] + MN[i][:, W2:]
        yield

    y = y_scr[...]
    inv_n = 1.0 / N
    mean = segsum(y) * inv_n
    d = y - mean
    yield
    var = segsum(d * d, split=False) * inv_n
    yn = d * lax.rsqrt(var + RWKV_GN_EPS) * ln_g + ln_b
    yield
    bonus = segsum(r * k2 * r_k, split=False) * v
    o_ref[...] = ((yn + bonus) * g).astype(o_ref.dtype)


def _ssd_body(z_ref, xs_ref, bc_ref, dt_ref, cwx_ref, cwbc_ref, vecx_ref, vecbc_ref, vecd_ref,
              eh_ref, ltri_ref, o_ref, st_scr, tailx_scr, tailbc_scr, y_scr):
    RB = xs_ref.shape[0]
    Q = SSD_CHUNK
    P, N = SSD_P, SSD_N
    GW = SSD_W // SSD_GROUPS

    def conv_silu(x_ref, tail_scr, cw_ref, b_ref):
        parts = []
        for q in range(x_ref.shape[1] // 256):
            cs = slice(q * 256, (q + 1) * 256)
            x_raw = x_ref[:, cs]
            ext = jnp.concatenate([tail_scr[:, cs], x_raw], axis=0)
            acc = b_ref[0:1, cs] + cw_ref[SSD_CONV - 1:SSD_CONV, cs] * x_raw
            for s in range(1, SSD_CONV):
                acc = acc + cw_ref[SSD_CONV - 1 - s:SSD_CONV - s, cs] * pltpu.roll(ext, s, 0)[8:, :]
            tail_scr[:, cs] = x_raw[RB - 8:, :]
            parts.append(_silu(acc))
            yield
        return jnp.concatenate(parts, axis=1)

    xs = yield from conv_silu(xs_ref, tailx_scr, cwx_ref, vecx_ref)
    bc = yield from conv_silu(bc_ref, tailbc_scr, cwbc_ref, vecbc_ref)

    eh = eh_ref[...]
    dt = _softplus(dt_ref[...] + vecd_ref[0:1, :])
    a = dt * vecd_ref[1:2, :]
    acum = _dot_sel_lhs(ltri_ref[...], a)
    dt_e = _dot_sel_rhs(dt, eh)
    acum_e = _dot_sel_rhs(acum, eh)
    acum_t = acum.T
    yield
    xdt = xs * dt_e
    xdt_b = xdt.astype(BF16)
    ea = jnp.exp(acum_e)
    bc_b = bc.astype(BF16)
    yield

    li = lax.broadcasted_iota(jnp.int32, (Q, Q), 0)
    si = lax.broadcasted_iota(jnp.int32, (Q, Q), 1)
    causal = si <= li

    for ch in range(RB // Q):
        rs = slice(ch * Q, (ch + 1) * Q)
        a_last = acum_e[(ch + 1) * Q - 1:(ch + 1) * Q, :]
        xdte = (xdt[rs] * jnp.exp(a_last - acum_e[rs])).astype(BF16)
        chunk_decay = jnp.exp(a_last)
        for gi in range(SSD_GROUPS):
            gs = slice(gi * GW, (gi + 1) * GW)
            Bg = bc_b[rs, gi * N:(gi + 1) * N]
            Cg = bc_b[rs, SSD_GROUPS * N + gi * N:SSD_GROUPS * N + (gi + 1) * N]
            cb = _dot_nt(Cg, Bg)
            prev = st_scr[gi]
            y_off = _dot(Cg, prev.astype(BF16)) * ea[rs, gs]
            st_scr[gi] = prev * chunk_decay[:, gs] + _dot(bc[rs, gi * N:(gi + 1) * N].T.astype(BF16), xdte[:, gs])
            yield
            ys = []
            for j in range(SSD_HEADS // SSD_GROUPS):
                hh = gi * (SSD_HEADS // SSD_GROUPS) + j
                hs = slice(hh * P, (hh + 1) * P)
                seg = acum_e[rs, hs] - acum_t[hh:hh + 1, rs]
                wm = jnp.where(causal, cb * jnp.exp(seg), 0.0).astype(BF16)
                ys.append(_dot(wm, xdt_b[rs, hs]))
                if j % 2 == 1:
                    yield
            y_scr[rs, gs] = jnp.concatenate(ys, axis=1) + y_off

    for gi in range(SSD_GROUPS):
        gs = slice(gi * GW, (gi + 1) * GW)
        yg = (y_scr[:, gs] + vecx_ref[2:3, gs] * xs[:, gs]) * _silu(z_ref[:, gs])
        yg = yg * lax.rsqrt(jnp.mean(yg * yg, axis=-1, keepdims=True) + EPS)
        o_ref[:, gs] = (yg * vecx_ref[1:2, gs]).astype(o_ref.dtype)
        yield


N_RWKV_IN, N_SSD_IN = 11, 11
N_RWKV_SCR, N_SSD_SCR = 4, 4
RWKV_TURN, SSD_TURN = 4, 1


def _rwkv_ssd_kernel(*refs):
    n_in = N_RWKV_IN + N_SSD_IN
    rwkv_in, ssd_in = refs[:N_RWKV_IN], refs[N_RWKV_IN:n_in]
    o_rwkv, o_ssd = refs[n_in], refs[n_in + 1]
    scr = refs[n_in + 2:]
    rwkv_scr, ssd_scr = scr[:N_RWKV_SCR], scr[N_RWKV_SCR:]

    @pl.when(pl.program_id(1) == 0)
    def _():
        for s in rwkv_scr[:3] + ssd_scr[:3]:
            s[...] = jnp.zeros_like(s)

    _interleave((_rwkv_body(*rwkv_in, o_rwkv, *rwkv_scr), RWKV_TURN), (_ssd_body(*ssd_in, o_ssd, *ssd_scr), SSD_TURN))


def _interleave(*weighted):
    live = list(weighted)
    while live:
        for item in list(live):
            gen, turn = item
            for _ in range(turn):
                try:
                    next(gen)
                except StopIteration:
                    live.remove(item)
                    break


def _rwkv_ssd_group(p0, rwkv_cols, ssd_cols, batch, seq_len, vec, vec2, w2a2, g2, cwx, cwbc, vecx, vecbc, vecd):
    C, Q = RWKV_CHUNK, SSD_CHUNK
    assert Q == SSD_P
    RB = C * RWKV_BLOCK_CHUNKS
    assert RB == Q * SSD_BLOCK_CHUNKS
    nc = seq_len // RB
    T = batch * seq_len
    cr, ck, cv, czwa, czg = rwkv_cols
    cz, cx, cbc, cdt = ssd_cols
    eseg = jnp.asarray(np.arange(256)[:, None] // RWKV_N == np.arange(256)[None, :] // RWKV_N, BF16)
    same_chunk = np.arange(RB)[:, None] // C == np.arange(RB)[None, :] // C
    ltri = jnp.asarray(np.tril(np.ones((RB, RB))) * same_chunk, BF16)
    csum = ltri
    eh = np.zeros((128, SSD_W), np.float32)
    for hh in range(SSD_HEADS):
        eh[hh, hh * SSD_P:(hh + 1) * SSD_P] = 1.0
    eh = jnp.asarray(eh, BF16)

    def col(width, idx):
        return pl.BlockSpec((RB, width), lambda b, c: (b * nc + c, idx))

    def full(arr):
        return pl.BlockSpec(arr.shape, lambda b, c: (0,) * arr.ndim)

    rwkv_in = [col(RWKV_W, cr), col(RWKV_W, ck), col(RWKV_W, cv), col(128, czwa), col(128, czg),
               full(vec), full(vec2), full(w2a2), full(g2), full(eseg), full(csum)]
    ssd_in = [col(SSD_W, cz), col(SSD_W, cx), col(512, cbc), col(128, cdt),
              full(cwx), full(cwbc), full(vecx), full(vecbc), full(vecd), full(eh), full(ltri)]
    assert len(rwkv_in) == N_RWKV_IN and len(ssd_in) == N_SSD_IN
    rwkv_scr = [pltpu.VMEM((RWKV_HEADS // 2, 2 * RWKV_N, 2 * RWKV_N), F32),
                pltpu.VMEM((8, RWKV_W), F32), pltpu.VMEM((8, 128), F32), pltpu.VMEM((RB, RWKV_W), F32)]
    ssd_scr = [pltpu.VMEM((SSD_GROUPS, SSD_N, SSD_W // SSD_GROUPS), F32),
               pltpu.VMEM((8, SSD_W), F32), pltpu.VMEM((8, 512), F32), pltpu.VMEM((RB, SSD_W), F32)]
    return pl.pallas_call(
        _rwkv_ssd_kernel,
        grid=(batch, nc),
        in_specs=rwkv_in + ssd_in,
        out_specs=[pl.BlockSpec((RB, RWKV_W), lambda b, c: (b * nc + c, 0)),
                   pl.BlockSpec((RB, SSD_W), lambda b, c: (b * nc + c, 0))],
        out_shape=[jax.ShapeDtypeStruct((T, RWKV_W), BF16), jax.ShapeDtypeStruct((T, SSD_W), BF16)],
        scratch_shapes=rwkv_scr + ssd_scr,
        compiler_params=_cparams(("arbitrary", "arbitrary")),
    )(p0, p0, p0, p0, p0, vec, vec2, w2a2, g2, eseg, csum,
      p0, p0, p0, p0, cwx, cwbc, vecx, vecbc, vecd, eh, ltri)


def _s5_kernel(u_ref, blk_ref, bz_ref, cz_ref, lam_ref, d_ref, wglu_ref, bglu_ref, o_ref,
               st_scr, zre_scr, zim_scr, sre_scr, sim_scr, y_scr, toep_ref, fold_scr):
    i = pl.program_id(1)
    Mb = u_ref.shape[0]
    LC, NT = S5_LC, S5_TILES
    HW = S5_W // NT * (S5_P // S5_H)

    @pl.when(jnp.logical_and(pl.program_id(0) == 0, i == 0))
    def _():
        toep_ref[...] = jnp.zeros_like(toep_ref)
        for n in range(NT):
            for li in range(LC):
                for lo in range(li, LC):
                    toep_ref[n, li * 128:(li + 1) * 128, lo * 128:(lo + 1) * 128] = blk_ref[lo - li, n]

    @pl.when(i == 0)
    def _():
        st_scr[...] = jnp.zeros_like(st_scr)

    u = u_ref[...]
    ucat = []
    for n in range(NT):
        un = jnp.concatenate([u[:, l * S5_W + n * 128:l * S5_W + (n + 1) * 128] for l in range(LC)],
                             axis=1).astype(BF16)
        ucat.append(un)
        z = _dot(un, bz_ref[n])
        zre_scr[:, n * HW:(n + 1) * HW] = z[:, :HW]
        zim_scr[:, n * HW:(n + 1) * HW] = z[:, HW:]

    lr = lam_ref[0:1, :]
    lim = lam_ref[1:2, :]

    def step(m, carry):
        sre, sim = carry
        sre_scr[pl.ds(m, 1), :] = sre
        sim_scr[pl.ds(m, 1), :] = sim
        zr = zre_scr[pl.ds(m, 1), :]
        zi = zim_scr[pl.ds(m, 1), :]
        return lr * sre - lim * sim + zr, lr * sim + lim * sre + zi

    sre, sim = lax.fori_loop(0, Mb, step, (st_scr[0:1, :], st_scr[1:2, :]))
    st_scr[0:1, :] = sre
    st_scr[1:2, :] = sim

    for n in range(NT):
        s_re = sre_scr[:, n * HW:(n + 1) * HW].astype(BF16)
        s_im = sim_scr[:, n * HW:(n + 1) * HW].astype(BF16)
        yn = _dot(ucat[n], toep_ref[n]) + _dot(s_re, cz_ref[n, :HW, :]) + _dot(s_im, cz_ref[n, HW:, :])
        for l in range(LC):
            y_scr[:, l * S5_W + n * 128:l * S5_W + (n + 1) * 128] = yn[:, l * 128:(l + 1) * 128]

    y = y_scr[...] + d_ref[...] * u
    y = 0.5 * y * (1.0 + jnp.tanh(math.sqrt(2.0 / math.pi) * (y + 0.044715 * (y * y * y))))
    wglu = wglu_ref[...]
    for l in range(LC):
        yl = y[:, l * S5_W:(l + 1) * S5_W]
        gate = _sigmoid(_dot(yl.astype(BF16), wglu) + bglu_ref[...])
        out_l = yl * gate
        for n in range(NT):
            fold_scr[n, pl.ds(l, Mb, stride=LC), :] = out_l[:, n * 128:(n + 1) * 128]
    for n in range(NT):
        o_ref[:, n * 128:(n + 1) * 128] = fold_scr[n]


def _s5_tables(lam_re, lam_im, log_dt, b_re, b_im, c_re, c_im):
    LC, NT, G, H, Pn = S5_LC, S5_TILES, S5_GROUPS, S5_H, S5_P
    GL = G // NT
    hp = lax.Precision.HIGHEST
    dt = jnp.exp(log_dt)[:, None]
    ar, ai = lam_re * dt, lam_im * dt
    jj = jnp.arange(LC + 1, dtype=F32)[:, None, None]
    mag = jnp.exp(jj * ar[None])
    pw_re, pw_im = mag * jnp.cos(jj * ai[None]), mag * jnp.sin(jj * ai[None])
    nr, ni = pw_re[1] - 1.0, pw_im[1]
    den = lam_re * lam_re + lam_im * lam_im
    qr, qi = (nr * lam_re + ni * lam_im) / den, (ni * lam_re - nr * lam_im) / den
    bb_re = qr[..., None] * b_re - qi[..., None] * b_im
    bb_im = qr[..., None] * b_im + qi[..., None] * b_re
    cl_re = c_re[None] * pw_re[:, :, None, :] - c_im[None] * pw_im[:, :, None, :]
    cl_im = c_re[None] * pw_im[:, :, None, :] + c_im[None] * pw_re[:, :, None, :]
    kd = (jnp.einsum("jghp,gpk->jghk", cl_re[:LC], bb_re, precision=hp)
          - jnp.einsum("jghp,gpk->jghk", cl_im[:LC], bb_im, precision=hp))
    def spread(x, rows_per_group, cols_per_group):
        sel = np.tile(np.eye(cols_per_group, dtype=np.float32), (1, GL))
        keep = (np.arange(GL * rows_per_group)[:, None] // rows_per_group
                == np.arange(GL * cols_per_group)[None, :] // cols_per_group)
        return jnp.where(keep, jnp.einsum("...rc,cd->...rd", x, jnp.asarray(sel)), 0.0)

    blk = spread(jnp.swapaxes(kd, -1, -2).reshape(LC, NT, GL * H, H), H, H).astype(BF16)
    def cz_part(cl):
        x = spread(jnp.swapaxes(cl[1:], -1, -2).reshape(LC, NT, GL * Pn, H), Pn, H).astype(BF16)
        return jnp.concatenate([x[l] for l in range(LC)], axis=-1)
    cz = jnp.concatenate([cz_part(cl_re), cz_part(-cl_im)], axis=1)
    jr = (LC - 1.0) - jnp.arange(LC, dtype=F32)[:, None, None]
    mag_r = jnp.exp(jr * ar[None])
    rev_re, rev_im = mag_r * jnp.cos(jr * ai[None]), mag_r * jnp.sin(jr * ai[None])
    bl_re = rev_re[..., None] * bb_re[None] - rev_im[..., None] * bb_im[None]
    bl_im = rev_re[..., None] * bb_im[None] + rev_im[..., None] * bb_re[None]

    def bz_part(bl):
        x = spread(jnp.swapaxes(bl, -1, -2).reshape(LC, NT, GL * H, Pn), H, Pn).astype(BF16)
        return jnp.swapaxes(x, 0, 1).reshape(NT, LC * GL * H, GL * Pn)
    bz = jnp.concatenate([bz_part(bl_re), bz_part(bl_im)], axis=2)
    lam = _rows([pw_re[LC].reshape(-1), pw_im[LC].reshape(-1)], 8)
    return blk, bz, cz, lam


def _s5_group(p1, col_u, batch, seq_len, tables, d_skip, w_glu, b_glu):
    LC = S5_LC
    T = batch * seq_len
    blk, bz, cz, lam = tables
    rows_per_seq = seq_len // LC
    Mb = min(S5_ROWS, rows_per_seq)
    nb = rows_per_seq // Mb
    NPc = p1.shape[1]
    W = LC * S5_W
    d_t = jnp.tile(d_skip.reshape(1, S5_W), (1, LC))
    SW = S5_TILES * (S5_GROUPS // S5_TILES) * S5_P

    def full(arr):
        return pl.BlockSpec(arr.shape, lambda b, i: (0,) * arr.ndim)

    def kern(u_ref, *rest):
        ucat_scr, fold_scr = rest[-2], rest[-1]
        for n in range(S5_TILES):
            fold_scr[n] = u_ref[:, n * 128:(n + 1) * 128]
        for n in range(S5_TILES):
            for l in range(LC):
                ucat_scr[:, l * S5_W + n * 128:l * S5_W + (n + 1) * 128] = fold_scr[n, pl.ds(l, Mb, stride=LC), :]
        _s5_kernel(ucat_scr, *rest[:-2], fold_scr)

    return pl.pallas_call(
        kern,
        grid=(batch, nb),
        in_specs=[pl.BlockSpec((Mb * LC, S5_W), lambda b, i: (b * nb + i, col_u)),
                  _resident(blk.shape), _resident(bz.shape), _resident(cz.shape), full(lam), full(d_t),
                  full(w_glu),
                  pl.BlockSpec((1, S5_W), lambda b, i: (0, 0))],
        out_specs=pl.BlockSpec((Mb * LC, S5_W), lambda b, i: (b * nb + i, 0)),
        out_shape=jax.ShapeDtypeStruct((T, S5_W), F32),
        scratch_shapes=[pltpu.VMEM((8, SW), F32),
                        pltpu.VMEM((Mb, SW), F32), pltpu.VMEM((Mb, SW), F32),
                        pltpu.VMEM((Mb, SW), F32), pltpu.VMEM((Mb, SW), F32),
                        pltpu.VMEM((Mb, W), F32),
                        pltpu.VMEM((S5_TILES, LC * 128, LC * 128), BF16),
                        pltpu.VMEM((Mb, W), F32),
                        pltpu.VMEM((S5_TILES, Mb * LC, 128), F32)],
        compiler_params=_cparams(("arbitrary", "arbitrary")),
    )(p1, blk, bz, cz, lam, d_t, w_glu, b_glu.reshape(1, S5_W))


def _ret_kernel(q_ref, k_ref, v_ref, g_ref, cos_ref, sin_ref, intra_ref, kdec_ref, qdec_ref, o_ref,
                st_scr, *, chunk_decay):
    c = pl.program_id(1)
    DK, DV = RET_DK, RET_DV

    @pl.when(c == 0)
    def _():
        st_scr[...] = jnp.zeros_like(st_scr)

    Q = intra_ref.shape[1]
    scale = DK ** -0.5
    for ch in range(q_ref.shape[0] // Q):
        rs = slice(ch * Q, (ch + 1) * Q)
        cosf = cos_ref[rs, :]
        sinf = sin_ref[rs, :]
        for h in range(RET_HEADS):
            ks = slice(h * DK, (h + 1) * DK)
            vs = slice(h * DV, (h + 1) * DV)
            qh = q_ref[rs, ks]
            kh = k_ref[rs, ks]
            qh = qh * cosf + pltpu.roll(qh, DK // 2, 1) * sinf
            kh = (kh * cosf + pltpu.roll(kh, DK // 2, 1) * sinf) * scale
            vh = v_ref[rs, vs].astype(BF16)
            scores = (_dot_nt(qh.astype(BF16), kh.astype(BF16)) * intra_ref[h]).astype(BF16)
            prev = st_scr[h]
            y = _dot(scores, vh) + _dot((qh * qdec_ref[:, ks]).astype(BF16), prev.astype(BF16))
            kd = (kh * kdec_ref[:, ks]).T.astype(BF16)
            st_scr[h] = prev * chunk_decay[h] + _dot(kd, vh)
            y = y * lax.rsqrt(jnp.mean(y * y, axis=-1, keepdims=True) + EPS)
            o_ref[rs, vs] = (_silu(g_ref[rs, vs]) * y).astype(o_ref.dtype)


def _ret_group(p1, cols, batch, seq_len):
    Q = min(RET_CHUNK, seq_len)
    RB = min(Q * RET_BLOCK_CHUNKS, seq_len)
    nc = seq_len // RB
    T = batch * seq_len
    cq, ck, cv, cg = cols
    H, DK, DV = RET_HEADS, RET_DK, RET_DV
    half = DK // 2
    theta = 1.0 / (10000.0 ** jnp.linspace(0.0, 1.0, half, dtype=F32))
    ang = jnp.arange(seq_len).astype(F32)[:, None] * theta[None, :]
    cos, sin = jnp.cos(ang), jnp.sin(ang)
    cosf = jnp.concatenate([cos, cos], axis=1)
    sinf = jnp.concatenate([-sin, sin], axis=1)
    log_gamma = jnp.log(1.0 - 2.0 ** (-5.0 - jnp.arange(H, dtype=F32)))
    t = jnp.arange(Q, dtype=F32)
    diff = t[:, None] - t[None, :]
    intra = jnp.where((diff >= 0)[None], jnp.exp(jnp.maximum(diff, 0.0)[None] * log_gamma[:, None, None]), 0.0)
    kdec = jnp.repeat(jnp.exp((Q - 1.0 - t)[:, None] * log_gamma[None, :]), DK, axis=1)
    qdec = jnp.repeat(jnp.exp((t + 1.0)[:, None] * log_gamma[None, :]), DK, axis=1)
    chunk_decay = tuple(float((1.0 - 2.0 ** (-5.0 - h)) ** Q) for h in range(H))

    def col(width, idx):
        return pl.BlockSpec((RB, width), lambda b, c: (b * nc + c, idx))

    def full(arr):
        return pl.BlockSpec(arr.shape, lambda b, c: (0,) * arr.ndim)

    return pl.pallas_call(
        functools.partial(_ret_kernel, chunk_decay=chunk_decay),
        grid=(batch, nc),
        in_specs=[col(H * DK, cq), col(H * DK, ck), col(H * DV, cv), col(H * DV, cg),
                  pl.BlockSpec((RB, DK), lambda b, c: (c, 0)), pl.BlockSpec((RB, DK), lambda b, c: (c, 0)),
                  full(intra), full(kdec), full(qdec)],
        out_specs=pl.BlockSpec((RB, H * DV), lambda b, c: (b * nc + c, 0)),
        out_shape=jax.ShapeDtypeStruct((T, H * DV), BF16),
        scratch_shapes=[pltpu.VMEM((H, DK, DV), F32)],
        compiler_params=_cparams(("arbitrary", "arbitrary")),
    )(p1, p1, p1, p1, cosf, sinf, intra, kdec, qdec)


def _pad_cols(w, n):
    return jnp.pad(w, ((0, 0), (0, n - w.shape[1])))


def _rows(vectors, n_rows):
    width = vectors[0].shape[0]
    pad = [jnp.zeros((n_rows - len(vectors), width), F32)] if n_rows > len(vectors) else []
    return jnp.concatenate([v.astype(F32).reshape(1, width) for v in vectors] + pad, axis=0)


def kernel(x, l0_norm_mix, l0_w_in, l0_rwkv_mu, l0_rwkv_w0, l0_rwkv_w2, l0_rwkv_a0, l0_rwkv_a2, l0_rwkv_g2, l0_rwkv_k_k, l0_rwkv_k_a, l0_rwkv_r_k, l0_rwkv_ln_g, l0_rwkv_ln_b, l0_ssd_conv_w, l0_ssd_conv_b, l0_ssd_dt_bias, l0_ssd_a_log, l0_ssd_d, l0_ssd_norm_g, l0_w_out, l0_norm_ffn, l0_ffn_up, l0_ffn_conv_w, l0_ffn_conv_b, l0_ffn_down, l1_norm_mix, l1_w_in, l1_s5_lam_re, l1_s5_lam_im, l1_s5_log_dt, l1_s5_b_re, l1_s5_b_im, l1_s5_c_re, l1_s5_c_im, l1_s5_d, l1_s5_w_glu, l1_s5_b_glu, l1_w_out, l1_norm_ffn, l1_ffn_up, l1_ffn_conv_w, l1_ffn_conv_b, l1_ffn_down, final_norm):
    B, L, D = x.shape
    T = B * L
    x2 = x.reshape(T, D).astype(F32)

    RP = 3 * RWKV_W + 256
    o_z = RP
    o_xbc = o_z + SSD_W
    o_dt = o_xbc + SSD_W + 2 * SSD_GROUPS * SSD_N
    n_xbc = o_dt - o_xbc
    w0 = _pad_cols(l0_w_in, o_dt + 128).astype(BF16)
    p0 = _norm_proj(x2, l0_norm_mix, w0, ((o_z, 0, SSD_W), (o_xbc, SSD_W, n_xbc), (0, SSD_W + n_xbc, RP),
                                          (o_dt, SSD_W + n_xbc + RP, 128)))

    mu = l0_rwkv_mu
    vec = _rows([mu[0:512], mu[512:1024], mu[1024:1536], l0_rwkv_w0, l0_rwkv_a0, l0_rwkv_k_k, l0_rwkv_k_a,
                 l0_rwkv_r_k.reshape(-1), l0_rwkv_ln_g, l0_rwkv_ln_b], 16)
    vec2 = _rows([mu[1536:1664], mu[1664:1792]], 8)
    zl = jnp.zeros((64, RWKV_W), F32)
    w2a2 = jnp.concatenate([jnp.concatenate([l0_rwkv_w2.astype(F32), zl], axis=1),
                            jnp.concatenate([zl, l0_rwkv_a2.astype(F32)], axis=1)], axis=0).astype(BF16)

    cw = l0_ssd_conv_w
    cb = l0_ssd_conv_b
    cwx = _rows([cw[k, :SSD_W] for k in range(SSD_CONV)], 8)
    cwbc = _rows([cw[k, SSD_W:] for k in range(SSD_CONV)], 8)
    vecx = _rows([cb[:SSD_W], l0_ssd_norm_g, jnp.repeat(l0_ssd_d, SSD_P)], 8)
    vecbc = _rows([cb[SSD_W:]], 8)
    hpad = jnp.zeros((128 - SSD_HEADS,), F32)
    vecd = _rows([jnp.concatenate([l0_ssd_dt_bias.astype(F32), hpad]),
                  jnp.concatenate([-jnp.exp(l0_ssd_a_log.astype(F32)), hpad])], 8)
    y_a, y_b = _rwkv_ssd_group(p0, (5, 6, 7, 32, 33), (0, 1, 4, 34), B, L, vec, vec2, w2a2,
                               l0_rwkv_g2.astype(BF16), cwx, cwbc, vecx, vecbc, vecd)

    x2 = _mix_ffn(x2, y_a, y_b, l0_w_out.astype(BF16), L, l0_norm_ffn, l0_ffn_up.astype(BF16),
                  l0_ffn_conv_w.astype(F32), l0_ffn_conv_b.astype(F32), l0_ffn_down.astype(BF16),
                  final_norm, False)

    p1 = _norm_proj(x2, l1_norm_mix, l1_w_in.astype(BF16), ((1536, 0, 1024), (2560, 1024, 1024), (0, 2048, 1536)))
    tables = _s5_tables(*(t.astype(F32) for t in (l1_s5_lam_re, l1_s5_lam_im, l1_s5_log_dt, l1_s5_b_re,
                                                   l1_s5_b_im, l1_s5_c_re, l1_s5_c_im)))
    y_c = _s5_group(p1, 4, B, L, tables, l1_s5_d.astype(F32), l1_s5_w_glu.astype(BF16),
                    l1_s5_b_glu.astype(F32))
    y_d = _ret_group(p1, (5, 6, 0, 1), B, L)
    x2 = _mix_ffn(x2, y_c, y_d, l1_w_out.astype(BF16), L, l1_norm_ffn, l1_ffn_up.astype(BF16),
                  l1_ffn_conv_w.astype(F32), l1_ffn_conv_b.astype(F32), l1_ffn_down.astype(BF16),
                  final_norm, True)
    return x2.reshape(B, L, D).astype(x.dtype)
```

```python
import functools
import math

import numpy as np
import jax
import jax.numpy as jnp
from jax import lax
from jax.experimental import pallas as pl
from jax.experimental.pallas import tpu as pltpu

F32 = jnp.float32
BF16 = jnp.bfloat16

EPS = 1e-6
D_MODEL = 1024

RWKV_HEADS = 8
RWKV_N = 64
RWKV_W = 512
RWKV_GN_EPS = 64e-5
RWKV_CHUNK = 64
RWKV_BLOCK_CHUNKS = 4
RWKV_WAVES = 1

SSD_HEADS = 16
SSD_P = 64
SSD_GROUPS = 2
SSD_N = 128
SSD_W = 1024
SSD_CONV = 4
SSD_CHUNK = 64
SSD_BLOCK_CHUNKS = 4

S5_W = 512
S5_GROUPS = 32
S5_H = 16
S5_P = 64
S5_LC = 8
S5_TILES = 4
S5_ROWS = 128

RET_HEADS = 4
RET_DK = 128
RET_DV = 256
RET_CHUNK = 256
RET_BLOCK_CHUNKS = 2

FFN_HIDDEN = 2816
FFN_TN = 256
FFN_HALO = 16

VMEM_LIMIT = 48 * 1024 * 1024


def _dot(a, b):
    return jnp.dot(a, b, preferred_element_type=F32)


def _dot_nt(a, b):
    return lax.dot_general(a, b, (((1,), (1,)), ((), ())), preferred_element_type=F32)


def _dot_tn(a, b):
    return lax.dot_general(a, b, (((0,), (0,)), ((), ())), preferred_element_type=F32)


def _split3(x):
    hi = x.astype(BF16)
    r1 = x - hi.astype(F32)
    mid = r1.astype(BF16)
    lo = (r1 - mid.astype(F32)).astype(BF16)
    return hi, mid, lo


def _dot_sel_rhs(x, sel):
    hi, mid, lo = _split3(x)
    return _dot(hi, sel) + _dot(mid, sel) + _dot(lo, sel)


def _dot_sel_lhs(sel, x):
    hi, mid, lo = _split3(x)
    return _dot(sel, hi) + _dot(sel, mid) + _dot(sel, lo)


def _sigmoid(x):
    return 1.0 / (1.0 + jnp.exp(-x))


def _silu(x):
    return x * _sigmoid(x)


def _softplus(x):
    return jnp.maximum(x, 0.0) + jnp.log(1.0 + jnp.exp(-jnp.abs(x)))


def _cparams(sem):
    return pltpu.CompilerParams(dimension_semantics=sem, vmem_limit_bytes=VMEM_LIMIT)


def _proj_kernel(x_ref, g_ref, w_ref, o_ref, *, pieces):
    x = x_ref[...]
    y = x * lax.rsqrt(jnp.mean(x * x, axis=-1, keepdims=True) + EPS)
    hn = (y * g_ref[...]).astype(BF16)
    for src, dst, width in pieces:
        o_ref[:, dst:dst + width] = _dot(hn, w_ref[:, src:src + width])


def _norm_proj(x2, g, w, pieces, tm=512):
    T, D = x2.shape
    N = sum(p[2] for p in pieces)
    return pl.pallas_call(
        functools.partial(_proj_kernel, pieces=pieces),
        grid=(T // tm,),
        in_specs=[pl.BlockSpec((tm, D), lambda i: (i, 0)), _resident((1, D)), _resident(w.shape)],
        out_specs=pl.BlockSpec((tm, N), lambda i: (i, 0)),
        out_shape=jax.ShapeDtypeStruct((T, N), F32),
        compiler_params=_cparams(("parallel",)),
    )(x2, g.reshape(1, D), w)


def _ffn_kernel(x_ref, xh_ref, ya_ref, yah_ref, yb_ref, ybh_ref, wo_ref, g_ref, wup_ref,
                cw_ref, cb_ref, wd_ref, gf_ref, o_ref, act_scr, *, blocks_per_seq, final_norm):
    i = pl.program_id(0)
    Hd = wd_ref.shape[0]
    tn = FFN_TN
    Ka = ya_ref.shape[1]

    def mixed(x, ya, yb):
        return x + _dot(ya.astype(BF16), wo_ref[0:Ka, :]) + _dot(yb.astype(BF16), wo_ref[Ka:, :])

    def norm(x):
        y = x * lax.rsqrt(jnp.mean(x * x, axis=-1, keepdims=True) + EPS)
        return (y * g_ref[...]).astype(BF16)

    x1 = mixed(x_ref[...], ya_ref[...], yb_ref[...])
    keep = jnp.where(i % blocks_per_seq == 0, 0.0, 1.0)
    x1h = mixed(xh_ref[...], yah_ref[...], ybh_ref[...]) * keep
    hn = jnp.concatenate([norm(x1h), norm(x1)], axis=0)

    def conv(u, cols):
        u1 = pltpu.roll(u, 1, 0)
        u2 = pltpu.roll(u, 2, 0)
        c = cw_ref[2:3, cols] * u + cw_ref[1:2, cols] * u1 + cw_ref[0:1, cols] * u2 + cb_ref[:, cols]
        return c[FFN_HALO:, :]

    for j in range(Hd // tn):
        gs = slice(j * tn, (j + 1) * tn)
        vs = slice(Hd + j * tn, Hd + (j + 1) * tn)
        cg = conv(_dot(hn, wup_ref[:, gs]), gs)
        cv = conv(_dot(hn, wup_ref[:, vs]), vs)
        act_scr[:, gs] = (_silu(cg) * cv).astype(BF16)

    y = x1 + _dot(act_scr[...], wd_ref[...])
    if final_norm:
        y = y * lax.rsqrt(jnp.mean(y * y, axis=-1, keepdims=True) + EPS) * gf_ref[...]
    o_ref[...] = y


def _resident(shape):
    return pl.BlockSpec(shape, lambda *_: (0,) * len(shape), pipeline_mode=pl.Buffered(1))


def _mix_ffn(x2, ya, yb, wo, seq_len, g, w_up, conv_w, conv_b, w_down, g_final, final_norm, tm=512):
    T, D = x2.shape
    Hd = w_down.shape[0]
    Ka, Kb = ya.shape[1], yb.shape[1]
    tm = min(tm, seq_len)
    hb = tm // FFN_HALO
    kern = functools.partial(_ffn_kernel, blocks_per_seq=seq_len // tm, final_norm=final_norm)

    def rows(width):
        return pl.BlockSpec((tm, width), lambda i: (i, 0))

    def halo(width):
        return pl.BlockSpec((FFN_HALO, width), lambda i: (jnp.maximum(i * hb - 1, 0), 0))

    return pl.pallas_call(
        kern,
        grid=(T // tm,),
        in_specs=[rows(D), halo(D), rows(Ka), halo(Ka), rows(Kb), halo(Kb),
                  _resident((Ka + Kb, D)),
                  _resident((1, D)), _resident((D, 2 * Hd)), _resident((3, 2 * Hd)), _resident((1, 2 * Hd)),
                  _resident((Hd, D)), _resident((1, D))],
        out_specs=pl.BlockSpec((tm, D), lambda i: (i, 0)),
        out_shape=jax.ShapeDtypeStruct((T, D), F32),
        scratch_shapes=[pltpu.VMEM((tm, Hd), BF16)],
        compiler_params=_cparams(("parallel",)),
    )(x2, x2, ya, ya, yb, yb, wo, g.reshape(1, D), w_up, conv_w, conv_b.reshape(1, -1), w_down,
      g_final.reshape(1, D))


def _rwkv_body(r_ref, k_ref, v_ref, zwa_ref, zg_ref, vec_ref, vec2_ref, w2a2_ref, g2_ref,
               eseg_ref, csum_ref, o_ref, h_scr, prev_scr, prev2_scr, y_scr):
    RB = r_ref.shape[0]
    C = RWKV_CHUNK
    N = RWKV_N

    row = lax.broadcasted_iota(jnp.int32, (RB, 1), 0)

    def shift_lerp(x, prev_row, mu):
        xs = jnp.where(row == 0, prev_row, pltpu.roll(x, 1, 0))
        return x + (xs - x) * mu

    r_raw, k_raw, v_raw = r_ref[...], k_ref[...], v_ref[...]
    zwa_raw, zg_raw = zwa_ref[...], zg_ref[...]
    r = shift_lerp(r_raw, prev_scr[0:1, :], vec_ref[0:1, :])
    k = shift_lerp(k_raw, prev_scr[1:2, :], vec_ref[1:2, :])
    v = shift_lerp(v_raw, prev_scr[2:3, :], vec_ref[2:3, :])
    zwa = shift_lerp(zwa_raw, prev2_scr[0:1, :], vec2_ref[0:1, :])
    zg = shift_lerp(zg_raw, prev2_scr[1:2, :], vec2_ref[1:2, :])
    prev_scr[0:1, :] = r_raw[RB - 1:RB, :]
    prev_scr[1:2, :] = k_raw[RB - 1:RB, :]
    prev_scr[2:3, :] = v_raw[RB - 1:RB, :]
    prev2_scr[0:1, :] = zwa_raw[RB - 1:RB, :]
    prev2_scr[1:2, :] = zg_raw[RB - 1:RB, :]

    w0, a0 = vec_ref[3:4, :], vec_ref[4:5, :]
    k_k, k_a, r_k = vec_ref[5:6, :], vec_ref[6:7, :], vec_ref[7:8, :]
    ln_g, ln_b = vec_ref[8:9, :], vec_ref[9:10, :]

    lane = lax.broadcasted_iota(jnp.int32, (1, 128), 1)
    lora_in = jnp.where(lane < 64, jnp.tanh(zwa), zwa).astype(BF16)
    wa = _dot(lora_in, w2a2_ref[...])
    w_log = -_softplus(-(w0 + wa[:, :RWKV_W])) - 0.5
    logw = -jnp.exp(w_log)
    a = _sigmoid(a0 + wa[:, RWKV_W:])
    g = _dot(_sigmoid(zg).astype(BF16), g2_ref[...])

    eseg = eseg_ref[...]

    def segsum(x, split=True):
        EW = eseg.shape[0]
        hi = x.astype(BF16)
        lo = (x - hi.astype(F32)).astype(BF16) if split else None
        parts = []
        for q in range(RWKV_W // EW):
            qs = slice(q * EW, (q + 1) * EW)
            sq = _dot(hi[:, qs], eseg)
            parts.append(sq + _dot(lo[:, qs], eseg) if split else sq)
        return jnp.concatenate(parts, axis=1)

    yield
    kk = k * k_k
    kk = kk * lax.rsqrt(jnp.maximum(segsum(kk * kk), 1e-24))
    k2 = k * (1.0 + (a - 1.0) * k_a)
    yield

    lg = _dot_sel_lhs(csum_ref[...], logw)
    lg_end = jnp.concatenate([jnp.broadcast_to(lg[(j + 1) * C - 1:(j + 1) * C, :], (C, RWKV_W))
                              for j in range(RB // C)], axis=0)
    eg = jnp.exp(lg)
    egi = jnp.exp(-lg)
    g_end = jnp.exp(lg_end)
    d_end = jnp.exp(lg_end - lg)
    yield
    kka = kk * a
    ab = -kk * jnp.exp(lg - logw)
    rb = r * eg
    bt = kka * egi
    kt = k2 * egi
    bg = kka * d_end
    kg = k2 * d_end
    yield

    W2 = 2 * N
    left = lax.broadcasted_iota(jnp.int32, (1, W2), 1) < N

    def bd(x):
        return jnp.concatenate([jnp.where(left, x, 0.0), jnp.where(left, 0.0, x)], axis=0)

    ri = lax.broadcasted_iota(jnp.int32, (2 * W2, 2 * W2), 0)
    ci = lax.broadcasted_iota(jnp.int32, (2 * W2, 2 * W2), 1)
    mask = (ci & (N - 1)) < (ri & (N - 1)) + jnp.where(ri >= W2, 1, 0)
    e_r = lax.broadcasted_iota(jnp.int32, (W2, W2), 0)
    e_c = lax.broadcasted_iota(jnp.int32, (W2, W2), 1)
    eye = jnp.where(e_r == e_c, 1.0, 0.0)
    zeros_b = jnp.zeros((W2, W2), BF16)

    all_units = [(j, p) for j in range(RB // C) for p in range(RWKV_HEADS // 2)]

    def piece(x, j, p):
        return x[j * C:(j + 1) * C, p * W2:(p + 1) * W2]

    def each(fn, n):
        out = []
        for i in range(n):
            out.append(fn(i))
            yield
        return out

    def diag2(ya, yb):
        z = jnp.zeros_like(ya)
        return jnp.concatenate([jnp.concatenate([ya, z], axis=1), jnp.concatenate([z, yb], axis=1)], axis=0)

    def diag2_of(xx):
        return diag2(xx[:, :W2], xx[:, W2:])

    def wave(units):
        U = len(units)
        ab_u = yield from each(lambda i: bd(piece(ab, *units[i])).astype(BF16), U)
        AR = yield from each(lambda i: jnp.concatenate([ab_u[i], bd(piece(rb, *units[i])).astype(BF16)], axis=0), U)
        BK = yield from each(lambda i: jnp.concatenate([bd(piece(bt, *units[i])), bd(piece(kt, *units[i]))],
                                                       axis=0).astype(BF16), U)
        v_u = yield from each(lambda i: bd(piece(v, *units[i])).astype(BF16), U)
        G = yield from each(lambda i: jnp.where(mask, _dot_nt(AR[i], BK[i]), 0.0), U)
        SU = U // 2
        AA = [jnp.concatenate([G[2 * s][:W2, :W2], G[2 * s + 1][:W2, :W2]], axis=1) for s in range(SU)]
        SS = [jnp.concatenate([eye, eye], axis=1) + AA[s] for s in range(SU)]
        AAb = [AA[s].astype(BF16) for s in range(SU)]
        QQ = yield from each(lambda s: _dot(AAb[s], diag2_of(AAb[s])), SU)
        for _ in range(int(math.log2(C)) - 2):
            QQb = [QQ[s].astype(BF16) for s in range(SU)]
            QS = yield from each(lambda s: _dot(jnp.concatenate([QQb[s], SS[s].astype(BF16)], axis=0),
                                                diag2_of(QQb[s])), SU)
            QQ = [QS[s][:W2] for s in range(SU)]
            SS = [SS[s] + QS[s][W2:] for s in range(SU)]
        SS = yield from each(
            lambda s: (SS[s] + _dot(SS[s].astype(BF16), diag2_of(QQ[s].astype(BF16)))).astype(BF16), SU)
        S = [SS[i // 2][:, (i % 2) * W2:(i % 2 + 1) * W2] for i in range(U)]
        akv2 = yield from each(
            lambda s: _dot(jnp.concatenate([G[2 * s][:W2, W2:], G[2 * s + 1][:W2, W2:]], axis=1).astype(BF16),
                           diag2(v_u[2 * s], v_u[2 * s + 1])), SU)
        akv = [akv2[i // 2][:, (i % 2) * W2:(i % 2 + 1) * W2] for i in range(U)]
        WU = yield from each(
            lambda i: _dot(S[i], jnp.concatenate([ab_u[i], akv[i].astype(BF16)], axis=1)).astype(BF16), U)
        Z = [jnp.concatenate([WU[i], jnp.concatenate([zeros_b, v_u[i]], axis=1)], axis=0) for i in range(U)]
        QY = yield from each(lambda i: _dot(G[i][W2:, :].astype(BF16), Z[i]), U)
        BKgT = yield from each(lambda i: jnp.concatenate([bd(piece(bg, *units[i])).T, bd(piece(kg, *units[i])).T],
                                                         axis=1).astype(BF16), U)
        MN = yield from each(lambda i: _dot(BKgT[i], Z[i]), U)
        QM = [jnp.concatenate([AR[i][W2:].astype(F32) + QY[i][:, :W2], MN[i][:, :W2]], axis=0).astype(BF16)
              for i in range(U)]
        gcol = [jnp.sum(eye * piece(g_end, j, p)[0:1, :], axis=1, keepdims=True) for j, p in units]

        for s in range(SU):
            ia, ib = 2 * s, 2 * s + 1
            Ha, Hb = h_scr[units[ia][1]], h_scr[units[ib][1]]
            YH2 = _dot(jnp.concatenate([QM[ia], QM[ib]], axis=1), diag2(Ha.astype(BF16), Hb.astype(BF16)))
            for i, H, YH in ((ia, Ha, YH2[:, :W2]), (ib, Hb, YH2[:, W2:])):
                j, p = units[i]
                ybd = YH[:W2] + QY[i][:, W2:]
                y_scr[j * C:(j + 1) * C, p * W2:(p + 1) * W2] = ybd[:C] + ybd[C:]
                h_scr[p] = H * gcol[i] + YH[W2:] + MN[i][:, W2:]
            yield

    per_wave = len(all_units) // RWKV_WAVES
    for w in range(RWKV_WAVES):
        yield from wave(all_units[w * per_wave:(w + 1) * per_wave])

    y = y_scr[...]
    inv_n = 1.0 / N
    mean = segsum(y) * inv_n
    d = y - mean
    yield
    var = segsum(d * d, split=False) * inv_n
    yn = d * lax.rsqrt(var + RWKV_GN_EPS) * ln_g + ln_b
    yield
    bonus = segsum(r * k2 * r_k, split=False) * v
    o_ref[...] = ((yn + bonus) * g).astype(o_ref.dtype)


def _ssd_body(z_ref, xs_ref, bc_ref, dt_ref, cwx_ref, cwbc_ref, vecx_ref, vecbc_ref, vecd_ref,
              eh_ref, ltri_ref, o_ref, st_scr, tailx_scr, tailbc_scr, y_scr):
    RB = xs_ref.shape[0]
    Q = SSD_CHUNK
    P, N = SSD_P, SSD_N
    GW = SSD_W // SSD_GROUPS

    def conv_silu(x_ref, tail_scr, cw_ref, b_ref):
        parts = []
        for q in range(x_ref.shape[1] // 256):
            cs = slice(q * 256, (q + 1) * 256)
            x_raw = x_ref[:, cs]
            ext = jnp.concatenate([tail_scr[:, cs], x_raw], axis=0)
            acc = b_ref[0:1, cs] + cw_ref[SSD_CONV - 1:SSD_CONV, cs] * x_raw
            for s in range(1, SSD_CONV):
                acc = acc + cw_ref[SSD_CONV - 1 - s:SSD_CONV - s, cs] * pltpu.roll(ext, s, 0)[8:, :]
            tail_scr[:, cs] = x_raw[RB - 8:, :]
            parts.append(_silu(acc))
            yield
        return jnp.concatenate(parts, axis=1)

    xs = yield from conv_silu(xs_ref, tailx_scr, cwx_ref, vecx_ref)
    bc = yield from conv_silu(bc_ref, tailbc_scr, cwbc_ref, vecbc_ref)

    eh = eh_ref[...]
    dt = _softplus(dt_ref[...] + vecd_ref[0:1, :])
    a = dt * vecd_ref[1:2, :]
    acum = _dot_sel_lhs(ltri_ref[...], a)
    dt_e = _dot_sel_rhs(dt, eh)
    acum_e = _dot_sel_rhs(acum, eh)
    acum_t = acum.T
    yield
    xdt = xs * dt_e
    xdt_b = xdt.astype(BF16)
    ea = jnp.exp(acum_e)
    bc_b = bc.astype(BF16)
    yield

    li = lax.broadcasted_iota(jnp.int32, (Q, Q), 0)
    si = lax.broadcasted_iota(jnp.int32, (Q, Q), 1)
    causal = si <= li

    for ch in range(RB // Q):
        rs = slice(ch * Q, (ch + 1) * Q)
        a_last = acum_e[(ch + 1) * Q - 1:(ch + 1) * Q, :]
        xdte = (xdt[rs] * jnp.exp(a_last - acum_e[rs])).astype(BF16)
        chunk_decay = jnp.exp(a_last)
        for gi in range(SSD_GROUPS):
            gs = slice(gi * GW, (gi + 1) * GW)
            Bg = bc_b[rs, gi * N:(gi + 1) * N]
            Cg = bc_b[rs, SSD_GROUPS * N + gi * N:SSD_GROUPS * N + (gi + 1) * N]
            cb = _dot_nt(Cg, Bg)
            prev = st_scr[gi]
            y_off = _dot(Cg, prev.astype(BF16)) * ea[rs, gs]
            st_scr[gi] = prev * chunk_decay[:, gs] + _dot(bc[rs, gi * N:(gi + 1) * N].T.astype(BF16), xdte[:, gs])
            yield
            ys = []
            for j in range(SSD_HEADS // SSD_GROUPS):
                hh = gi * (SSD_HEADS // SSD_GROUPS) + j
                hs = slice(hh * P, (hh + 1) * P)
                seg = acum_e[rs, hs] - acum_t[hh:hh + 1, rs]
                wm = jnp.where(causal, cb * jnp.exp(seg), 0.0).astype(BF16)
                ys.append(_dot(wm, xdt_b[rs, hs]))
                if j % 2 == 1:
                    yield
            y_scr[rs, gs] = jnp.concatenate(ys, axis=1) + y_off

    for gi in range(SSD_GROUPS):
        gs = slice(gi * GW, (gi + 1) * GW)
        yg = (y_scr[:, gs] + vecx_ref[2:3, gs] * xs[:, gs]) * _silu(z_ref[:, gs])
        yg = yg * lax.rsqrt(jnp.mean(yg * yg, axis=-1, keepdims=True) + EPS)
        o_ref[:, gs] = (yg * vecx_ref[1:2, gs]).astype(o_ref.dtype)
        yield


N_RWKV_IN, N_SSD_IN = 11, 11
N_RWKV_SCR, N_SSD_SCR = 4, 4
RWKV_TURN, SSD_TURN = 3, 1


def _rwkv_ssd_kernel(*refs):
    n_in = N_RWKV_IN + N_SSD_IN
    rwkv_in, ssd_in = refs[:N_RWKV_IN], refs[N_RWKV_IN:n_in]
    o_rwkv, o_ssd = refs[n_in], refs[n_in + 1]
    scr = refs[n_in + 2:]
    rwkv_scr, ssd_scr = scr[:N_RWKV_SCR], scr[N_RWKV_SCR:]

    @pl.when(pl.program_id(1) == 0)
    def _():
        for s in rwkv_scr[:3] + ssd_scr[:3]:
            s[...] = jnp.zeros_like(s)

    _interleave((_rwkv_body(*rwkv_in, o_rwkv, *rwkv_scr), RWKV_TURN), (_ssd_body(*ssd_in, o_ssd, *ssd_scr), SSD_TURN))


def _interleave(*weighted):
    live = list(weighted)
    while live:
        for item in list(live):
            gen, turn = item
            for _ in range(turn):
                try:
                    next(gen)
                except StopIteration:
                    live.remove(item)
                    break


def _rwkv_ssd_group(p0, rwkv_cols, ssd_cols, batch, seq_len, vec, vec2, w2a2, g2, cwx, cwbc, vecx, vecbc, vecd):
    C, Q = RWKV_CHUNK, SSD_CHUNK
    assert Q == SSD_P
    RB = C * RWKV_BLOCK_CHUNKS
    assert RB == Q * SSD_BLOCK_CHUNKS
    nc = seq_len // RB
    T = batch * seq_len
    cr, ck, cv, czwa, czg = rwkv_cols
    cz, cx, cbc, cdt = ssd_cols
    eseg = jnp.asarray(np.arange(256)[:, None] // RWKV_N == np.arange(256)[None, :] // RWKV_N, BF16)
    same_chunk = np.arange(RB)[:, None] // C == np.arange(RB)[None, :] // C
    ltri = jnp.asarray(np.tril(np.ones((RB, RB))) * same_chunk, BF16)
    csum = ltri
    eh = np.zeros((128, SSD_W), np.float32)
    for hh in range(SSD_HEADS):
        eh[hh, hh * SSD_P:(hh + 1) * SSD_P] = 1.0
    eh = jnp.asarray(eh, BF16)

    def col(width, idx):
        return pl.BlockSpec((RB, width), lambda b, c: (b * nc + c, idx))

    def full(arr):
        return pl.BlockSpec(arr.shape, lambda b, c: (0,) * arr.ndim)

    rwkv_in = [col(RWKV_W, cr), col(RWKV_W, ck), col(RWKV_W, cv), col(128, czwa), col(128, czg),
               full(vec), full(vec2), full(w2a2), full(g2), full(eseg), full(csum)]
    ssd_in = [col(SSD_W, cz), col(SSD_W, cx), col(512, cbc), col(128, cdt),
              full(cwx), full(cwbc), full(vecx), full(vecbc), full(vecd), full(eh), full(ltri)]
    assert len(rwkv_in) == N_RWKV_IN and len(ssd_in) == N_SSD_IN
    rwkv_scr = [pltpu.VMEM((RWKV_HEADS // 2, 2 * RWKV_N, 2 * RWKV_N), F32),
                pltpu.VMEM((8, RWKV_W), F32), pltpu.VMEM((8, 128), F32), pltpu.VMEM((RB, RWKV_W), F32)]
    ssd_scr = [pltpu.VMEM((SSD_GROUPS, SSD_N, SSD_W // SSD_GROUPS), F32),
               pltpu.VMEM((8, SSD_W), F32), pltpu.VMEM((8, 512), F32), pltpu.VMEM((RB, SSD_W), F32)]
    return pl.pallas_call(
        _rwkv_ssd_kernel,
        grid=(batch, nc),
        in_specs=rwkv_in + ssd_in,
        out_specs=[pl.BlockSpec((RB, RWKV_W), lambda b, c: (b * nc + c, 0)),
                   pl.BlockSpec((RB, SSD_W), lambda b, c: (b * nc + c, 0))],
        out_shape=[jax.ShapeDtypeStruct((T, RWKV_W), BF16), jax.ShapeDtypeStruct((T, SSD_W), BF16)],
        scratch_shapes=rwkv_scr + ssd_scr,
        compiler_params=_cparams(("arbitrary", "arbitrary")),
    )(p0, p0, p0, p0, p0, vec, vec2, w2a2, g2, eseg, csum,
      p0, p0, p0, p0, cwx, cwbc, vecx, vecbc, vecd, eh, ltri)


def _s5_kernel(u_ref, blk_ref, bz_ref, cz_ref, lam_ref, d_ref, wglu_ref, bglu_ref, o_ref,
               st_scr, zre_scr, zim_scr, sre_scr, sim_scr, y_scr, toep_ref, fold_scr):
    i = pl.program_id(1)
    Mb = u_ref.shape[0]
    LC, NT = S5_LC, S5_TILES
    HW = S5_W // NT * (S5_P // S5_H)

    @pl.when(jnp.logical_and(pl.program_id(0) == 0, i == 0))
    def _():
        toep_ref[...] = jnp.zeros_like(toep_ref)
        for n in range(NT):
            for li in range(LC):
                for lo in range(li, LC):
                    toep_ref[n, li * 128:(li + 1) * 128, lo * 128:(lo + 1) * 128] = blk_ref[lo - li, n]

    @pl.when(i == 0)
    def _():
        st_scr[...] = jnp.zeros_like(st_scr)

    u = u_ref[...]
    ucat = []
    for n in range(NT):
        un = jnp.concatenate([u[:, l * S5_W + n * 128:l * S5_W + (n + 1) * 128] for l in range(LC)],
                             axis=1).astype(BF16)
        ucat.append(un)
        z = _dot(un, bz_ref[n])
        zre_scr[:, n * HW:(n + 1) * HW] = z[:, :HW]
        zim_scr[:, n * HW:(n + 1) * HW] = z[:, HW:]

    lr = lam_ref[0:1, :]
    lim = lam_ref[1:2, :]

    def step(m, carry):
        sre, sim = carry
        sre_scr[pl.ds(m, 1), :] = sre
        sim_scr[pl.ds(m, 1), :] = sim
        zr = zre_scr[pl.ds(m, 1), :]
        zi = zim_scr[pl.ds(m, 1), :]
        return lr * sre - lim * sim + zr, lr * sim + lim * sre + zi

    sre, sim = lax.fori_loop(0, Mb, step, (st_scr[0:1, :], st_scr[1:2, :]))
    st_scr[0:1, :] = sre
    st_scr[1:2, :] = sim

    for n in range(NT):
        s_re = sre_scr[:, n * HW:(n + 1) * HW].astype(BF16)
        s_im = sim_scr[:, n * HW:(n + 1) * HW].astype(BF16)
        yn = _dot(ucat[n], toep_ref[n]) + _dot(s_re, cz_ref[n, :HW, :]) + _dot(s_im, cz_ref[n, HW:, :])
        for l in range(LC):
            y_scr[:, l * S5_W + n * 128:l * S5_W + (n + 1) * 128] = yn[:, l * 128:(l + 1) * 128]

    y = y_scr[...] + d_ref[...] * u
    y = 0.5 * y * (1.0 + jnp.tanh(math.sqrt(2.0 / math.pi) * (y + 0.044715 * (y * y * y))))
    wglu = wglu_ref[...]
    for l in range(LC):
        yl = y[:, l * S5_W:(l + 1) * S5_W]
        gate = _sigmoid(_dot(yl.astype(BF16), wglu) + bglu_ref[...])
        out_l = yl * gate
        for n in range(NT):
            fold_scr[n, pl.ds(l, Mb, stride=LC), :] = out_l[:, n * 128:(n + 1) * 128]
    for n in range(NT):
        o_ref[:, n * 128:(n + 1) * 128] = fold_scr[n]


def _s5_tables(lam_re, lam_im, log_dt, b_re, b_im, c_re, c_im):
    LC, NT, G, H, Pn = S5_LC, S5_TILES, S5_GROUPS, S5_H, S5_P
    GL = G // NT
    dt = jnp.exp(log_dt)[:, None]
    ar, ai = lam_re * dt, lam_im * dt
    jj = jnp.arange(LC + 1, dtype=F32)[:, None, None]
    mag = jnp.exp(jj * ar[None])
    pw_re, pw_im = mag * jnp.cos(jj * ai[None]), mag * jnp.sin(jj * ai[None])
    nr, ni = pw_re[1] - 1.0, pw_im[1]
    den = lam_re * lam_re + lam_im * lam_im
    qr, qi = (nr * lam_re + ni * lam_im) / den, (ni * lam_re - nr * lam_im) / den
    bb_re = qr[..., None] * b_re - qi[..., None] * b_im
    bb_im = qr[..., None] * b_im + qi[..., None] * b_re
    cl_re = c_re[None] * pw_re[:, :, None, :] - c_im[None] * pw_im[:, :, None, :]
    cl_im = c_re[None] * pw_im[:, :, None, :] + c_im[None] * pw_re[:, :, None, :]
    bt_re, bt_im = jnp.swapaxes(bb_re, 1, 2), jnp.swapaxes(bb_im, 1, 2)
    kd = jnp.sum(cl_re[:LC, :, :, None, :] * bt_re[None, :, None, :, :]
                 - cl_im[:LC, :, :, None, :] * bt_im[None, :, None, :, :], axis=-1)
    def spread(x, rows_per_group, cols_per_group):
        sel = np.tile(np.eye(cols_per_group, dtype=np.float32), (1, GL))
        keep = (np.arange(GL * rows_per_group)[:, None] // rows_per_group
                == np.arange(GL * cols_per_group)[None, :] // cols_per_group)
        return jnp.where(keep, jnp.einsum("...rc,cd->...rd", x, jnp.asarray(sel)), 0.0)

    blk = spread(jnp.swapaxes(kd, -1, -2).reshape(LC, NT, GL * H, H), H, H).astype(BF16)
    def cz_part(cl):
        x = spread(jnp.swapaxes(cl[1:], -1, -2).reshape(LC, NT, GL * Pn, H), Pn, H).astype(BF16)
        return jnp.concatenate([x[l] for l in range(LC)], axis=-1)
    cz = jnp.concatenate([cz_part(cl_re), cz_part(-cl_im)], axis=1)
    jr = (LC - 1.0) - jnp.arange(LC, dtype=F32)[:, None, None]
    mag_r = jnp.exp(jr * ar[None])
    rev_re, rev_im = mag_r * jnp.cos(jr * ai[None]), mag_r * jnp.sin(jr * ai[None])
    bl_re = rev_re[..., None] * bb_re[None] - rev_im[..., None] * bb_im[None]
    bl_im = rev_re[..., None] * bb_im[None] + rev_im[..., None] * bb_re[None]

    def bz_part(bl):
        x = spread(jnp.swapaxes(bl, -1, -2).reshape(LC, NT, GL * H, Pn), H, Pn).astype(BF16)
        return jnp.swapaxes(x, 0, 1).reshape(NT, LC * GL * H, GL * Pn)
    bz = jnp.concatenate([bz_part(bl_re), bz_part(bl_im)], axis=2)
    lam = _rows([pw_re[LC].reshape(-1), pw_im[LC].reshape(-1)], 8)
    return blk, bz, cz, lam


def _s5_group(p1, col_u, batch, seq_len, tables, d_skip, w_glu, b_glu):
    LC = S5_LC
    T = batch * seq_len
    blk, bz, cz, lam = tables
    rows_per_seq = seq_len // LC
    Mb = min(S5_ROWS, rows_per_seq)
    nb = rows_per_seq // Mb
    NPc = p1.shape[1]
    W = LC * S5_W
    d_t = jnp.tile(d_skip.reshape(1, S5_W), (1, LC))
    SW = S5_TILES * (S5_GROUPS // S5_TILES) * S5_P

    def full(arr):
        return pl.BlockSpec(arr.shape, lambda b, i: (0,) * arr.ndim)

    def kern(u_ref, *rest):
        ucat_scr, fold_scr = rest[-2], rest[-1]
        for n in range(S5_TILES):
            fold_scr[n] = u_ref[:, n * 128:(n + 1) * 128]
        for n in range(S5_TILES):
            for l in range(LC):
                ucat_scr[:, l * S5_W + n * 128:l * S5_W + (n + 1) * 128] = fold_scr[n, pl.ds(l, Mb, stride=LC), :]
        _s5_kernel(ucat_scr, *rest[:-2], fold_scr)

    return pl.pallas_call(
        kern,
        grid=(batch, nb),
        in_specs=[pl.BlockSpec((Mb * LC, S5_W), lambda b, i: (b * nb + i, col_u)),
                  _resident(blk.shape), _resident(bz.shape), _resident(cz.shape), full(lam), full(d_t),
                  full(w_glu),
                  pl.BlockSpec((1, S5_W), lambda b, i: (0, 0))],
        out_specs=pl.BlockSpec((Mb * LC, S5_W), lambda b, i: (b * nb + i, 0)),
        out_shape=jax.ShapeDtypeStruct((T, S5_W), F32),
        scratch_shapes=[pltpu.VMEM((8, SW), F32),
                        pltpu.VMEM((Mb, SW), F32), pltpu.VMEM((Mb, SW), F32),
                        pltpu.VMEM((Mb, SW), F32), pltpu.VMEM((Mb, SW), F32),
                        pltpu.VMEM((Mb, W), F32),
                        pltpu.VMEM((S5_TILES, LC * 128, LC * 128), BF16),
                        pltpu.VMEM((Mb, W), F32),
                        pltpu.VMEM((S5_TILES, Mb * LC, 128), F32)],
        compiler_params=_cparams(("arbitrary", "arbitrary")),
    )(p1, blk, bz, cz, lam, d_t, w_glu, b_glu.reshape(1, S5_W))


def _ret_kernel(q_ref, k_ref, v_ref, g_ref, cosl_ref, sinl_ref, base_ref, intra_ref, kdec_ref, qdec_ref,
                o_ref, st_scr, *, chunk_decay):
    c = pl.program_id(1)
    DK, DV = RET_DK, RET_DV

    @pl.when(c == 0)
    def _():
        st_scr[...] = jnp.zeros_like(st_scr)

    Q = intra_ref.shape[1]
    scale = DK ** -0.5
    cb, sb = base_ref[0, 0:1, :], base_ref[0, 1:2, :]
    sign = jnp.where(lax.broadcasted_iota(jnp.int32, (1, DK), 1) < DK // 2, -1.0, 1.0)
    for ch in range(q_ref.shape[0] // Q):
        rs = slice(ch * Q, (ch + 1) * Q)
        cl, sl = cosl_ref[rs, :], sinl_ref[rs, :]
        cosf = cb * cl - sb * sl
        sinf = sign * (sb * cl + cb * sl)
        for h in range(RET_HEADS):
            ks = slice(h * DK, (h + 1) * DK)
            vs = slice(h * DV, (h + 1) * DV)
            qh = q_ref[rs, ks]
            kh = k_ref[rs, ks]
            qh = qh * cosf + pltpu.roll(qh, DK // 2, 1) * sinf
            kh = (kh * cosf + pltpu.roll(kh, DK // 2, 1) * sinf) * scale
            vh = v_ref[rs, vs].astype(BF16)
            scores = (_dot_nt(qh.astype(BF16), kh.astype(BF16)) * intra_ref[h]).astype(BF16)
            prev = st_scr[h]
            y = _dot(scores, vh) + _dot((qh * qdec_ref[:, ks]).astype(BF16), prev.astype(BF16))
            kd = (kh * kdec_ref[:, ks]).T.astype(BF16)
            st_scr[h] = prev * chunk_decay[h] + _dot(kd, vh)
            y = y * lax.rsqrt(jnp.mean(y * y, axis=-1, keepdims=True) + EPS)
            o_ref[rs, vs] = (_silu(g_ref[rs, vs]) * y).astype(o_ref.dtype)


def _ret_group(p1, cols, batch, seq_len):
    Q = min(RET_CHUNK, seq_len)
    RB = min(Q * RET_BLOCK_CHUNKS, seq_len)
    nc = seq_len // RB
    T = batch * seq_len
    cq, ck, cv, cg = cols
    H, DK, DV = RET_HEADS, RET_DK, RET_DV
    half = DK // 2
    theta = 1.0 / (10000.0 ** jnp.linspace(0.0, 1.0, half, dtype=F32))
    theta2 = jnp.concatenate([theta, theta])[None, :]
    ang_l = jnp.arange(RB).astype(F32)[:, None] * theta2
    cos_l, sin_l = jnp.cos(ang_l), jnp.sin(ang_l)
    ang_b = (jnp.arange(nc) * RB).astype(F32)[:, None] * theta2
    base = jnp.concatenate([jnp.cos(ang_b)[:, None, :], jnp.sin(ang_b)[:, None, :],
                            jnp.zeros((nc, 6, DK), F32)], axis=1)
    log_gamma = jnp.log(1.0 - 2.0 ** (-5.0 - jnp.arange(H, dtype=F32)))
    t = jnp.arange(Q, dtype=F32)
    diff = t[:, None] - t[None, :]
    intra = jnp.where((diff >= 0)[None], jnp.exp(jnp.maximum(diff, 0.0)[None] * log_gamma[:, None, None]), 0.0)
    kdec = jnp.repeat(jnp.exp((Q - 1.0 - t)[:, None] * log_gamma[None, :]), DK, axis=1)
    qdec = jnp.repeat(jnp.exp((t + 1.0)[:, None] * log_gamma[None, :]), DK, axis=1)
    chunk_decay = tuple(float((1.0 - 2.0 ** (-5.0 - h)) ** Q) for h in range(H))

    def col(width, idx):
        return pl.BlockSpec((RB, width), lambda b, c: (b * nc + c, idx))

    def full(arr):
        return pl.BlockSpec(arr.shape, lambda b, c: (0,) * arr.ndim)

    return pl.pallas_call(
        functools.partial(_ret_kernel, chunk_decay=chunk_decay),
        grid=(batch, nc),
        in_specs=[col(H * DK, cq), col(H * DK, ck), col(H * DV, cv), col(H * DV, cg),
                  full(cos_l), full(sin_l), pl.BlockSpec((1, 8, DK), lambda b, c: (c, 0, 0)),
                  full(intra), full(kdec), full(qdec)],
        out_specs=pl.BlockSpec((RB, H * DV), lambda b, c: (b * nc + c, 0)),
        out_shape=jax.ShapeDtypeStruct((T, H * DV), BF16),
        scratch_shapes=[pltpu.VMEM((H, DK, DV), F32)],
        compiler_params=_cparams(("arbitrary", "arbitrary")),
    )(p1, p1, p1, p1, cos_l, sin_l, base, intra, kdec, qdec)


def _pad_cols(w, n):
    return jnp.pad(w, ((0, 0), (0, n - w.shape[1])))


def _rows(vectors, n_rows):
    width = vectors[0].shape[0]
    pad = [jnp.zeros((n_rows - len(vectors), width), F32)] if n_rows > len(vectors) else []
    return jnp.concatenate([v.astype(F32).reshape(1, width) for v in vectors] + pad, axis=0)


def kernel(x, l0_norm_mix, l0_w_in, l0_rwkv_mu, l0_rwkv_w0, l0_rwkv_w2, l0_rwkv_a0, l0_rwkv_a2, l0_rwkv_g2, l0_rwkv_k_k, l0_rwkv_k_a, l0_rwkv_r_k, l0_rwkv_ln_g, l0_rwkv_ln_b, l0_ssd_conv_w, l0_ssd_conv_b, l0_ssd_dt_bias, l0_ssd_a_log, l0_ssd_d, l0_ssd_norm_g, l0_w_out, l0_norm_ffn, l0_ffn_up, l0_ffn_conv_w, l0_ffn_conv_b, l0_ffn_down, l1_norm_mix, l1_w_in, l1_s5_lam_re, l1_s5_lam_im, l1_s5_log_dt, l1_s5_b_re, l1_s5_b_im, l1_s5_c_re, l1_s5_c_im, l1_s5_d, l1_s5_w_glu, l1_s5_b_glu, l1_w_out, l1_norm_ffn, l1_ffn_up, l1_ffn_conv_w, l1_ffn_conv_b, l1_ffn_down, final_norm):
    B, L, D = x.shape
    T = B * L
    x2 = x.reshape(T, D).astype(F32)

    RP = 3 * RWKV_W + 256
    o_z = RP
    o_xbc = o_z + SSD_W
    o_dt = o_xbc + SSD_W + 2 * SSD_GROUPS * SSD_N
    n_xbc = o_dt - o_xbc
    w0 = _pad_cols(l0_w_in, o_dt + 128).astype(BF16)
    p0 = _norm_proj(x2, l0_norm_mix, w0, ((o_z, 0, SSD_W), (o_xbc, SSD_W, n_xbc), (0, SSD_W + n_xbc, RP),
                                          (o_dt, SSD_W + n_xbc + RP, 128)))

    mu = l0_rwkv_mu
    vec = _rows([mu[0:512], mu[512:1024], mu[1024:1536], l0_rwkv_w0, l0_rwkv_a0, l0_rwkv_k_k, l0_rwkv_k_a,
                 l0_rwkv_r_k.reshape(-1), l0_rwkv_ln_g, l0_rwkv_ln_b], 16)
    vec2 = _rows([mu[1536:1664], mu[1664:1792]], 8)
    zl = jnp.zeros((64, RWKV_W), F32)
    w2a2 = jnp.concatenate([jnp.concatenate([l0_rwkv_w2.astype(F32), zl], axis=1),
                            jnp.concatenate([zl, l0_rwkv_a2.astype(F32)], axis=1)], axis=0).astype(BF16)

    cw = l0_ssd_conv_w
    cb = l0_ssd_conv_b
    cwx = _rows([cw[k, :SSD_W] for k in range(SSD_CONV)], 8)
    cwbc = _rows([cw[k, SSD_W:] for k in range(SSD_CONV)], 8)
    vecx = _rows([cb[:SSD_W], l0_ssd_norm_g, jnp.repeat(l0_ssd_d, SSD_P)], 8)
    vecbc = _rows([cb[SSD_W:]], 8)
    hpad = jnp.zeros((128 - SSD_HEADS,), F32)
    vecd = _rows([jnp.concatenate([l0_ssd_dt_bias.astype(F32), hpad]),
                  jnp.concatenate([-jnp.exp(l0_ssd_a_log.astype(F32)), hpad])], 8)
    y_a, y_b = _rwkv_ssd_group(p0, (5, 6, 7, 32, 33), (0, 1, 4, 34), B, L, vec, vec2, w2a2,
                               l0_rwkv_g2.astype(BF16), cwx, cwbc, vecx, vecbc, vecd)

    x2 = _mix_ffn(x2, y_a, y_b, l0_w_out.astype(BF16), L, l0_norm_ffn, l0_ffn_up.astype(BF16),
                  l0_ffn_conv_w.astype(F32), l0_ffn_conv_b.astype(F32), l0_ffn_down.astype(BF16),
                  final_norm, False)

    p1 = _norm_proj(x2, l1_norm_mix, l1_w_in.astype(BF16), ((1536, 0, 1024), (2560, 1024, 1024), (0, 2048, 1536)))
    tables = _s5_tables(*(t.astype(F32) for t in (l1_s5_lam_re, l1_s5_lam_im, l1_s5_log_dt, l1_s5_b_re,
                                                   l1_s5_b_im, l1_s5_c_re, l1_s5_c_im)))
    y_c = _s5_group(p1, 4, B, L, tables, l1_s5_d.astype(F32), l1_s5_w_glu.astype(BF16),
                    l1_s5_b_glu.astype(F32))
    y_d = _ret_group(p1, (5, 6, 0, 1), B, L)
    x2 = _mix_ffn(x2, y_c, y_d, l1_w_out.astype(BF16), L, l1_norm_ffn, l1_ffn_up.astype(BF16),
                  l1_ffn_conv_w.astype(F32), l1_ffn_conv_b.astype(F32), l1_ffn_down.astype(BF16),
                  final_norm, True)
    return x2.reshape(B, L, D).astype(x.dtype)
```

```python
import functools
import math

import numpy as np
import jax
import jax.numpy as jnp
from jax import lax
from jax.experimental import pallas as pl
from jax.experimental.pallas import tpu as pltpu

F32 = jnp.float32
BF16 = jnp.bfloat16

EPS = 1e-6
D_MODEL = 1024

RWKV_HEADS = 8
RWKV_N = 64
RWKV_W = 512
RWKV_GN_EPS = 64e-5
RWKV_CHUNK = 64
RWKV_BLOCK_CHUNKS = 4
RWKV_WAVES = 1

SSD_HEADS = 16
SSD_P = 64
SSD_GROUPS = 2
SSD_N = 128
SSD_W = 1024
SSD_CONV = 4
SSD_CHUNK = 64
SSD_BLOCK_CHUNKS = 4

S5_W = 512
S5_GROUPS = 32
S5_H = 16
S5_P = 64
S5_LC = 8
S5_TILES = 4
S5_ROWS = 128

RET_HEADS = 4
RET_DK = 128
RET_DV = 256
RET_CHUNK = 256
RET_BLOCK_CHUNKS = 2

FFN_HIDDEN = 2816
FFN_TN = 256
FFN_HALO = 16

VMEM_LIMIT = 48 * 1024 * 1024


def _dot(a, b):
    return jnp.dot(a, b, preferred_element_type=F32)


def _dot_nt(a, b):
    return lax.dot_general(a, b, (((1,), (1,)), ((), ())), preferred_element_type=F32)


def _dot_tn(a, b):
    return lax.dot_general(a, b, (((0,), (0,)), ((), ())), preferred_element_type=F32)


def _split3(x):
    hi = x.astype(BF16)
    r1 = x - hi.astype(F32)
    mid = r1.astype(BF16)
    lo = (r1 - mid.astype(F32)).astype(BF16)
    return hi, mid, lo


def _dot_sel_rhs(x, sel):
    hi, mid, lo = _split3(x)
    return _dot(hi, sel) + _dot(mid, sel) + _dot(lo, sel)


def _dot_sel_lhs(sel, x):
    hi, mid, lo = _split3(x)
    return _dot(sel, hi) + _dot(sel, mid) + _dot(sel, lo)


def _sigmoid(x):
    return 1.0 / (1.0 + jnp.exp(-x))


def _silu(x):
    return x * _sigmoid(x)


def _softplus(x):
    return jnp.maximum(x, 0.0) + jnp.log(1.0 + jnp.exp(-jnp.abs(x)))


def _cparams(sem):
    return pltpu.CompilerParams(dimension_semantics=sem, vmem_limit_bytes=VMEM_LIMIT)


def _proj_kernel(x_ref, g_ref, w_ref, o_ref, *, pieces):
    x = x_ref[...]
    y = x * lax.rsqrt(jnp.mean(x * x, axis=-1, keepdims=True) + EPS)
    hn = (y * g_ref[...]).astype(BF16)
    for src, dst, width in pieces:
        o_ref[:, dst:dst + width] = _dot(hn, w_ref[:, src:src + width])


def _norm_proj(x2, g, w, pieces, tm=512):
    T, D = x2.shape
    N = sum(p[2] for p in pieces)
    return pl.pallas_call(
        functools.partial(_proj_kernel, pieces=pieces),
        grid=(T // tm,),
        in_specs=[pl.BlockSpec((tm, D), lambda i: (i, 0)), _resident((1, D)), _resident(w.shape)],
        out_specs=pl.BlockSpec((tm, N), lambda i: (i, 0)),
        out_shape=jax.ShapeDtypeStruct((T, N), F32),
        compiler_params=_cparams(("parallel",)),
    )(x2, g.reshape(1, D), w)


def _ffn_kernel(x_ref, xh_ref, ya_ref, yah_ref, yb_ref, ybh_ref, wo_ref, g_ref, wup_ref,
                cw_ref, cb_ref, wd_ref, gf_ref, o_ref, act_scr, *, blocks_per_seq, final_norm):
    i = pl.program_id(0)
    Hd = wd_ref.shape[0]
    tn = FFN_TN
    Ka = ya_ref.shape[1]

    def mixed(x, ya, yb):
        return x + _dot(ya.astype(BF16), wo_ref[0:Ka, :]) + _dot(yb.astype(BF16), wo_ref[Ka:, :])

    def norm(x):
        y = x * lax.rsqrt(jnp.mean(x * x, axis=-1, keepdims=True) + EPS)
        return (y * g_ref[...]).astype(BF16)

    x1e = mixed(jnp.concatenate([xh_ref[...], x_ref[...]], axis=0),
                jnp.concatenate([yah_ref[...], ya_ref[...]], axis=0),
                jnp.concatenate([ybh_ref[...], yb_ref[...]], axis=0))
    x1 = x1e[FFN_HALO:, :]
    keep = jnp.where(i % blocks_per_seq == 0, 0.0, 1.0)
    rowi = lax.broadcasted_iota(jnp.int32, (x1e.shape[0], 1), 0)
    hn = norm(x1e * jnp.where(rowi < FFN_HALO, keep, 1.0))

    def conv(u, cols):
        u1 = pltpu.roll(u, 1, 0)
        u2 = pltpu.roll(u, 2, 0)
        c = cw_ref[2:3, cols] * u + cw_ref[1:2, cols] * u1 + cw_ref[0:1, cols] * u2 + cb_ref[:, cols]
        return c[FFN_HALO:, :]

    for j in range(Hd // tn):
        gs = slice(j * tn, (j + 1) * tn)
        vs = slice(Hd + j * tn, Hd + (j + 1) * tn)
        cg = conv(_dot(hn, wup_ref[:, gs]), gs)
        cv = conv(_dot(hn, wup_ref[:, vs]), vs)
        act_scr[:, gs] = (_silu(cg) * cv).astype(BF16)

    y = x1 + _dot(act_scr[...], wd_ref[...])
    if final_norm:
        y = y * lax.rsqrt(jnp.mean(y * y, axis=-1, keepdims=True) + EPS) * gf_ref[...]
    o_ref[...] = y


def _resident(shape):
    return pl.BlockSpec(shape, lambda *_: (0,) * len(shape), pipeline_mode=pl.Buffered(1))


def _mix_ffn(x2, ya, yb, wo, seq_len, g, w_up, conv_w, conv_b, w_down, g_final, final_norm, tm=512):
    T, D = x2.shape
    Hd = w_down.shape[0]
    Ka, Kb = ya.shape[1], yb.shape[1]
    tm = min(tm, seq_len)
    hb = tm // FFN_HALO
    kern = functools.partial(_ffn_kernel, blocks_per_seq=seq_len // tm, final_norm=final_norm)

    def rows(width):
        return pl.BlockSpec((tm, width), lambda i: (i, 0))

    def halo(width):
        return pl.BlockSpec((FFN_HALO, width), lambda i: (jnp.maximum(i * hb - 1, 0), 0))

    return pl.pallas_call(
        kern,
        grid=(T // tm,),
        in_specs=[rows(D), halo(D), rows(Ka), halo(Ka), rows(Kb), halo(Kb),
                  _resident((Ka + Kb, D)),
                  _resident((1, D)), _resident((D, 2 * Hd)), _resident((3, 2 * Hd)), _resident((1, 2 * Hd)),
                  _resident((Hd, D)), _resident((1, D))],
        out_specs=pl.BlockSpec((tm, D), lambda i: (i, 0)),
        out_shape=jax.ShapeDtypeStruct((T, D), F32),
        scratch_shapes=[pltpu.VMEM((tm, Hd), BF16)],
        compiler_params=_cparams(("parallel",)),
    )(x2, x2, ya, ya, yb, yb, wo, g.reshape(1, D), w_up, conv_w, conv_b.reshape(1, -1), w_down,
      g_final.reshape(1, D))


def _rwkv_body(r_ref, k_ref, v_ref, zwa_ref, zg_ref, vec_ref, vec2_ref, w2a2_ref, g2_ref,
               eseg_ref, csum_ref, o_ref, h_scr, prev_scr, prev2_scr, y_scr):
    RB = r_ref.shape[0]
    C = RWKV_CHUNK
    N = RWKV_N

    row = lax.broadcasted_iota(jnp.int32, (RB, 1), 0)

    def shift_lerp(x, prev_row, mu):
        xs = jnp.where(row == 0, prev_row, pltpu.roll(x, 1, 0))
        return x + (xs - x) * mu

    r_raw, k_raw, v_raw = r_ref[...], k_ref[...], v_ref[...]
    zwa_raw, zg_raw = zwa_ref[...], zg_ref[...]
    r = shift_lerp(r_raw, prev_scr[0:1, :], vec_ref[0:1, :])
    k = shift_lerp(k_raw, prev_scr[1:2, :], vec_ref[1:2, :])
    v = shift_lerp(v_raw, prev_scr[2:3, :], vec_ref[2:3, :])
    zwa = shift_lerp(zwa_raw, prev2_scr[0:1, :], vec2_ref[0:1, :])
    zg = shift_lerp(zg_raw, prev2_scr[1:2, :], vec2_ref[1:2, :])
    prev_scr[0:1, :] = r_raw[RB - 1:RB, :]
    prev_scr[1:2, :] = k_raw[RB - 1:RB, :]
    prev_scr[2:3, :] = v_raw[RB - 1:RB, :]
    prev2_scr[0:1, :] = zwa_raw[RB - 1:RB, :]
    prev2_scr[1:2, :] = zg_raw[RB - 1:RB, :]

    w0, a0 = vec_ref[3:4, :], vec_ref[4:5, :]
    k_k, k_a, r_k = vec_ref[5:6, :], vec_ref[6:7, :], vec_ref[7:8, :]
    ln_g, ln_b = vec_ref[8:9, :], vec_ref[9:10, :]

    lane = lax.broadcasted_iota(jnp.int32, (1, 128), 1)
    lora_in = jnp.where(lane < 64, jnp.tanh(zwa), zwa).astype(BF16)
    wa = _dot(lora_in, w2a2_ref[...])
    w_log = -_softplus(-(w0 + wa[:, :RWKV_W])) - 0.5
    logw = -jnp.exp(w_log)
    a = _sigmoid(a0 + wa[:, RWKV_W:])
    g = _dot(_sigmoid(zg).astype(BF16), g2_ref[...])

    eseg = eseg_ref[...]

    def segsum(x, split=True):
        EW = eseg.shape[0]
        hi = x.astype(BF16)
        lo = (x - hi.astype(F32)).astype(BF16) if split else None
        parts = []
        for q in range(RWKV_W // EW):
            qs = slice(q * EW, (q + 1) * EW)
            sq = _dot(hi[:, qs], eseg)
            parts.append(sq + _dot(lo[:, qs], eseg) if split else sq)
        return jnp.concatenate(parts, axis=1)

    yield
    kk = k * k_k
    kk = kk * lax.rsqrt(jnp.maximum(segsum(kk * kk), 1e-24))
    k2 = k * (1.0 + (a - 1.0) * k_a)
    yield

    lg = _dot_sel_lhs(csum_ref[...], logw)
    lg_end = jnp.concatenate([jnp.broadcast_to(lg[(j + 1) * C - 1:(j + 1) * C, :], (C, RWKV_W))
                              for j in range(RB // C)], axis=0)
    eg = jnp.exp(lg)
    egi = jnp.exp(-lg)
    g_end = jnp.exp(lg_end)
    d_end = jnp.exp(lg_end - lg)
    yield
    kka = kk * a
    ab = -kk * jnp.exp(lg - logw)
    rb = r * eg
    bt = kka * egi
    kt = k2 * egi
    bg = kka * d_end
    kg = k2 * d_end
    yield

    W2 = 2 * N
    left = lax.broadcasted_iota(jnp.int32, (1, W2), 1) < N

    def bd(x):
        return jnp.concatenate([jnp.where(left, x, 0.0), jnp.where(left, 0.0, x)], axis=0)

    ri = lax.broadcasted_iota(jnp.int32, (2 * W2, 2 * W2), 0)
    ci = lax.broadcasted_iota(jnp.int32, (2 * W2, 2 * W2), 1)
    mask = (ci & (N - 1)) < (ri & (N - 1)) + jnp.where(ri >= W2, 1, 0)
    e_r = lax.broadcasted_iota(jnp.int32, (W2, W2), 0)
    e_c = lax.broadcasted_iota(jnp.int32, (W2, W2), 1)
    eye = jnp.where(e_r == e_c, 1.0, 0.0)
    zeros_b = jnp.zeros((W2, W2), BF16)

    all_units = [(j, p) for j in range(RB // C) for p in range(RWKV_HEADS // 2)]

    def piece(x, j, p):
        return x[j * C:(j + 1) * C, p * W2:(p + 1) * W2]

    def each(fn, n):
        out = []
        for i in range(n):
            out.append(fn(i))
            yield
        return out

    def diag2(ya, yb):
        z = jnp.zeros_like(ya)
        return jnp.concatenate([jnp.concatenate([ya, z], axis=1), jnp.concatenate([z, yb], axis=1)], axis=0)

    def diag2_of(xx):
        return diag2(xx[:, :W2], xx[:, W2:])

    def wave(units):
        U = len(units)
        ab_u = yield from each(lambda i: bd(piece(ab, *units[i])).astype(BF16), U)
        AR = yield from each(lambda i: jnp.concatenate([ab_u[i], bd(piece(rb, *units[i])).astype(BF16)], axis=0), U)
        BK = yield from each(lambda i: jnp.concatenate([bd(piece(bt, *units[i])), bd(piece(kt, *units[i]))],
                                                       axis=0).astype(BF16), U)
        v_u = yield from each(lambda i: bd(piece(v, *units[i])).astype(BF16), U)
        G = yield from each(lambda i: jnp.where(mask, _dot_nt(AR[i], BK[i]), 0.0), U)
        SU = U // 2
        AA = [jnp.concatenate([G[2 * s][:W2, :W2], G[2 * s + 1][:W2, :W2]], axis=1) for s in range(SU)]
        SS = [jnp.concatenate([eye, eye], axis=1) + AA[s] for s in range(SU)]
        AAb = [AA[s].astype(BF16) for s in range(SU)]
        QQ = yield from each(lambda s: _dot(AAb[s], diag2_of(AAb[s])), SU)
        for _ in range(int(math.log2(C)) - 2):
            QQb = [QQ[s].astype(BF16) for s in range(SU)]
            QS = yield from each(lambda s: _dot(jnp.concatenate([QQb[s], SS[s].astype(BF16)], axis=0),
                                                diag2_of(QQb[s])), SU)
            QQ = [QS[s][:W2] for s in range(SU)]
            SS = [SS[s] + QS[s][W2:] for s in range(SU)]
        SS = yield from each(
            lambda s: (SS[s] + _dot(SS[s].astype(BF16), diag2_of(QQ[s].astype(BF16)))).astype(BF16), SU)
        S = [SS[i // 2][:, (i % 2) * W2:(i % 2 + 1) * W2] for i in range(U)]
        akv2 = yield from each(
            lambda s: _dot(jnp.concatenate([G[2 * s][:W2, W2:], G[2 * s + 1][:W2, W2:]], axis=1).astype(BF16),
                           diag2(v_u[2 * s], v_u[2 * s + 1])), SU)
        akv = [akv2[i // 2][:, (i % 2) * W2:(i % 2 + 1) * W2] for i in range(U)]
        WU = yield from each(
            lambda i: _dot(S[i], jnp.concatenate([ab_u[i], akv[i].astype(BF16)], axis=1)).astype(BF16), U)
        Z = [jnp.concatenate([WU[i], jnp.concatenate([zeros_b, v_u[i]], axis=1)], axis=0) for i in range(U)]
        QY = yield from each(lambda i: _dot(G[i][W2:, :].astype(BF16), Z[i]), U)
        BKgT = yield from each(lambda i: jnp.concatenate([bd(piece(bg, *units[i])).T, bd(piece(kg, *units[i])).T],
                                                         axis=1).astype(BF16), U)
        MN = yield from each(lambda i: _dot(BKgT[i], Z[i]), U)
        QM = [jnp.concatenate([AR[i][W2:].astype(F32) + QY[i][:, :W2], MN[i][:, :W2]], axis=0).astype(BF16)
              for i in range(U)]
        gcol = [jnp.sum(eye * piece(g_end, j, p)[0:1, :], axis=1, keepdims=True) for j, p in units]

        for s in range(SU):
            ia, ib = 2 * s, 2 * s + 1
            Ha, Hb = h_scr[units[ia][1]], h_scr[units[ib][1]]
            YH2 = _dot(jnp.concatenate([QM[ia], QM[ib]], axis=1), diag2(Ha.astype(BF16), Hb.astype(BF16)))
            for i, H, YH in ((ia, Ha, YH2[:, :W2]), (ib, Hb, YH2[:, W2:])):
                j, p = units[i]
                ybd = YH[:W2] + QY[i][:, W2:]
                y_scr[j * C:(j + 1) * C, p * W2:(p + 1) * W2] = ybd[:C] + ybd[C:]
                h_scr[p] = H * gcol[i] + YH[W2:] + MN[i][:, W2:]
            yield

    per_wave = len(all_units) // RWKV_WAVES
    for w in range(RWKV_WAVES):
        yield from wave(all_units[w * per_wave:(w + 1) * per_wave])

    y = y_scr[...]
    inv_n = 1.0 / N
    mean = segsum(y) * inv_n
    d = y - mean
    yield
    var = segsum(d * d, split=False) * inv_n
    yn = d * lax.rsqrt(var + RWKV_GN_EPS) * ln_g + ln_b
    yield
    bonus = segsum(r * k2 * r_k, split=False) * v
    o_ref[...] = ((yn + bonus) * g).astype(o_ref.dtype)


def _ssd_body(z_ref, xs_ref, bc_ref, dt_ref, cwx_ref, cwbc_ref, vecx_ref, vecbc_ref, vecd_ref,
              eh_ref, ltri_ref, o_ref, st_scr, tailx_scr, tailbc_scr, y_scr):
    RB = xs_ref.shape[0]
    Q = SSD_CHUNK
    P, N = SSD_P, SSD_N
    GW = SSD_W // SSD_GROUPS

    def conv_silu(x_ref, tail_scr, cw_ref, b_ref):
        parts = []
        for q in range(x_ref.shape[1] // 256):
            cs = slice(q * 256, (q + 1) * 256)
            x_raw = x_ref[:, cs]
            ext = jnp.concatenate([tail_scr[:, cs], x_raw], axis=0)
            acc = b_ref[0:1, cs] + cw_ref[SSD_CONV - 1:SSD_CONV, cs] * x_raw
            for s in range(1, SSD_CONV):
                acc = acc + cw_ref[SSD_CONV - 1 - s:SSD_CONV - s, cs] * pltpu.roll(ext, s, 0)[8:, :]
            tail_scr[:, cs] = x_raw[RB - 8:, :]
            parts.append(_silu(acc))
            yield
        return jnp.concatenate(parts, axis=1)

    xs = yield from conv_silu(xs_ref, tailx_scr, cwx_ref, vecx_ref)
    bc = yield from conv_silu(bc_ref, tailbc_scr, cwbc_ref, vecbc_ref)

    eh = eh_ref[...]
    dt = _softplus(dt_ref[...] + vecd_ref[0:1, :])
    a = dt * vecd_ref[1:2, :]
    acum = _dot_sel_lhs(ltri_ref[...], a)
    dt_e = _dot_sel_rhs(dt, eh)
    acum_e = _dot_sel_rhs(acum, eh)
    acum_t = acum.T
    yield
    xdt = xs * dt_e
    xdt_b = xdt.astype(BF16)
    ea = jnp.exp(acum_e)
    bc_b = bc.astype(BF16)
    yield

    li = lax.broadcasted_iota(jnp.int32, (Q, Q), 0)
    si = lax.broadcasted_iota(jnp.int32, (Q, Q), 1)
    causal = si <= li

    for ch in range(RB // Q):
        rs = slice(ch * Q, (ch + 1) * Q)
        a_last = acum_e[(ch + 1) * Q - 1:(ch + 1) * Q, :]
        xdte = (xdt[rs] * jnp.exp(a_last - acum_e[rs])).astype(BF16)
        chunk_decay = jnp.exp(a_last)
        for gi in range(SSD_GROUPS):
            gs = slice(gi * GW, (gi + 1) * GW)
            Bg = bc_b[rs, gi * N:(gi + 1) * N]
            Cg = bc_b[rs, SSD_GROUPS * N + gi * N:SSD_GROUPS * N + (gi + 1) * N]
            cb = _dot_nt(Cg, Bg)
            prev = st_scr[gi]
            y_off = _dot(Cg, prev.astype(BF16)) * ea[rs, gs]
            st_scr[gi] = prev * chunk_decay[:, gs] + _dot(bc[rs, gi * N:(gi + 1) * N].T.astype(BF16), xdte[:, gs])
            yield
            ys = []
            for j in range(SSD_HEADS // SSD_GROUPS):
                hh = gi * (SSD_HEADS // SSD_GROUPS) + j
                hs = slice(hh * P, (hh + 1) * P)
                seg = acum_e[rs, hs] - acum_t[hh:hh + 1, rs]
                wm = jnp.where(causal, cb * jnp.exp(seg), 0.0).astype(BF16)
                ys.append(_dot(wm, xdt_b[rs, hs]))
                if j % 2 == 1:
                    yield
            y_scr[rs, gs] = jnp.concatenate(ys, axis=1) + y_off

    for gi in range(SSD_GROUPS):
        gs = slice(gi * GW, (gi + 1) * GW)
        yg = (y_scr[:, gs] + vecx_ref[2:3, gs] * xs[:, gs]) * _silu(z_ref[:, gs])
        yg = yg * lax.rsqrt(jnp.mean(yg * yg, axis=-1, keepdims=True) + EPS)
        o_ref[:, gs] = (yg * vecx_ref[1:2, gs]).astype(o_ref.dtype)
        yield


N_RWKV_IN, N_SSD_IN = 11, 11
N_RWKV_SCR, N_SSD_SCR = 4, 4
RWKV_TURN, SSD_TURN = 3, 1


def _rwkv_ssd_kernel(*refs):
    n_in = N_RWKV_IN + N_SSD_IN
    rwkv_in, ssd_in = refs[:N_RWKV_IN], refs[N_RWKV_IN:n_in]
    o_rwkv, o_ssd = refs[n_in], refs[n_in + 1]
    scr = refs[n_in + 2:]
    rwkv_scr, ssd_scr = scr[:N_RWKV_SCR], scr[N_RWKV_SCR:]

    @pl.when(pl.program_id(1) == 0)
    def _():
        for s in rwkv_scr[:3] + ssd_scr[:3]:
            s[...] = jnp.zeros_like(s)

    _interleave((_rwkv_body(*rwkv_in, o_rwkv, *rwkv_scr), RWKV_TURN), (_ssd_body(*ssd_in, o_ssd, *ssd_scr), SSD_TURN))


def _interleave(*weighted):
    live = list(weighted)
    while live:
        for item in list(live):
            gen, turn = item
            for _ in range(turn):
                try:
                    next(gen)
                except StopIteration:
                    live.remove(item)
                    break


def _rwkv_ssd_group(p0, rwkv_cols, ssd_cols, batch, seq_len, vec, vec2, w2a2, g2, cwx, cwbc, vecx, vecbc, vecd):
    C, Q = RWKV_CHUNK, SSD_CHUNK
    assert Q == SSD_P
    RB = C * RWKV_BLOCK_CHUNKS
    assert RB == Q * SSD_BLOCK_CHUNKS
    nc = seq_len // RB
    T = batch * seq_len
    cr, ck, cv, czwa, czg = rwkv_cols
    cz, cx, cbc, cdt = ssd_cols
    eseg = jnp.asarray(np.arange(256)[:, None] // RWKV_N == np.arange(256)[None, :] // RWKV_N, BF16)
    same_chunk = np.arange(RB)[:, None] // C == np.arange(RB)[None, :] // C
    ltri = jnp.asarray(np.tril(np.ones((RB, RB))) * same_chunk, BF16)
    csum = ltri
    eh = np.zeros((128, SSD_W), np.float32)
    for hh in range(SSD_HEADS):
        eh[hh, hh * SSD_P:(hh + 1) * SSD_P] = 1.0
    eh = jnp.asarray(eh, BF16)

    def col(width, idx):
        return pl.BlockSpec((RB, width), lambda b, c: (b * nc + c, idx))

    def full(arr):
        return pl.BlockSpec(arr.shape, lambda b, c: (0,) * arr.ndim)

    rwkv_in = [col(RWKV_W, cr), col(RWKV_W, ck), col(RWKV_W, cv), col(128, czwa), col(128, czg),
               full(vec), full(vec2), full(w2a2), full(g2), full(eseg), full(csum)]
    ssd_in = [col(SSD_W, cz), col(SSD_W, cx), col(512, cbc), col(128, cdt),
              full(cwx), full(cwbc), full(vecx), full(vecbc), full(vecd), full(eh), full(ltri)]
    assert len(rwkv_in) == N_RWKV_IN and len(ssd_in) == N_SSD_IN
    rwkv_scr = [pltpu.VMEM((RWKV_HEADS // 2, 2 * RWKV_N, 2 * RWKV_N), F32),
                pltpu.VMEM((8, RWKV_W), F32), pltpu.VMEM((8, 128), F32), pltpu.VMEM((RB, RWKV_W), F32)]
    ssd_scr = [pltpu.VMEM((SSD_GROUPS, SSD_N, SSD_W // SSD_GROUPS), F32),
               pltpu.VMEM((8, SSD_W), F32), pltpu.VMEM((8, 512), F32), pltpu.VMEM((RB, SSD_W), F32)]
    return pl.pallas_call(
        _rwkv_ssd_kernel,
        grid=(batch, nc),
        in_specs=rwkv_in + ssd_in,
        out_specs=[pl.BlockSpec((RB, RWKV_W), lambda b, c: (b * nc + c, 0)),
                   pl.BlockSpec((RB, SSD_W), lambda b, c: (b * nc + c, 0))],
        out_shape=[jax.ShapeDtypeStruct((T, RWKV_W), BF16), jax.ShapeDtypeStruct((T, SSD_W), BF16)],
        scratch_shapes=rwkv_scr + ssd_scr,
        compiler_params=_cparams(("arbitrary", "arbitrary")),
    )(p0, p0, p0, p0, p0, vec, vec2, w2a2, g2, eseg, csum,
      p0, p0, p0, p0, cwx, cwbc, vecx, vecbc, vecd, eh, ltri)


def _s5_kernel(u_ref, blk_ref, bzr_ref, bzi_ref, czr_ref, czi_ref, lam_ref, d_ref, wglu_ref, bglu_ref, o_ref,
               st_scr, zre_scr, zim_scr, sre_scr, sim_scr, y_scr, toep_ref, fold_scr):
    i = pl.program_id(1)
    Mb = u_ref.shape[0]
    LC, NT = S5_LC, S5_TILES
    HW = S5_W // NT * (S5_P // S5_H)

    @pl.when(jnp.logical_and(pl.program_id(0) == 0, i == 0))
    def _():
        toep_ref[...] = jnp.zeros_like(toep_ref)
        for n in range(NT):
            for li in range(LC):
                for lo in range(li, LC):
                    toep_ref[n, li * 128:(li + 1) * 128, lo * 128:(lo + 1) * 128] = blk_ref[lo - li, n]

    @pl.when(i == 0)
    def _():
        st_scr[...] = jnp.zeros_like(st_scr)

    u = u_ref[...]
    ucat = []
    for n in range(NT):
        un = jnp.concatenate([u[:, l * S5_W + n * 128:l * S5_W + (n + 1) * 128] for l in range(LC)],
                             axis=1).astype(BF16)
        ucat.append(un)
        zre_scr[:, n * HW:(n + 1) * HW] = _dot(un, bzr_ref[n])
        zim_scr[:, n * HW:(n + 1) * HW] = _dot(un, bzi_ref[n])

    lr = lam_ref[0:1, :]
    lim = lam_ref[1:2, :]

    def step(m, carry):
        sre, sim = carry
        sre_scr[pl.ds(m, 1), :] = sre
        sim_scr[pl.ds(m, 1), :] = sim
        zr = zre_scr[pl.ds(m, 1), :]
        zi = zim_scr[pl.ds(m, 1), :]
        return lr * sre - lim * sim + zr, lr * sim + lim * sre + zi

    sre, sim = lax.fori_loop(0, Mb, step, (st_scr[0:1, :], st_scr[1:2, :]))
    st_scr[0:1, :] = sre
    st_scr[1:2, :] = sim

    for n in range(NT):
        s_re = sre_scr[:, n * HW:(n + 1) * HW].astype(BF16)
        s_im = sim_scr[:, n * HW:(n + 1) * HW].astype(BF16)
        yn = _dot(ucat[n], toep_ref[n]) + _dot(s_re, czr_ref[n]) + _dot(s_im, czi_ref[n])
        for l in range(LC):
            y_scr[:, l * S5_W + n * 128:l * S5_W + (n + 1) * 128] = yn[:, l * 128:(l + 1) * 128]

    y = y_scr[...] + d_ref[...] * u
    y = 0.5 * y * (1.0 + jnp.tanh(math.sqrt(2.0 / math.pi) * (y + 0.044715 * (y * y * y))))
    wglu = wglu_ref[...]
    for l in range(LC):
        yl = y[:, l * S5_W:(l + 1) * S5_W]
        gate = _sigmoid(_dot(yl.astype(BF16), wglu) + bglu_ref[...])
        out_l = yl * gate
        for n in range(NT):
            fold_scr[n, pl.ds(l, Mb, stride=LC), :] = out_l[:, n * 128:(n + 1) * 128]
    for n in range(NT):
        o_ref[:, n * 128:(n + 1) * 128] = fold_scr[n]


def _s5_tables(lam_re, lam_im, log_dt, b_re, b_im, c_re, c_im):
    LC, NT, G, H, Pn = S5_LC, S5_TILES, S5_GROUPS, S5_H, S5_P
    GL = G // NT
    dt = jnp.exp(log_dt)[:, None]
    ar, ai = lam_re * dt, lam_im * dt
    jj = jnp.arange(LC + 1, dtype=F32)[:, None, None]
    mag = jnp.exp(jj * ar[None])
    pw_re, pw_im = mag * jnp.cos(jj * ai[None]), mag * jnp.sin(jj * ai[None])
    nr, ni = pw_re[1] - 1.0, pw_im[1]
    den = lam_re * lam_re + lam_im * lam_im
    qr, qi = (nr * lam_re + ni * lam_im) / den, (ni * lam_re - nr * lam_im) / den
    bb_re = qr[..., None] * b_re - qi[..., None] * b_im
    bb_im = qr[..., None] * b_im + qi[..., None] * b_re
    cl_re = c_re[None] * pw_re[:, :, None, :] - c_im[None] * pw_im[:, :, None, :]
    cl_im = c_re[None] * pw_im[:, :, None, :] + c_im[None] * pw_re[:, :, None, :]
    bt_re, bt_im = jnp.swapaxes(bb_re, 1, 2), jnp.swapaxes(bb_im, 1, 2)
    kd = jnp.sum(cl_re[:LC, :, :, None, :] * bt_re[None, :, None, :, :]
                 - cl_im[:LC, :, :, None, :] * bt_im[None, :, None, :, :], axis=-1)
    def spread(x, rows_per_group, cols_per_group):
        sel = np.tile(np.eye(cols_per_group, dtype=np.float32), (1, GL))
        keep = (np.arange(GL * rows_per_group)[:, None] // rows_per_group
                == np.arange(GL * cols_per_group)[None, :] // cols_per_group)
        return jnp.where(keep, jnp.einsum("...rc,cd->...rd", x, jnp.asarray(sel)), 0.0)

    blk = spread(jnp.swapaxes(kd, -1, -2).reshape(LC, NT, GL * H, H), H, H).astype(BF16)
    def cz_part(cl):
        x = spread(jnp.swapaxes(cl[1:], -1, -2).reshape(LC, NT, GL * Pn, H), Pn, H).astype(BF16)
        return jnp.concatenate([x[l] for l in range(LC)], axis=-1)
    cz = (cz_part(cl_re), cz_part(-cl_im))
    jr = (LC - 1.0) - jnp.arange(LC, dtype=F32)[:, None, None]
    mag_r = jnp.exp(jr * ar[None])
    rev_re, rev_im = mag_r * jnp.cos(jr * ai[None]), mag_r * jnp.sin(jr * ai[None])
    bl_re = rev_re[..., None] * bb_re[None] - rev_im[..., None] * bb_im[None]
    bl_im = rev_re[..., None] * bb_im[None] + rev_im[..., None] * bb_re[None]

    def bz_part(bl):
        x = spread(jnp.swapaxes(bl, -1, -2).reshape(LC, NT, GL * H, Pn), H, Pn).astype(BF16)
        return jnp.swapaxes(x, 0, 1).reshape(NT, LC * GL * H, GL * Pn)
    bz = (bz_part(bl_re), bz_part(bl_im))
    lam = _rows([pw_re[LC].reshape(-1), pw_im[LC].reshape(-1)], 8)
    return blk, bz, cz, lam


def _s5_group(p1, col_u, batch, seq_len, tables, d_skip, w_glu, b_glu):
    LC = S5_LC
    T = batch * seq_len
    blk, bz, cz, lam = tables
    rows_per_seq = seq_len // LC
    Mb = min(S5_ROWS, rows_per_seq)
    nb = rows_per_seq // Mb
    NPc = p1.shape[1]
    W = LC * S5_W
    d_t = jnp.tile(d_skip.reshape(1, S5_W), (1, LC))
    SW = S5_TILES * (S5_GROUPS // S5_TILES) * S5_P

    def full(arr):
        return pl.BlockSpec(arr.shape, lambda b, i: (0,) * arr.ndim)

    def kern(u_ref, *rest):
        ucat_scr, fold_scr = rest[-2], rest[-1]
        for n in range(S5_TILES):
            fold_scr[n] = u_ref[:, n * 128:(n + 1) * 128]
        for n in range(S5_TILES):
            for l in range(LC):
                ucat_scr[:, l * S5_W + n * 128:l * S5_W + (n + 1) * 128] = fold_scr[n, pl.ds(l, Mb, stride=LC), :]
        _s5_kernel(ucat_scr, *rest[:-2], fold_scr)

    return pl.pallas_call(
        kern,
        grid=(batch, nb),
        in_specs=[pl.BlockSpec((Mb * LC, S5_W), lambda b, i: (b * nb + i, col_u)),
                  _resident(blk.shape), _resident(bz[0].shape), _resident(bz[1].shape),
                  _resident(cz[0].shape), _resident(cz[1].shape), full(lam), full(d_t),
                  full(w_glu),
                  pl.BlockSpec((1, S5_W), lambda b, i: (0, 0))],
        out_specs=pl.BlockSpec((Mb * LC, S5_W), lambda b, i: (b * nb + i, 0)),
        out_shape=jax.ShapeDtypeStruct((T, S5_W), F32),
        scratch_shapes=[pltpu.VMEM((8, SW), F32),
                        pltpu.VMEM((Mb, SW), F32), pltpu.VMEM((Mb, SW), F32),
                        pltpu.VMEM((Mb, SW), F32), pltpu.VMEM((Mb, SW), F32),
                        pltpu.VMEM((Mb, W), F32),
                        pltpu.VMEM((S5_TILES, LC * 128, LC * 128), BF16),
                        pltpu.VMEM((Mb, W), F32),
                        pltpu.VMEM((S5_TILES, Mb * LC, 128), F32)],
        compiler_params=_cparams(("arbitrary", "arbitrary")),
    )(p1, blk, *bz, *cz, lam, d_t, w_glu, b_glu.reshape(1, S5_W))


def _ret_kernel(q_ref, k_ref, v_ref, g_ref, cosl_ref, sinl_ref, base_ref, intra_ref, kdec_ref, qdec_ref,
                o_ref, st_scr, *, chunk_decay):
    c = pl.program_id(1)
    DK, DV = RET_DK, RET_DV

    @pl.when(c == 0)
    def _():
        st_scr[...] = jnp.zeros_like(st_scr)

    Q = intra_ref.shape[1]
    scale = DK ** -0.5
    cb, sb = base_ref[0, 0:1, :], base_ref[0, 1:2, :]
    sign = jnp.where(lax.broadcasted_iota(jnp.int32, (1, DK), 1) < DK // 2, -1.0, 1.0)
    for ch in range(q_ref.shape[0] // Q):
        rs = slice(ch * Q, (ch + 1) * Q)
        cl, sl = cosl_ref[rs, :], sinl_ref[rs, :]
        cosf = cb * cl - sb * sl
        sinf = sign * (sb * cl + cb * sl)
        for h in range(RET_HEADS):
            ks = slice(h * DK, (h + 1) * DK)
            vs = slice(h * DV, (h + 1) * DV)
            qh = q_ref[rs, ks]
            kh = k_ref[rs, ks]
            qh = qh * cosf + pltpu.roll(qh, DK // 2, 1) * sinf
            kh = (kh * cosf + pltpu.roll(kh, DK // 2, 1) * sinf) * scale
            vh = v_ref[rs, vs].astype(BF16)
            scores = (_dot_nt(qh.astype(BF16), kh.astype(BF16)) * intra_ref[h]).astype(BF16)
            prev = st_scr[h]
            y = _dot(scores, vh) + _dot((qh * qdec_ref[:, ks]).astype(BF16), prev.astype(BF16))
            kd = (kh * kdec_ref[:, ks]).T.astype(BF16)
            st_scr[h] = prev * chunk_decay[h] + _dot(kd, vh)
            y = y * lax.rsqrt(jnp.mean(y * y, axis=-1, keepdims=True) + EPS)
            o_ref[rs, vs] = (_silu(g_ref[rs, vs]) * y).astype(o_ref.dtype)


def _ret_group(p1, cols, batch, seq_len):
    Q = min(RET_CHUNK, seq_len)
    RB = min(Q * RET_BLOCK_CHUNKS, seq_len)
    nc = seq_len // RB
    T = batch * seq_len
    cq, ck, cv, cg = cols
    H, DK, DV = RET_HEADS, RET_DK, RET_DV
    half = DK // 2
    theta = 1.0 / (10000.0 ** jnp.linspace(0.0, 1.0, half, dtype=F32))
    theta2 = jnp.concatenate([theta, theta])[None, :]
    ang_l = jnp.arange(RB).astype(F32)[:, None] * theta2
    cos_l, sin_l = jnp.cos(ang_l), jnp.sin(ang_l)
    ang_b = (jnp.arange(nc) * RB).astype(F32)[:, None] * theta2
    base = jnp.concatenate([jnp.cos(ang_b)[:, None, :], jnp.sin(ang_b)[:, None, :],
                            jnp.zeros((nc, 6, DK), F32)], axis=1)
    log_gamma = jnp.log(1.0 - 2.0 ** (-5.0 - jnp.arange(H, dtype=F32)))
    t = jnp.arange(Q, dtype=F32)
    diff = t[:, None] - t[None, :]
    intra = jnp.where((diff >= 0)[None], jnp.exp(jnp.maximum(diff, 0.0)[None] * log_gamma[:, None, None]), 0.0)
    kdec = jnp.repeat(jnp.exp((Q - 1.0 - t)[:, None] * log_gamma[None, :]), DK, axis=1)
    qdec = jnp.repeat(jnp.exp((t + 1.0)[:, None] * log_gamma[None, :]), DK, axis=1)
    chunk_decay = tuple(float((1.0 - 2.0 ** (-5.0 - h)) ** Q) for h in range(H))

    def col(width, idx):
        return pl.BlockSpec((RB, width), lambda b, c: (b * nc + c, idx))

    def full(arr):
        return pl.BlockSpec(arr.shape, lambda b, c: (0,) * arr.ndim)

    return pl.pallas_call(
        functools.partial(_ret_kernel, chunk_decay=chunk_decay),
        grid=(batch, nc),
        in_specs=[col(H * DK, cq), col(H * DK, ck), col(H * DV, cv), col(H * DV, cg),
                  full(cos_l), full(sin_l), pl.BlockSpec((1, 8, DK), lambda b, c: (c, 0, 0)),
                  full(intra), full(kdec), full(qdec)],
        out_specs=pl.BlockSpec((RB, H * DV), lambda b, c: (b * nc + c, 0)),
        out_shape=jax.ShapeDtypeStruct((T, H * DV), BF16),
        scratch_shapes=[pltpu.VMEM((H, DK, DV), F32)],
        compiler_params=_cparams(("arbitrary", "arbitrary")),
    )(p1, p1, p1, p1, cos_l, sin_l, base, intra, kdec, qdec)


def _pad_cols(w, n):
    return jnp.pad(w, ((0, 0), (0, n - w.shape[1])))


def _rows(vectors, n_rows):
    width = vectors[0].shape[0]
    pad = [jnp.zeros((n_rows - len(vectors), width), F32)] if n_rows > len(vectors) else []
    return jnp.concatenate([v.astype(F32).reshape(1, width) for v in vectors] + pad, axis=0)


def kernel(x, l0_norm_mix, l0_w_in, l0_rwkv_mu, l0_rwkv_w0, l0_rwkv_w2, l0_rwkv_a0, l0_rwkv_a2, l0_rwkv_g2, l0_rwkv_k_k, l0_rwkv_k_a, l0_rwkv_r_k, l0_rwkv_ln_g, l0_rwkv_ln_b, l0_ssd_conv_w, l0_ssd_conv_b, l0_ssd_dt_bias, l0_ssd_a_log, l0_ssd_d, l0_ssd_norm_g, l0_w_out, l0_norm_ffn, l0_ffn_up, l0_ffn_conv_w, l0_ffn_conv_b, l0_ffn_down, l1_norm_mix, l1_w_in, l1_s5_lam_re, l1_s5_lam_im, l1_s5_log_dt, l1_s5_b_re, l1_s5_b_im, l1_s5_c_re, l1_s5_c_im, l1_s5_d, l1_s5_w_glu, l1_s5_b_glu, l1_w_out, l1_norm_ffn, l1_ffn_up, l1_ffn_conv_w, l1_ffn_conv_b, l1_ffn_down, final_norm):
    B, L, D = x.shape
    T = B * L
    x2 = x.reshape(T, D).astype(F32)

    RP = 3 * RWKV_W + 256
    o_z = RP
    o_xbc = o_z + SSD_W
    o_dt = o_xbc + SSD_W + 2 * SSD_GROUPS * SSD_N
    n_xbc = o_dt - o_xbc
    w0 = _pad_cols(l0_w_in.astype(BF16), o_dt + 128)
    p0 = _norm_proj(x2, l0_norm_mix, w0, ((o_z, 0, SSD_W), (o_xbc, SSD_W, n_xbc), (0, SSD_W + n_xbc, RP),
                                          (o_dt, SSD_W + n_xbc + RP, 128)))

    mu = l0_rwkv_mu
    vec = _rows([mu[0:512], mu[512:1024], mu[1024:1536], l0_rwkv_w0, l0_rwkv_a0, l0_rwkv_k_k, l0_rwkv_k_a,
                 l0_rwkv_r_k.reshape(-1), l0_rwkv_ln_g, l0_rwkv_ln_b], 16)
    vec2 = _rows([mu[1536:1664], mu[1664:1792]], 8)
    zl = jnp.zeros((64, RWKV_W), F32)
    w2a2 = jnp.concatenate([jnp.concatenate([l0_rwkv_w2.astype(F32), zl], axis=1),
                            jnp.concatenate([zl, l0_rwkv_a2.astype(F32)], axis=1)], axis=0).astype(BF16)

    cw = l0_ssd_conv_w
    cb = l0_ssd_conv_b
    cwx = _rows([cw[k, :SSD_W] for k in range(SSD_CONV)], 8)
    cwbc = _rows([cw[k, SSD_W:] for k in range(SSD_CONV)], 8)
    vecx = _rows([cb[:SSD_W], l0_ssd_norm_g, jnp.repeat(l0_ssd_d, SSD_P)], 8)
    vecbc = _rows([cb[SSD_W:]], 8)
    hpad = jnp.zeros((128 - SSD_HEADS,), F32)
    vecd = _rows([jnp.concatenate([l0_ssd_dt_bias.astype(F32), hpad]),
                  jnp.concatenate([-jnp.exp(l0_ssd_a_log.astype(F32)), hpad])], 8)
    y_a, y_b = _rwkv_ssd_group(p0, (5, 6, 7, 32, 33), (0, 1, 4, 34), B, L, vec, vec2, w2a2,
                               l0_rwkv_g2.astype(BF16), cwx, cwbc, vecx, vecbc, vecd)

    x2 = _mix_ffn(x2, y_a, y_b, l0_w_out.astype(BF16), L, l0_norm_ffn, l0_ffn_up.astype(BF16),
                  l0_ffn_conv_w.astype(F32), l0_ffn_conv_b.astype(F32), l0_ffn_down.astype(BF16),
                  final_norm, False)

    p1 = _norm_proj(x2, l1_norm_mix, l1_w_in.astype(BF16), ((1536, 0, 1024), (2560, 1024, 1024), (0, 2048, 1536)))
    tables = _s5_tables(*(t.astype(F32) for t in (l1_s5_lam_re, l1_s5_lam_im, l1_s5_log_dt, l1_s5_b_re,
                                                   l1_s5_b_im, l1_s5_c_re, l1_s5_c_im)))
    y_c = _s5_group(p1, 4, B, L, tables, l1_s5_d.astype(F32), l1_s5_w_glu.astype(BF16),
                    l1_s5_b_glu.astype(F32))
    y_d = _ret_group(p1, (5, 6, 0, 1), B, L)
    x2 = _mix_ffn(x2, y_c, y_d, l1_w_out.astype(BF16), L, l1_norm_ffn, l1_ffn_up.astype(BF16),
                  l1_ffn_conv_w.astype(F32), l1_ffn_conv_b.astype(F32), l1_ffn_down.astype(BF16),
                  final_norm, True)
    return x2.reshape(B, L, D).astype(x.dtype)
```

```python
import functools
import math

import numpy as np
import jax
import jax.numpy as jnp
from jax import lax
from jax.experimental import pallas as pl
from jax.experimental.pallas import tpu as pltpu

F32 = jnp.float32
BF16 = jnp.bfloat16

EPS = 1e-6
D_MODEL = 1024

RWKV_HEADS = 8
RWKV_N = 64
RWKV_W = 512
RWKV_GN_EPS = 64e-5
RWKV_CHUNK = 64
RWKV_BLOCK_CHUNKS = 4
RWKV_WAVES = 1

SSD_HEADS = 16
SSD_P = 64
SSD_GROUPS = 2
SSD_N = 128
SSD_W = 1024
SSD_CONV = 4
SSD_CHUNK = 64
SSD_BLOCK_CHUNKS = 4

S5_W = 512
S5_GROUPS = 32
S5_H = 16
S5_P = 64
S5_LC = 8
S5_TILES = 4
S5_ROWS = 128

RET_HEADS = 4
RET_DK = 128
RET_DV = 256
RET_CHUNK = 256
RET_BLOCK_CHUNKS = 2

FFN_HIDDEN = 2816
FFN_TN = 256
FFN_HALO = 16

VMEM_LIMIT = 48 * 1024 * 1024


def _dot(a, b):
    return jnp.dot(a, b, preferred_element_type=F32)


def _dot_nt(a, b):
    return lax.dot_general(a, b, (((1,), (1,)), ((), ())), preferred_element_type=F32)


def _dot_tn(a, b):
    return lax.dot_general(a, b, (((0,), (0,)), ((), ())), preferred_element_type=F32)


def _split3(x):
    hi = x.astype(BF16)
    r1 = x - hi.astype(F32)
    mid = r1.astype(BF16)
    lo = (r1 - mid.astype(F32)).astype(BF16)
    return hi, mid, lo


def _dot_sel_rhs(x, sel):
    hi, mid, lo = _split3(x)
    return _dot(hi, sel) + _dot(mid, sel) + _dot(lo, sel)


def _dot_sel_lhs(sel, x):
    hi, mid, lo = _split3(x)
    return _dot(sel, hi) + _dot(sel, mid) + _dot(sel, lo)


def _sigmoid(x):
    return 0.5 + 0.5 * jnp.tanh(0.5 * x)


def _silu(x):
    return x * _sigmoid(x)


def _softplus(x):
    return jnp.maximum(x, 0.0) + jnp.log(1.0 + jnp.exp(-jnp.abs(x)))


def _cparams(sem):
    return pltpu.CompilerParams(dimension_semantics=sem, vmem_limit_bytes=VMEM_LIMIT)


def _proj_kernel(x_ref, g_ref, *refs, pieces):
    w_refs, o_ref = refs[:-1], refs[-1]
    x = x_ref[...]
    y = x * lax.rsqrt(jnp.mean(x * x, axis=-1, keepdims=True) + EPS)
    hn = (y * g_ref[...]).astype(BF16)
    for which, src, dst, width in pieces:
        o_ref[:, dst:dst + width] = _dot(hn, w_refs[which][:, src:src + width])


def _norm_proj(x2, g, ws, pieces, tm=512):
    T, D = x2.shape
    N = sum(p[3] for p in pieces)
    return pl.pallas_call(
        functools.partial(_proj_kernel, pieces=pieces),
        grid=(T // tm,),
        in_specs=[pl.BlockSpec((tm, D), lambda i: (i, 0)), _resident((1, D))] + [_resident(w.shape) for w in ws],
        out_specs=pl.BlockSpec((tm, N), lambda i: (i, 0)),
        out_shape=jax.ShapeDtypeStruct((T, N), F32),
        compiler_params=_cparams(("parallel",)),
    )(x2, g.reshape(1, D), *ws)


def _ffn_kernel(x_ref, xh_ref, ya_ref, yah_ref, yb_ref, ybh_ref, wo_ref, g_ref, wup_ref,
                cw_ref, cb_ref, wd_ref, gf_ref, o_ref, act_scr, *, blocks_per_seq, final_norm):
    i = pl.program_id(0)
    Hd = wd_ref.shape[0]
    tn = FFN_TN
    Ka = ya_ref.shape[1]

    def mixed(x, ya, yb):
        return x + _dot(ya.astype(BF16), wo_ref[0:Ka, :]) + _dot(yb.astype(BF16), wo_ref[Ka:, :])

    def norm(x):
        y = x * lax.rsqrt(jnp.mean(x * x, axis=-1, keepdims=True) + EPS)
        return (y * g_ref[...]).astype(BF16)

    x1e = mixed(jnp.concatenate([xh_ref[...], x_ref[...]], axis=0),
                jnp.concatenate([yah_ref[...], ya_ref[...]], axis=0),
                jnp.concatenate([ybh_ref[...], yb_ref[...]], axis=0))
    x1 = x1e[FFN_HALO:, :]
    keep = jnp.where(i % blocks_per_seq == 0, 0.0, 1.0)
    rowi = lax.broadcasted_iota(jnp.int32, (x1e.shape[0], 1), 0)
    hn = norm(x1e * jnp.where(rowi < FFN_HALO, keep, 1.0))

    def conv(u, cols):
        u1 = pltpu.roll(u, 1, 0)
        u2 = pltpu.roll(u, 2, 0)
        c = cw_ref[2:3, cols] * u + cw_ref[1:2, cols] * u1 + cw_ref[0:1, cols] * u2 + cb_ref[:, cols]
        return c[FFN_HALO:, :]

    for j in range(Hd // tn):
        gs = slice(j * tn, (j + 1) * tn)
        vs = slice(Hd + j * tn, Hd + (j + 1) * tn)
        cg = conv(_dot(hn, wup_ref[:, gs]), gs)
        cv = conv(_dot(hn, wup_ref[:, vs]), vs)
        act_scr[:, gs] = (_silu(cg) * cv).astype(BF16)

    y = x1 + _dot(act_scr[...], wd_ref[...])
    if final_norm:
        y = y * lax.rsqrt(jnp.mean(y * y, axis=-1, keepdims=True) + EPS) * gf_ref[...]
    o_ref[...] = y


def _resident(shape):
    return pl.BlockSpec(shape, lambda *_: (0,) * len(shape), pipeline_mode=pl.Buffered(1))


def _mix_ffn(x2, ya, yb, wo, seq_len, g, w_up, conv_w, conv_b, w_down, g_final, final_norm, tm=512):
    T, D = x2.shape
    Hd = w_down.shape[0]
    Ka, Kb = ya.shape[1], yb.shape[1]
    tm = min(tm, seq_len)
    hb = tm // FFN_HALO
    kern = functools.partial(_ffn_kernel, blocks_per_seq=seq_len // tm, final_norm=final_norm)

    def rows(width):
        return pl.BlockSpec((tm, width), lambda i: (i, 0))

    def halo(width):
        return pl.BlockSpec((FFN_HALO, width), lambda i: (jnp.maximum(i * hb - 1, 0), 0))

    return pl.pallas_call(
        kern,
        grid=(T // tm,),
        in_specs=[rows(D), halo(D), rows(Ka), halo(Ka), rows(Kb), halo(Kb),
                  _resident((Ka + Kb, D)),
                  _resident((1, D)), _resident((D, 2 * Hd)), _resident((3, 2 * Hd)), _resident((1, 2 * Hd)),
                  _resident((Hd, D)), _resident((1, D))],
        out_specs=pl.BlockSpec((tm, D), lambda i: (i, 0)),
        out_shape=jax.ShapeDtypeStruct((T, D), F32),
        scratch_shapes=[pltpu.VMEM((tm, Hd), BF16)],
        compiler_params=_cparams(("parallel",)),
    )(x2, x2, ya, ya, yb, yb, wo, g.reshape(1, D), w_up, conv_w, conv_b.reshape(1, -1), w_down,
      g_final.reshape(1, D))


def _rwkv_body(r_ref, k_ref, v_ref, zwa_ref, zg_ref, vec_ref, vec2_ref, w2a2_ref, g2_ref,
               eseg_ref, csum_ref, o_ref, h_scr, prev_scr, prev2_scr, y_scr):
    RB = r_ref.shape[0]
    C = RWKV_CHUNK
    N = RWKV_N

    row = lax.broadcasted_iota(jnp.int32, (RB, 1), 0)

    def shift_lerp(x, prev_row, mu):
        xs = jnp.where(row == 0, prev_row, pltpu.roll(x, 1, 0))
        return x + (xs - x) * mu

    r_raw, k_raw, v_raw = r_ref[...], k_ref[...], v_ref[...]
    zwa_raw, zg_raw = zwa_ref[...], zg_ref[...]
    r = shift_lerp(r_raw, prev_scr[0:1, :], vec_ref[0:1, :])
    k = shift_lerp(k_raw, prev_scr[1:2, :], vec_ref[1:2, :])
    v = shift_lerp(v_raw, prev_scr[2:3, :], vec_ref[2:3, :])
    zwa = shift_lerp(zwa_raw, prev2_scr[0:1, :], vec2_ref[0:1, :])
    zg = shift_lerp(zg_raw, prev2_scr[1:2, :], vec2_ref[1:2, :])
    prev_scr[0:1, :] = r_raw[RB - 1:RB, :]
    prev_scr[1:2, :] = k_raw[RB - 1:RB, :]
    prev_scr[2:3, :] = v_raw[RB - 1:RB, :]
    prev2_scr[0:1, :] = zwa_raw[RB - 1:RB, :]
    prev2_scr[1:2, :] = zg_raw[RB - 1:RB, :]

    w0, a0 = vec_ref[3:4, :], vec_ref[4:5, :]
    k_k, k_a, r_k = vec_ref[5:6, :], vec_ref[6:7, :], vec_ref[7:8, :]
    ln_g, ln_b = vec_ref[8:9, :], vec_ref[9:10, :]

    lane = lax.broadcasted_iota(jnp.int32, (1, 128), 1)
    lora_in = jnp.where(lane < 64, jnp.tanh(zwa), zwa).astype(BF16)
    wa = _dot(lora_in, w2a2_ref[...])
    w_log = -_softplus(-(w0 + wa[:, :RWKV_W])) - 0.5
    logw = -jnp.exp(w_log)
    a = _sigmoid(a0 + wa[:, RWKV_W:])
    g = _dot(_sigmoid(zg).astype(BF16), g2_ref[...])

    eseg = eseg_ref[...]

    def segsum(x, split=True):
        EW = eseg.shape[0]
        hi = x.astype(BF16)
        lo = (x - hi.astype(F32)).astype(BF16) if split else None
        parts = []
        for q in range(RWKV_W // EW):
            qs = slice(q * EW, (q + 1) * EW)
            sq = _dot(hi[:, qs], eseg)
            parts.append(sq + _dot(lo[:, qs], eseg) if split else sq)
        return jnp.concatenate(parts, axis=1)

    yield
    kk = k * k_k
    kk = kk * lax.rsqrt(jnp.maximum(segsum(kk * kk), 1e-24))
    k2 = k * (1.0 + (a - 1.0) * k_a)
    yield

    lg = _dot_sel_lhs(csum_ref[...], logw)
    lg_end = jnp.concatenate([jnp.broadcast_to(lg[(j + 1) * C - 1:(j + 1) * C, :], (C, RWKV_W))
                              for j in range(RB // C)], axis=0)
    eg = jnp.exp(lg)
    egi = jnp.exp(-lg)
    g_end = jnp.exp(lg_end)
    d_end = jnp.exp(lg_end - lg)
    yield
    kka = kk * a
    ab = -kk * jnp.exp(lg - logw)
    rb = r * eg
    bt = kka * egi
    kt = k2 * egi
    bg = kka * d_end
    kg = k2 * d_end
    yield

    W2 = 2 * N
    left = lax.broadcasted_iota(jnp.int32, (1, W2), 1) < N

    def bd(x):
        return jnp.concatenate([jnp.where(left, x, 0.0), jnp.where(left, 0.0, x)], axis=0)

    ri = lax.broadcasted_iota(jnp.int32, (2 * W2, 2 * W2), 0)
    ci = lax.broadcasted_iota(jnp.int32, (2 * W2, 2 * W2), 1)
    mask = (ci & (N - 1)) < (ri & (N - 1)) + jnp.where(ri >= W2, 1, 0)
    e_r = lax.broadcasted_iota(jnp.int32, (W2, W2), 0)
    e_c = lax.broadcasted_iota(jnp.int32, (W2, W2), 1)
    eye = jnp.where(e_r == e_c, 1.0, 0.0)
    zeros_b = jnp.zeros((W2, W2), BF16)

    all_units = [(j, p) for j in range(RB // C) for p in range(RWKV_HEADS // 2)]

    def piece(x, j, p):
        return x[j * C:(j + 1) * C, p * W2:(p + 1) * W2]

    def each(fn, n):
        out = []
        for i in range(n):
            out.append(fn(i))
            yield
        return out

    def diag2(ya, yb):
        z = jnp.zeros_like(ya)
        return jnp.concatenate([jnp.concatenate([ya, z], axis=1), jnp.concatenate([z, yb], axis=1)], axis=0)

    def diag2_of(xx):
        return diag2(xx[:, :W2], xx[:, W2:])

    def wave(units):
        U = len(units)
        ab_u = yield from each(lambda i: bd(piece(ab, *units[i])).astype(BF16), U)
        AR = yield from each(lambda i: jnp.concatenate([ab_u[i], bd(piece(rb, *units[i])).astype(BF16)], axis=0), U)
        BK = yield from each(lambda i: jnp.concatenate([bd(piece(bt, *units[i])), bd(piece(kt, *units[i]))],
                                                       axis=0).astype(BF16), U)
        v_u = yield from each(lambda i: bd(piece(v, *units[i])).astype(BF16), U)
        G = yield from each(lambda i: jnp.where(mask, _dot_nt(AR[i], BK[i]), 0.0), U)
        SU = U // 2
        AA = [jnp.concatenate([G[2 * s][:W2, :W2], G[2 * s + 1][:W2, :W2]], axis=1) for s in range(SU)]
        SS = [jnp.concatenate([eye, eye], axis=1) + AA[s] for s in range(SU)]
        AAb = [AA[s].astype(BF16) for s in range(SU)]
        QQ = yield from each(lambda s: _dot(AAb[s], diag2_of(AAb[s])), SU)
        for _ in range(int(math.log2(C)) - 2):
            QQb = [QQ[s].astype(BF16) for s in range(SU)]
            QS = yield from each(lambda s: _dot(jnp.concatenate([QQb[s], SS[s].astype(BF16)], axis=0),
                                                diag2_of(QQb[s])), SU)
            QQ = [QS[s][:W2] for s in range(SU)]
            SS = [SS[s] + QS[s][W2:] for s in range(SU)]
        SS = yield from each(
            lambda s: (SS[s] + _dot(SS[s].astype(BF16), diag2_of(QQ[s].astype(BF16)))).astype(BF16), SU)
        S = [SS[i // 2][:, (i % 2) * W2:(i % 2 + 1) * W2] for i in range(U)]
        akv2 = yield from each(
            lambda s: _dot(jnp.concatenate([G[2 * s][:W2, W2:], G[2 * s + 1][:W2, W2:]], axis=1).astype(BF16),
                           diag2(v_u[2 * s], v_u[2 * s + 1])), SU)
        akv = [akv2[i // 2][:, (i % 2) * W2:(i % 2 + 1) * W2] for i in range(U)]
        WU = yield from each(
            lambda i: _dot(S[i], jnp.concatenate([ab_u[i], akv[i].astype(BF16)], axis=1)).astype(BF16), U)
        Z = [jnp.concatenate([WU[i], jnp.concatenate([zeros_b, v_u[i]], axis=1)], axis=0) for i in range(U)]
        QY = yield from each(lambda i: _dot(G[i][W2:, :].astype(BF16), Z[i]), U)
        BKgT = yield from each(lambda i: jnp.concatenate([bd(piece(bg, *units[i])).T, bd(piece(kg, *units[i])).T],
                                                         axis=1).astype(BF16), U)
        MN = yield from each(lambda i: _dot(BKgT[i], Z[i]), U)
        QM = [jnp.concatenate([AR[i][W2:].astype(F32) + QY[i][:, :W2], MN[i][:, :W2]], axis=0).astype(BF16)
              for i in range(U)]
        gcol = [jnp.sum(eye * piece(g_end, j, p)[0:1, :], axis=1, keepdims=True) for j, p in units]

        for s in range(SU):
            ia, ib = 2 * s, 2 * s + 1
            Ha, Hb = h_scr[units[ia][1]], h_scr[units[ib][1]]
            YH2 = _dot(jnp.concatenate([QM[ia], QM[ib]], axis=1), diag2(Ha.astype(BF16), Hb.astype(BF16)))
            for i, H, YH in ((ia, Ha, YH2[:, :W2]), (ib, Hb, YH2[:, W2:])):
                j, p = units[i]
                ybd = YH[:W2] + QY[i][:, W2:]
                y_scr[j * C:(j + 1) * C, p * W2:(p + 1) * W2] = ybd[:C] + ybd[C:]
                h_scr[p] = H * gcol[i] + YH[W2:] + MN[i][:, W2:]
            yield

    per_wave = len(all_units) // RWKV_WAVES
    for w in range(RWKV_WAVES):
        yield from wave(all_units[w * per_wave:(w + 1) * per_wave])

    y = y_scr[...]
    inv_n = 1.0 / N
    mean = segsum(y) * inv_n
    d = y - mean
    yield
    var = segsum(d * d, split=False) * inv_n
    yn = d * lax.rsqrt(var + RWKV_GN_EPS) * ln_g + ln_b
    yield
    bonus = segsum(r * k2 * r_k, split=False) * v
    o_ref[...] = ((yn + bonus) * g).astype(o_ref.dtype)


def _ssd_body(z_ref, xs_ref, bc_ref, dt_ref, cwx_ref, cwbc_ref, vecx_ref, vecbc_ref, vecd_ref,
              eh_ref, ltri_ref, o_ref, st_scr, tailx_scr, tailbc_scr, y_scr):
    RB = xs_ref.shape[0]
    Q = SSD_CHUNK
    P, N = SSD_P, SSD_N
    GW = SSD_W // SSD_GROUPS

    def conv_silu(x_ref, tail_scr, cw_ref, b_ref):
        parts = []
        for q in range(x_ref.shape[1] // 256):
            cs = slice(q * 256, (q + 1) * 256)
            x_raw = x_ref[:, cs]
            ext = jnp.concatenate([tail_scr[:, cs], x_raw], axis=0)
            acc = b_ref[0:1, cs] + cw_ref[SSD_CONV - 1:SSD_CONV, cs] * x_raw
            for s in range(1, SSD_CONV):
                acc = acc + cw_ref[SSD_CONV - 1 - s:SSD_CONV - s, cs] * pltpu.roll(ext, s, 0)[8:, :]
            tail_scr[:, cs] = x_raw[RB - 8:, :]
            parts.append(_silu(acc))
            yield
        return jnp.concatenate(parts, axis=1)

    xs = yield from conv_silu(xs_ref, tailx_scr, cwx_ref, vecx_ref)
    bc = yield from conv_silu(bc_ref, tailbc_scr, cwbc_ref, vecbc_ref)

    eh = eh_ref[...]
    dt = _softplus(dt_ref[...] + vecd_ref[0:1, :])
    a = dt * vecd_ref[1:2, :]
    acum = _dot_sel_lhs(ltri_ref[...], a)
    dt_e = _dot_sel_rhs(dt, eh)
    acum_e = _dot_sel_rhs(acum, eh)
    acum_t = acum.T
    yield
    xdt = xs * dt_e
    xdt_b = xdt.astype(BF16)
    ea = jnp.exp(acum_e)
    bc_b = bc.astype(BF16)
    yield

    li = lax.broadcasted_iota(jnp.int32, (Q, Q), 0)
    si = lax.broadcasted_iota(jnp.int32, (Q, Q), 1)
    causal = si <= li

    for ch in range(RB // Q):
        rs = slice(ch * Q, (ch + 1) * Q)
        a_last = acum_e[(ch + 1) * Q - 1:(ch + 1) * Q, :]
        xdte = (xdt[rs] * jnp.exp(a_last - acum_e[rs])).astype(BF16)
        chunk_decay = jnp.exp(a_last)
        for gi in range(SSD_GROUPS):
            gs = slice(gi * GW, (gi + 1) * GW)
            Bg = bc_b[rs, gi * N:(gi + 1) * N]
            Cg = bc_b[rs, SSD_GROUPS * N + gi * N:SSD_GROUPS * N + (gi + 1) * N]
            cb = _dot_nt(Cg, Bg)
            prev = st_scr[gi]
            y_off = _dot(Cg, prev.astype(BF16)) * ea[rs, gs]
            st_scr[gi] = prev * chunk_decay[:, gs] + _dot(bc[rs, gi * N:(gi + 1) * N].T.astype(BF16), xdte[:, gs])
            yield
            ys = []
            for j in range(SSD_HEADS // SSD_GROUPS):
                hh = gi * (SSD_HEADS // SSD_GROUPS) + j
                hs = slice(hh * P, (hh + 1) * P)
                seg = acum_e[rs, hs] - acum_t[hh:hh + 1, rs]
                wm = jnp.where(causal, cb * jnp.exp(seg), 0.0).astype(BF16)
                ys.append(_dot(wm, xdt_b[rs, hs]))
                if j % 2 == 1:
                    yield
            y_scr[rs, gs] = jnp.concatenate(ys, axis=1) + y_off

    for gi in range(SSD_GROUPS):
        gs = slice(gi * GW, (gi + 1) * GW)
        yg = (y_scr[:, gs] + vecx_ref[2:3, gs] * xs[:, gs]) * _silu(z_ref[:, gs])
        yg = yg * lax.rsqrt(jnp.mean(yg * yg, axis=-1, keepdims=True) + EPS)
        o_ref[:, gs] = (yg * vecx_ref[1:2, gs]).astype(o_ref.dtype)
        yield


N_RWKV_IN, N_SSD_IN = 11, 11
N_RWKV_SCR, N_SSD_SCR = 4, 4
RWKV_TURN, SSD_TURN = 3, 1


def _rwkv_ssd_kernel(*refs):
    n_in = N_RWKV_IN + N_SSD_IN
    rwkv_in, ssd_in = refs[:N_RWKV_IN], refs[N_RWKV_IN:n_in]
    o_rwkv, o_ssd = refs[n_in], refs[n_in + 1]
    scr = refs[n_in + 2:]
    rwkv_scr, ssd_scr = scr[:N_RWKV_SCR], scr[N_RWKV_SCR:]

    @pl.when(pl.program_id(1) == 0)
    def _():
        for s in rwkv_scr[:3] + ssd_scr[:3]:
            s[...] = jnp.zeros_like(s)

    _interleave((_rwkv_body(*rwkv_in, o_rwkv, *rwkv_scr), RWKV_TURN), (_ssd_body(*ssd_in, o_ssd, *ssd_scr), SSD_TURN))


def _interleave(*weighted):
    live = list(weighted)
    while live:
        for item in list(live):
            gen, turn = item
            for _ in range(turn):
                try:
                    next(gen)
                except StopIteration:
                    live.remove(item)
                    break


def _rwkv_ssd_group(p0, rwkv_cols, ssd_cols, batch, seq_len, vec, vec2, w2a2, g2, cwx, cwbc, vecx, vecbc, vecd):
    C, Q = RWKV_CHUNK, SSD_CHUNK
    assert Q == SSD_P
    RB = C * RWKV_BLOCK_CHUNKS
    assert RB == Q * SSD_BLOCK_CHUNKS
    nc = seq_len // RB
    T = batch * seq_len
    cr, ck, cv, czwa, czg = rwkv_cols
    cz, cx, cbc, cdt = ssd_cols
    eseg = jnp.asarray(np.arange(256)[:, None] // RWKV_N == np.arange(256)[None, :] // RWKV_N, BF16)
    same_chunk = np.arange(RB)[:, None] // C == np.arange(RB)[None, :] // C
    ltri = jnp.asarray(np.tril(np.ones((RB, RB))) * same_chunk, BF16)
    csum = ltri
    eh = np.zeros((128, SSD_W), np.float32)
    for hh in range(SSD_HEADS):
        eh[hh, hh * SSD_P:(hh + 1) * SSD_P] = 1.0
    eh = jnp.asarray(eh, BF16)

    def col(width, idx):
        return pl.BlockSpec((RB, width), lambda b, c: (b * nc + c, idx))

    def full(arr):
        return pl.BlockSpec(arr.shape, lambda b, c: (0,) * arr.ndim)

    rwkv_in = [col(RWKV_W, cr), col(RWKV_W, ck), col(RWKV_W, cv), col(128, czwa), col(128, czg),
               full(vec), full(vec2), full(w2a2), full(g2), full(eseg), full(csum)]
    ssd_in = [col(SSD_W, cz), col(SSD_W, cx), col(512, cbc), col(128, cdt),
              full(cwx), full(cwbc), full(vecx), full(vecbc), full(vecd), full(eh), full(ltri)]
    assert len(rwkv_in) == N_RWKV_IN and len(ssd_in) == N_SSD_IN
    rwkv_scr = [pltpu.VMEM((RWKV_HEADS // 2, 2 * RWKV_N, 2 * RWKV_N), F32),
                pltpu.VMEM((8, RWKV_W), F32), pltpu.VMEM((8, 128), F32), pltpu.VMEM((RB, RWKV_W), F32)]
    ssd_scr = [pltpu.VMEM((SSD_GROUPS, SSD_N, SSD_W // SSD_GROUPS), F32),
               pltpu.VMEM((8, SSD_W), F32), pltpu.VMEM((8, 512), F32), pltpu.VMEM((RB, SSD_W), F32)]
    return pl.pallas_call(
        _rwkv_ssd_kernel,
        grid=(batch, nc),
        in_specs=rwkv_in + ssd_in,
        out_specs=[pl.BlockSpec((RB, RWKV_W), lambda b, c: (b * nc + c, 0)),
                   pl.BlockSpec((RB, SSD_W), lambda b, c: (b * nc + c, 0))],
        out_shape=[jax.ShapeDtypeStruct((T, RWKV_W), BF16), jax.ShapeDtypeStruct((T, SSD_W), BF16)],
        scratch_shapes=rwkv_scr + ssd_scr,
        compiler_params=_cparams(("arbitrary", "arbitrary")),
    )(p0, p0, p0, p0, p0, vec, vec2, w2a2, g2, eseg, csum,
      p0, p0, p0, p0, cwx, cwbc, vecx, vecbc, vecd, eh, ltri)


def _s5_kernel(u_ref, blk_ref, bzr_ref, bzi_ref, czr_ref, czi_ref, lam_ref, d_ref, wglu_ref, bglu_ref, o_ref,
               st_scr, zre_scr, zim_scr, sre_scr, sim_scr, y_scr, toep_ref, fold_scr):
    i = pl.program_id(1)
    Mb = u_ref.shape[0]
    LC, NT = S5_LC, S5_TILES
    HW = S5_W // NT * (S5_P // S5_H)

    @pl.when(jnp.logical_and(pl.program_id(0) == 0, i == 0))
    def _():
        toep_ref[...] = jnp.zeros_like(toep_ref)
        for n in range(NT):
            for li in range(LC):
                for lo in range(li, LC):
                    toep_ref[n, li * 128:(li + 1) * 128, lo * 128:(lo + 1) * 128] = blk_ref[lo - li, n]

    @pl.when(i == 0)
    def _():
        st_scr[...] = jnp.zeros_like(st_scr)

    u = u_ref[...]
    ucat = []
    for n in range(NT):
        un = jnp.concatenate([u[:, l * S5_W + n * 128:l * S5_W + (n + 1) * 128] for l in range(LC)],
                             axis=1).astype(BF16)
        ucat.append(un)
        zre_scr[:, n * HW:(n + 1) * HW] = _dot(un, bzr_ref[n])
        zim_scr[:, n * HW:(n + 1) * HW] = _dot(un, bzi_ref[n])

    lr = lam_ref[0:1, :]
    lim = lam_ref[1:2, :]

    def step(m, carry):
        sre, sim = carry
        sre_scr[pl.ds(m, 1), :] = sre
        sim_scr[pl.ds(m, 1), :] = sim
        zr = zre_scr[pl.ds(m, 1), :]
        zi = zim_scr[pl.ds(m, 1), :]
        return lr * sre - lim * sim + zr, lr * sim + lim * sre + zi

    sre, sim = lax.fori_loop(0, Mb, step, (st_scr[0:1, :], st_scr[1:2, :]))
    st_scr[0:1, :] = sre
    st_scr[1:2, :] = sim

    for n in range(NT):
        s_re = sre_scr[:, n * HW:(n + 1) * HW].astype(BF16)
        s_im = sim_scr[:, n * HW:(n + 1) * HW].astype(BF16)
        yn = _dot(ucat[n], toep_ref[n]) + _dot(s_re, czr_ref[n]) + _dot(s_im, czi_ref[n])
        for l in range(LC):
            y_scr[:, l * S5_W + n * 128:l * S5_W + (n + 1) * 128] = yn[:, l * 128:(l + 1) * 128]

    y = y_scr[...] + d_ref[...] * u
    y = 0.5 * y * (1.0 + jnp.tanh(math.sqrt(2.0 / math.pi) * (y + 0.044715 * (y * y * y))))
    wglu = wglu_ref[...]
    for l in range(LC):
        yl = y[:, l * S5_W:(l + 1) * S5_W]
        gate = _sigmoid(_dot(yl.astype(BF16), wglu) + bglu_ref[...])
        out_l = yl * gate
        for n in range(NT):
            fold_scr[n, pl.ds(l, Mb, stride=LC), :] = out_l[:, n * 128:(n + 1) * 128]
    for n in range(NT):
        o_ref[:, n * 128:(n + 1) * 128] = fold_scr[n]


def _s5_tables(lam_re, lam_im, log_dt, b_re, b_im, c_re, c_im):
    LC, NT, G, H, Pn = S5_LC, S5_TILES, S5_GROUPS, S5_H, S5_P
    GL = G // NT
    dt = jnp.exp(log_dt)[:, None]
    ar, ai = lam_re * dt, lam_im * dt
    jj = jnp.arange(LC + 1, dtype=F32)[:, None, None]
    mag = jnp.exp(jj * ar[None])
    pw_re, pw_im = mag * jnp.cos(jj * ai[None]), mag * jnp.sin(jj * ai[None])
    nr, ni = pw_re[1] - 1.0, pw_im[1]
    den = lam_re * lam_re + lam_im * lam_im
    qr, qi = (nr * lam_re + ni * lam_im) / den, (ni * lam_re - nr * lam_im) / den
    bb_re = qr[..., None] * b_re - qi[..., None] * b_im
    bb_im = qr[..., None] * b_im + qi[..., None] * b_re
    cl_re = c_re[None] * pw_re[:, :, None, :] - c_im[None] * pw_im[:, :, None, :]
    cl_im = c_re[None] * pw_im[:, :, None, :] + c_im[None] * pw_re[:, :, None, :]
    bt_re, bt_im = jnp.swapaxes(bb_re, 1, 2), jnp.swapaxes(bb_im, 1, 2)
    kd = jnp.sum(cl_re[:LC, :, :, None, :] * bt_re[None, :, None, :, :]
                 - cl_im[:LC, :, :, None, :] * bt_im[None, :, None, :, :], axis=-1)
    def spread(x, rows_per_group, cols_per_group):
        sel = np.tile(np.eye(cols_per_group, dtype=np.float32), (1, GL))
        keep = (np.arange(GL * rows_per_group)[:, None] // rows_per_group
                == np.arange(GL * cols_per_group)[None, :] // cols_per_group)
        return jnp.where(keep, jnp.einsum("...rc,cd->...rd", x, jnp.asarray(sel)), 0.0)

    blk = spread(jnp.swapaxes(kd, -1, -2).reshape(LC, NT, GL * H, H), H, H).astype(BF16)
    def cz_part(cl):
        x = spread(jnp.swapaxes(cl[1:], -1, -2).reshape(LC, NT, GL * Pn, H), Pn, H).astype(BF16)
        return jnp.concatenate([x[l] for l in range(LC)], axis=-1)
    cz = (cz_part(cl_re), cz_part(-cl_im))
    jr = (LC - 1.0) - jnp.arange(LC, dtype=F32)[:, None, None]
    mag_r = jnp.exp(jr * ar[None])
    rev_re, rev_im = mag_r * jnp.cos(jr * ai[None]), mag_r * jnp.sin(jr * ai[None])
    bl_re = rev_re[..., None] * bb_re[None] - rev_im[..., None] * bb_im[None]
    bl_im = rev_re[..., None] * bb_im[None] + rev_im[..., None] * bb_re[None]

    def bz_part(bl):
        x = spread(jnp.swapaxes(bl, -1, -2).reshape(LC, NT, GL * H, Pn), H, Pn).astype(BF16)
        return jnp.swapaxes(x, 0, 1).reshape(NT, LC * GL * H, GL * Pn)
    bz = (bz_part(bl_re), bz_part(bl_im))
    lam = _rows([pw_re[LC].reshape(-1), pw_im[LC].reshape(-1)], 8)
    return blk, bz, cz, lam


def _s5_group(p1, col_u, batch, seq_len, tables, d_skip, w_glu, b_glu):
    LC = S5_LC
    T = batch * seq_len
    blk, bz, cz, lam = tables
    rows_per_seq = seq_len // LC
    Mb = min(S5_ROWS, rows_per_seq)
    nb = rows_per_seq // Mb
    NPc = p1.shape[1]
    W = LC * S5_W
    d_t = jnp.tile(d_skip.reshape(1, S5_W), (1, LC))
    SW = S5_TILES * (S5_GROUPS // S5_TILES) * S5_P

    def full(arr):
        return pl.BlockSpec(arr.shape, lambda b, i: (0,) * arr.ndim)

    def kern(u_ref, *rest):
        ucat_scr, fold_scr = rest[-2], rest[-1]
        for n in range(S5_TILES):
            fold_scr[n] = u_ref[:, n * 128:(n + 1) * 128]
        for n in range(S5_TILES):
            for l in range(LC):
                ucat_scr[:, l * S5_W + n * 128:l * S5_W + (n + 1) * 128] = fold_scr[n, pl.ds(l, Mb, stride=LC), :]
        _s5_kernel(ucat_scr, *rest[:-2], fold_scr)

    return pl.pallas_call(
        kern,
        grid=(batch, nb),
        in_specs=[pl.BlockSpec((Mb * LC, S5_W), lambda b, i: (b * nb + i, col_u)),
                  _resident(blk.shape), _resident(bz[0].shape), _resident(bz[1].shape),
                  _resident(cz[0].shape), _resident(cz[1].shape), full(lam), full(d_t),
                  full(w_glu),
                  pl.BlockSpec((1, S5_W), lambda b, i: (0, 0))],
        out_specs=pl.BlockSpec((Mb * LC, S5_W), lambda b, i: (b * nb + i, 0)),
        out_shape=jax.ShapeDtypeStruct((T, S5_W), F32),
        scratch_shapes=[pltpu.VMEM((8, SW), F32),
                        pltpu.VMEM((Mb, SW), F32), pltpu.VMEM((Mb, SW), F32),
                        pltpu.VMEM((Mb, SW), F32), pltpu.VMEM((Mb, SW), F32),
                        pltpu.VMEM((Mb, W), F32),
                        pltpu.VMEM((S5_TILES, LC * 128, LC * 128), BF16),
                        pltpu.VMEM((Mb, W), F32),
                        pltpu.VMEM((S5_TILES, Mb * LC, 128), F32)],
        compiler_params=_cparams(("arbitrary", "arbitrary")),
    )(p1, blk, *bz, *cz, lam, d_t, w_glu, b_glu.reshape(1, S5_W))


def _ret_kernel(q_ref, k_ref, v_ref, g_ref, cosl_ref, sinl_ref, base_ref, intra_ref, kdec_ref, qdec_ref,
                o_ref, st_scr, *, chunk_decay):
    c = pl.program_id(1)
    DK, DV = RET_DK, RET_DV

    @pl.when(c == 0)
    def _():
        st_scr[...] = jnp.zeros_like(st_scr)

    Q = intra_ref.shape[1]
    scale = DK ** -0.5
    cb, sb = base_ref[0, 0:1, :], base_ref[0, 1:2, :]
    sign = jnp.where(lax.broadcasted_iota(jnp.int32, (1, DK), 1) < DK // 2, -1.0, 1.0)
    for ch in range(q_ref.shape[0] // Q):
        rs = slice(ch * Q, (ch + 1) * Q)
        cl, sl = cosl_ref[rs, :], sinl_ref[rs, :]
        cosf = cb * cl - sb * sl
        sinf = sign * (sb * cl + cb * sl)
        for h in range(RET_HEADS):
            ks = slice(h * DK, (h + 1) * DK)
            vs = slice(h * DV, (h + 1) * DV)
            qh = q_ref[rs, ks]
            kh = k_ref[rs, ks]
            qh = qh * cosf + pltpu.roll(qh, DK // 2, 1) * sinf
            kh = (kh * cosf + pltpu.roll(kh, DK // 2, 1) * sinf) * scale
            vh = v_ref[rs, vs].astype(BF16)
            scores = (_dot_nt(qh.astype(BF16), kh.astype(BF16)) * intra_ref[h]).astype(BF16)
            prev = st_scr[h]
            y = _dot(scores, vh) + _dot((qh * qdec_ref[:, ks]).astype(BF16), prev.astype(BF16))
            kd = (kh * kdec_ref[:, ks]).T.astype(BF16)
            st_scr[h] = prev * chunk_decay[h] + _dot(kd, vh)
            y = y * lax.rsqrt(jnp.mean(y * y, axis=-1, keepdims=True) + EPS)
            o_ref[rs, vs] = (_silu(g_ref[rs, vs]) * y).astype(o_ref.dtype)


def _ret_group(p1, cols, batch, seq_len):
    Q = min(RET_CHUNK, seq_len)
    RB = min(Q * RET_BLOCK_CHUNKS, seq_len)
    nc = seq_len // RB
    T = batch * seq_len
    cq, ck, cv, cg = cols
    H, DK, DV = RET_HEADS, RET_DK, RET_DV
    half = DK // 2
    theta = 1.0 / (10000.0 ** jnp.linspace(0.0, 1.0, half, dtype=F32))
    theta2 = jnp.concatenate([theta, theta])[None, :]
    ang_l = jnp.arange(RB).astype(F32)[:, None] * theta2
    cos_l, sin_l = jnp.cos(ang_l), jnp.sin(ang_l)
    ang_b = (jnp.arange(nc) * RB).astype(F32)[:, None] * theta2
    base = jnp.concatenate([jnp.cos(ang_b)[:, None, :], jnp.sin(ang_b)[:, None, :],
                            jnp.zeros((nc, 6, DK), F32)], axis=1)
    log_gamma = jnp.log(1.0 - 2.0 ** (-5.0 - jnp.arange(H, dtype=F32)))
    t = jnp.arange(Q, dtype=F32)
    diff = t[:, None] - t[None, :]
    intra = jnp.where((diff >= 0)[None], jnp.exp(jnp.maximum(diff, 0.0)[None] * log_gamma[:, None, None]), 0.0)
    kdec = jnp.repeat(jnp.exp((Q - 1.0 - t)[:, None] * log_gamma[None, :]), DK, axis=1)
    qdec = jnp.repeat(jnp.exp((t + 1.0)[:, None] * log_gamma[None, :]), DK, axis=1)
    chunk_decay = tuple(float((1.0 - 2.0 ** (-5.0 - h)) ** Q) for h in range(H))

    def col(width, idx):
        return pl.BlockSpec((RB, width), lambda b, c: (b * nc + c, idx))

    def full(arr):
        return pl.BlockSpec(arr.shape, lambda b, c: (0,) * arr.ndim)

    return pl.pallas_call(
        functools.partial(_ret_kernel, chunk_decay=chunk_decay),
        grid=(batch, nc),
        in_specs=[col(H * DK, cq), col(H * DK, ck), col(H * DV, cv), col(H * DV, cg),
                  full(cos_l), full(sin_l), pl.BlockSpec((1, 8, DK), lambda b, c: (c, 0, 0)),
                  full(intra), full(kdec), full(qdec)],
        out_specs=pl.BlockSpec((RB, H * DV), lambda b, c: (b * nc + c, 0)),
        out_shape=jax.ShapeDtypeStruct((T, H * DV), BF16),
        scratch_shapes=[pltpu.VMEM((H, DK, DV), F32)],
        compiler_params=_cparams(("arbitrary", "arbitrary")),
    )(p1, p1, p1, p1, cos_l, sin_l, base, intra, kdec, qdec)


def _pad_cols(w, n):
    return jnp.pad(w, ((0, 0), (0, n - w.shape[1])))


def _rows(vectors, n_rows):
    width = vectors[0].shape[0]
    pad = [jnp.zeros((n_rows - len(vectors), width), F32)] if n_rows > len(vectors) else []
    return jnp.concatenate([v.astype(F32).reshape(1, width) for v in vectors] + pad, axis=0)


def kernel(x, l0_norm_mix, l0_w_in, l0_rwkv_mu, l0_rwkv_w0, l0_rwkv_w2, l0_rwkv_a0, l0_rwkv_a2, l0_rwkv_g2, l0_rwkv_k_k, l0_rwkv_k_a, l0_rwkv_r_k, l0_rwkv_ln_g, l0_rwkv_ln_b, l0_ssd_conv_w, l0_ssd_conv_b, l0_ssd_dt_bias, l0_ssd_a_log, l0_ssd_d, l0_ssd_norm_g, l0_w_out, l0_norm_ffn, l0_ffn_up, l0_ffn_conv_w, l0_ffn_conv_b, l0_ffn_down, l1_norm_mix, l1_w_in, l1_s5_lam_re, l1_s5_lam_im, l1_s5_log_dt, l1_s5_b_re, l1_s5_b_im, l1_s5_c_re, l1_s5_c_im, l1_s5_d, l1_s5_w_glu, l1_s5_b_glu, l1_w_out, l1_norm_ffn, l1_ffn_up, l1_ffn_conv_w, l1_ffn_conv_b, l1_ffn_down, final_norm):
    B, L, D = x.shape
    T = B * L
    x2 = x.reshape(T, D).astype(F32)

    RP = 3 * RWKV_W + 256
    o_z = RP
    o_xbc = o_z + SSD_W
    o_dt = o_xbc + SSD_W + 2 * SSD_GROUPS * SSD_N
    n_xbc = o_dt - o_xbc
    w0 = l0_w_in.astype(BF16)
    w_dt = _pad_cols(w0[:, o_dt:], 128)
    p0 = _norm_proj(x2, l0_norm_mix, (w0, w_dt), ((0, o_z, 0, SSD_W), (0, o_xbc, SSD_W, n_xbc),
                                                  (0, 0, SSD_W + n_xbc, RP), (1, 0, SSD_W + n_xbc + RP, 128)))

    mu = l0_rwkv_mu
    vec = _rows([mu[0:512], mu[512:1024], mu[1024:1536], l0_rwkv_w0, l0_rwkv_a0, l0_rwkv_k_k, l0_rwkv_k_a,
                 l0_rwkv_r_k.reshape(-1), l0_rwkv_ln_g, l0_rwkv_ln_b], 16)
    vec2 = _rows([mu[1536:1664], mu[1664:1792]], 8)
    zl = jnp.zeros((64, RWKV_W), F32)
    w2a2 = jnp.concatenate([jnp.concatenate([l0_rwkv_w2.astype(F32), zl], axis=1),
                            jnp.concatenate([zl, l0_rwkv_a2.astype(F32)], axis=1)], axis=0).astype(BF16)

    cw = l0_ssd_conv_w
    cb = l0_ssd_conv_b
    cwx = _rows([cw[k, :SSD_W] for k in range(SSD_CONV)], 8)
    cwbc = _rows([cw[k, SSD_W:] for k in range(SSD_CONV)], 8)
    vecx = _rows([cb[:SSD_W], l0_ssd_norm_g, jnp.repeat(l0_ssd_d, SSD_P)], 8)
    vecbc = _rows([cb[SSD_W:]], 8)
    hpad = jnp.zeros((128 - SSD_HEADS,), F32)
    vecd = _rows([jnp.concatenate([l0_ssd_dt_bias.astype(F32), hpad]),
                  jnp.concatenate([-jnp.exp(l0_ssd_a_log.astype(F32)), hpad])], 8)
    y_a, y_b = _rwkv_ssd_group(p0, (5, 6, 7, 32, 33), (0, 1, 4, 34), B, L, vec, vec2, w2a2,
                               l0_rwkv_g2.astype(BF16), cwx, cwbc, vecx, vecbc, vecd)

    x2 = _mix_ffn(x2, y_a, y_b, l0_w_out.astype(BF16), L, l0_norm_ffn, l0_ffn_up.astype(BF16),
                  l0_ffn_conv_w.astype(F32), l0_ffn_conv_b.astype(F32), l0_ffn_down.astype(BF16),
                  final_norm, False)

    p1 = _norm_proj(x2, l1_norm_mix, (l1_w_in.astype(BF16),),
                    ((0, 1536, 0, 1024), (0, 2560, 1024, 1024), (0, 0, 2048, 1536)))
    tables = _s5_tables(*(t.astype(F32) for t in (l1_s5_lam_re, l1_s5_lam_im, l1_s5_log_dt, l1_s5_b_re,
                                                   l1_s5_b_im, l1_s5_c_re, l1_s5_c_im)))
    y_c = _s5_group(p1, 4, B, L, tables, l1_s5_d.astype(F32), l1_s5_w_glu.astype(BF16),
                    l1_s5_b_glu.astype(F32))
    y_d = _ret_group(p1, (5, 6, 0, 1), B, L)
    x2 = _mix_ffn(x2, y_c, y_d, l1_w_out.astype(BF16), L, l1_norm_ffn, l1_ffn_up.astype(BF16),
                  l1_ffn_conv_w.astype(F32), l1_ffn_conv_b.astype(F32), l1_ffn_down.astype(BF16),
                  final_norm, True)
    return x2.reshape(B, L, D).astype(x.dtype)
```

```python
import functools
import math

import numpy as np
import jax
import jax.numpy as jnp
from jax import lax
from jax.experimental import pallas as pl
from jax.experimental.pallas import tpu as pltpu

F32 = jnp.float32
BF16 = jnp.bfloat16

EPS = 1e-6
D_MODEL = 1024

RWKV_HEADS = 8
RWKV_N = 64
RWKV_W = 512
RWKV_GN_EPS = 64e-5
RWKV_CHUNK = 64
RWKV_BLOCK_CHUNKS = 4
RWKV_WAVES = 1

SSD_HEADS = 16
SSD_P = 64
SSD_GROUPS = 2
SSD_N = 128
SSD_W = 1024
SSD_CONV = 4
SSD_CHUNK = 64
SSD_BLOCK_CHUNKS = 4

S5_W = 512
S5_GROUPS = 32
S5_H = 16
S5_P = 64
S5_LC = 8
S5_TILES = 4
S5_ROWS = 128

RET_HEADS = 4
RET_DK = 128
RET_DV = 256
RET_CHUNK = 256
RET_BLOCK_CHUNKS = 2

FFN_HIDDEN = 2816
FFN_TN = 256
FFN_HALO = 16

VMEM_LIMIT = 48 * 1024 * 1024


def _dot(a, b):
    return jnp.dot(a, b, preferred_element_type=F32)


def _dot_nt(a, b):
    return lax.dot_general(a, b, (((1,), (1,)), ((), ())), preferred_element_type=F32)


def _dot_tn(a, b):
    return lax.dot_general(a, b, (((0,), (0,)), ((), ())), preferred_element_type=F32)


def _split3(x):
    hi = x.astype(BF16)
    r1 = x - hi.astype(F32)
    mid = r1.astype(BF16)
    lo = (r1 - mid.astype(F32)).astype(BF16)
    return hi, mid, lo


def _dot_sel_rhs(x, sel):
    hi, mid, lo = _split3(x)
    return _dot(hi, sel) + _dot(mid, sel) + _dot(lo, sel)


def _dot_sel_lhs(sel, x):
    hi, mid, lo = _split3(x)
    return _dot(sel, hi) + _dot(sel, mid) + _dot(sel, lo)


def _sigmoid(x):
    return 0.5 + 0.5 * jnp.tanh(0.5 * x)


def _silu(x):
    h = 0.5 * x
    return h + h * jnp.tanh(h)


def _softplus(x):
    return jnp.maximum(x, 0.0) + jnp.log(1.0 + jnp.exp(-jnp.abs(x)))


def _cparams(sem):
    return pltpu.CompilerParams(dimension_semantics=sem, vmem_limit_bytes=VMEM_LIMIT)


def _proj_kernel(x_ref, g_ref, *refs, pieces):
    w_refs, o_ref = refs[:-1], refs[-1]
    x = x_ref[...]
    y = x * lax.rsqrt(jnp.mean(x * x, axis=-1, keepdims=True) + EPS)
    hn = (y * g_ref[...]).astype(BF16)
    for which, src, dst, width in pieces:
        o_ref[:, dst:dst + width] = _dot(hn, w_refs[which][:, src:src + width])


def _norm_proj(x2, g, ws, pieces, tm=512):
    T, D = x2.shape
    N = sum(p[3] for p in pieces)
    return pl.pallas_call(
        functools.partial(_proj_kernel, pieces=pieces),
        grid=(T // tm,),
        in_specs=[pl.BlockSpec((tm, D), lambda i: (i, 0)), _resident((1, D))] + [_resident(w.shape) for w in ws],
        out_specs=pl.BlockSpec((tm, N), lambda i: (i, 0)),
        out_shape=jax.ShapeDtypeStruct((T, N), F32),
        compiler_params=_cparams(("parallel",)),
    )(x2, g.reshape(1, D), *ws)


def _ffn_kernel(x_ref, xh_ref, ya_ref, yah_ref, yb_ref, ybh_ref, wo_ref, g_ref, wup_ref,
                cw_ref, cb_ref, wd_ref, gf_ref, o_ref, act_scr, *, blocks_per_seq, final_norm):
    i = pl.program_id(0)
    Hd = wd_ref.shape[0]
    tn = FFN_TN
    Ka = ya_ref.shape[1]

    def mixed(x, ya, yb):
        return x + _dot(ya.astype(BF16), wo_ref[0:Ka, :]) + _dot(yb.astype(BF16), wo_ref[Ka:, :])

    def norm(x):
        y = x * lax.rsqrt(jnp.mean(x * x, axis=-1, keepdims=True) + EPS)
        return (y * g_ref[...]).astype(BF16)

    x1e = mixed(jnp.concatenate([xh_ref[...], x_ref[...]], axis=0),
                jnp.concatenate([yah_ref[...], ya_ref[...]], axis=0),
                jnp.concatenate([ybh_ref[...], yb_ref[...]], axis=0))
    x1 = x1e[FFN_HALO:, :]
    keep = jnp.where(i % blocks_per_seq == 0, 0.0, 1.0)
    rowi = lax.broadcasted_iota(jnp.int32, (x1e.shape[0], 1), 0)
    hn = norm(x1e * jnp.where(rowi < FFN_HALO, keep, 1.0))

    def conv(u, cols):
        u1 = pltpu.roll(u, 1, 0)
        u2 = pltpu.roll(u, 2, 0)
        c = cw_ref[2:3, cols] * u + cw_ref[1:2, cols] * u1 + cw_ref[0:1, cols] * u2 + cb_ref[:, cols]
        return c[FFN_HALO:, :]

    for j in range(Hd // tn):
        gs = slice(j * tn, (j + 1) * tn)
        vs = slice(Hd + j * tn, Hd + (j + 1) * tn)
        cg = conv(_dot(hn, wup_ref[:, gs]), gs)
        cv = conv(_dot(hn, wup_ref[:, vs]), vs)
        act_scr[:, gs] = (_silu(cg) * cv).astype(BF16)

    y = x1 + _dot(act_scr[...], wd_ref[...])
    if final_norm:
        y = y * lax.rsqrt(jnp.mean(y * y, axis=-1, keepdims=True) + EPS) * gf_ref[...]
    o_ref[...] = y


def _resident(shape):
    return pl.BlockSpec(shape, lambda *_: (0,) * len(shape), pipeline_mode=pl.Buffered(1))


def _mix_ffn(x2, ya, yb, wo, seq_len, g, w_up, conv_w, conv_b, w_down, g_final, final_norm, tm=512):
    T, D = x2.shape
    Hd = w_down.shape[0]
    Ka, Kb = ya.shape[1], yb.shape[1]
    tm = min(tm, seq_len)
    hb = tm // FFN_HALO
    kern = functools.partial(_ffn_kernel, blocks_per_seq=seq_len // tm, final_norm=final_norm)

    def rows(width):
        return pl.BlockSpec((tm, width), lambda i: (i, 0))

    def halo(width):
        return pl.BlockSpec((FFN_HALO, width), lambda i: (jnp.maximum(i * hb - 1, 0), 0))

    return pl.pallas_call(
        kern,
        grid=(T // tm,),
        in_specs=[rows(D), halo(D), rows(Ka), halo(Ka), rows(Kb), halo(Kb),
                  _resident((Ka + Kb, D)),
                  _resident((1, D)), _resident((D, 2 * Hd)), _resident((3, 2 * Hd)), _resident((1, 2 * Hd)),
                  _resident((Hd, D)), _resident((1, D))],
        out_specs=pl.BlockSpec((tm, D), lambda i: (i, 0)),
        out_shape=jax.ShapeDtypeStruct((T, D), F32),
        scratch_shapes=[pltpu.VMEM((tm, Hd), BF16)],
        compiler_params=_cparams(("parallel",)),
    )(x2, x2, ya, ya, yb, yb, wo, g.reshape(1, D), w_up, conv_w, conv_b.reshape(1, -1), w_down,
      g_final.reshape(1, D))


def _rwkv_body(r_ref, k_ref, v_ref, zwa_ref, zg_ref, vec_ref, vec2_ref, w2a2_ref, g2_ref,
               eseg_ref, csum_ref, o_ref, h_scr, prev_scr, prev2_scr, y_scr):
    RB = r_ref.shape[0]
    C = RWKV_CHUNK
    N = RWKV_N

    row = lax.broadcasted_iota(jnp.int32, (RB, 1), 0)

    def shift_lerp(x, prev_row, mu):
        xs = jnp.where(row == 0, prev_row, pltpu.roll(x, 1, 0))
        return x + (xs - x) * mu

    r_raw, k_raw, v_raw = r_ref[...], k_ref[...], v_ref[...]
    zwa_raw, zg_raw = zwa_ref[...], zg_ref[...]
    r = shift_lerp(r_raw, prev_scr[0:1, :], vec_ref[0:1, :])
    k = shift_lerp(k_raw, prev_scr[1:2, :], vec_ref[1:2, :])
    v = shift_lerp(v_raw, prev_scr[2:3, :], vec_ref[2:3, :])
    zwa = shift_lerp(zwa_raw, prev2_scr[0:1, :], vec2_ref[0:1, :])
    zg = shift_lerp(zg_raw, prev2_scr[1:2, :], vec2_ref[1:2, :])
    prev_scr[0:1, :] = r_raw[RB - 1:RB, :]
    prev_scr[1:2, :] = k_raw[RB - 1:RB, :]
    prev_scr[2:3, :] = v_raw[RB - 1:RB, :]
    prev2_scr[0:1, :] = zwa_raw[RB - 1:RB, :]
    prev2_scr[1:2, :] = zg_raw[RB - 1:RB, :]

    w0, a0 = vec_ref[3:4, :], vec_ref[4:5, :]
    k_k, k_a, r_k = vec_ref[5:6, :], vec_ref[6:7, :], vec_ref[7:8, :]
    ln_g, ln_b = vec_ref[8:9, :], vec_ref[9:10, :]

    lane = lax.broadcasted_iota(jnp.int32, (1, 128), 1)
    lora_in = jnp.where(lane < 64, jnp.tanh(zwa), zwa).astype(BF16)
    wa = _dot(lora_in, w2a2_ref[...])
    w_log = -_softplus(-(w0 + wa[:, :RWKV_W])) - 0.5
    logw = -jnp.exp(w_log)
    a = _sigmoid(a0 + wa[:, RWKV_W:])
    g = _dot(_sigmoid(zg).astype(BF16), g2_ref[...])

    eseg = eseg_ref[...]

    def segsum(x, split=True):
        EW = eseg.shape[0]
        hi = x.astype(BF16)
        lo = (x - hi.astype(F32)).astype(BF16) if split else None
        parts = []
        for q in range(RWKV_W // EW):
            qs = slice(q * EW, (q + 1) * EW)
            sq = _dot(hi[:, qs], eseg)
            parts.append(sq + _dot(lo[:, qs], eseg) if split else sq)
        return jnp.concatenate(parts, axis=1)

    yield
    kk = k * k_k
    kk = kk * lax.rsqrt(jnp.maximum(segsum(kk * kk), 1e-24))
    k2 = k * (1.0 + (a - 1.0) * k_a)
    yield

    lg = _dot_sel_lhs(csum_ref[...], logw)
    lg_end = jnp.concatenate([jnp.broadcast_to(lg[(j + 1) * C - 1:(j + 1) * C, :], (C, RWKV_W))
                              for j in range(RB // C)], axis=0)
    eg = jnp.exp(lg)
    egi = jnp.exp(-lg)
    g_end = jnp.exp(lg_end)
    d_end = jnp.exp(lg_end - lg)
    yield
    kka = kk * a
    ab = -kk * jnp.exp(lg - logw)
    rb = r * eg
    bt = kka * egi
    kt = k2 * egi
    bg = kka * d_end
    kg = k2 * d_end
    yield

    W2 = 2 * N
    left = lax.broadcasted_iota(jnp.int32, (1, W2), 1) < N

    def bd(x):
        return jnp.concatenate([jnp.where(left, x, 0.0), jnp.where(left, 0.0, x)], axis=0)

    ri = lax.broadcasted_iota(jnp.int32, (2 * W2, 2 * W2), 0)
    ci = lax.broadcasted_iota(jnp.int32, (2 * W2, 2 * W2), 1)
    mask = (ci & (N - 1)) < (ri & (N - 1)) + jnp.where(ri >= W2, 1, 0)
    e_r = lax.broadcasted_iota(jnp.int32, (W2, W2), 0)
    e_c = lax.broadcasted_iota(jnp.int32, (W2, W2), 1)
    eye = jnp.where(e_r == e_c, 1.0, 0.0)
    zeros_b = jnp.zeros((W2, W2), BF16)

    all_units = [(j, p) for j in range(RB // C) for p in range(RWKV_HEADS // 2)]

    def piece(x, j, p):
        return x[j * C:(j + 1) * C, p * W2:(p + 1) * W2]

    def each(fn, n):
        out = []
        for i in range(n):
            out.append(fn(i))
            yield
        return out

    def diag2(ya, yb):
        z = jnp.zeros_like(ya)
        return jnp.concatenate([jnp.concatenate([ya, z], axis=1), jnp.concatenate([z, yb], axis=1)], axis=0)

    def diag2_of(xx):
        return diag2(xx[:, :W2], xx[:, W2:])

    def wave(units):
        U = len(units)
        ab_u = yield from each(lambda i: bd(piece(ab, *units[i])).astype(BF16), U)
        AR = yield from each(lambda i: jnp.concatenate([ab_u[i], bd(piece(rb, *units[i])).astype(BF16)], axis=0), U)
        BK = yield from each(lambda i: jnp.concatenate([bd(piece(bt, *units[i])), bd(piece(kt, *units[i]))],
                                                       axis=0).astype(BF16), U)
        v_u = yield from each(lambda i: bd(piece(v, *units[i])).astype(BF16), U)
        G = yield from each(lambda i: jnp.where(mask, _dot_nt(AR[i], BK[i]), 0.0), U)
        SU = U // 2
        AA = [jnp.concatenate([G[2 * s][:W2, :W2], G[2 * s + 1][:W2, :W2]], axis=1) for s in range(SU)]
        SS = [jnp.concatenate([eye, eye], axis=1) + AA[s] for s in range(SU)]
        AAb = [AA[s].astype(BF16) for s in range(SU)]
        QQ = yield from each(lambda s: _dot(AAb[s], diag2_of(AAb[s])), SU)
        for _ in range(int(math.log2(C)) - 2):
            QQb = [QQ[s].astype(BF16) for s in range(SU)]
            QS = yield from each(lambda s: _dot(jnp.concatenate([QQb[s], SS[s].astype(BF16)], axis=0),
                                                diag2_of(QQb[s])), SU)
            QQ = [QS[s][:W2] for s in range(SU)]
            SS = [SS[s] + QS[s][W2:] for s in range(SU)]
        SS = yield from each(
            lambda s: (SS[s] + _dot(SS[s].astype(BF16), diag2_of(QQ[s].astype(BF16)))).astype(BF16), SU)
        S = [SS[i // 2][:, (i % 2) * W2:(i % 2 + 1) * W2] for i in range(U)]
        akv2 = yield from each(
            lambda s: _dot(jnp.concatenate([G[2 * s][:W2, W2:], G[2 * s + 1][:W2, W2:]], axis=1).astype(BF16),
                           diag2(v_u[2 * s], v_u[2 * s + 1])), SU)
        akv = [akv2[i // 2][:, (i % 2) * W2:(i % 2 + 1) * W2] for i in range(U)]
        WU = yield from each(
            lambda i: _dot(S[i], jnp.concatenate([ab_u[i], akv[i].astype(BF16)], axis=1)).astype(BF16), U)
        Z = [jnp.concatenate([WU[i], jnp.concatenate([zeros_b, v_u[i]], axis=1)], axis=0) for i in range(U)]
        QY = yield from each(lambda i: _dot(G[i][W2:, :].astype(BF16), Z[i]), U)
        BKgT = yield from each(lambda i: jnp.concatenate([bd(piece(bg, *units[i])).T, bd(piece(kg, *units[i])).T],
                                                         axis=1).astype(BF16), U)
        MN = yield from each(lambda i: _dot(BKgT[i], Z[i]), U)
        QM = [jnp.concatenate([AR[i][W2:].astype(F32) + QY[i][:, :W2], MN[i][:, :W2]], axis=0).astype(BF16)
              for i in range(U)]
        gcol = [jnp.sum(eye * piece(g_end, j, p)[0:1, :], axis=1, keepdims=True) for j, p in units]

        for s in range(SU):
            ia, ib = 2 * s, 2 * s + 1
            Ha, Hb = h_scr[units[ia][1]], h_scr[units[ib][1]]
            YH2 = _dot(jnp.concatenate([QM[ia], QM[ib]], axis=1), diag2(Ha.astype(BF16), Hb.astype(BF16)))
            for i, H, YH in ((ia, Ha, YH2[:, :W2]), (ib, Hb, YH2[:, W2:])):
                j, p = units[i]
                ybd = YH[:W2] + QY[i][:, W2:]
                y_scr[j * C:(j + 1) * C, p * W2:(p + 1) * W2] = ybd[:C] + ybd[C:]
                h_scr[p] = H * gcol[i] + YH[W2:] + MN[i][:, W2:]
            yield

    per_wave = len(all_units) // RWKV_WAVES
    for w in range(RWKV_WAVES):
        yield from wave(all_units[w * per_wave:(w + 1) * per_wave])

    y = y_scr[...]
    inv_n = 1.0 / N
    mean = segsum(y) * inv_n
    d = y - mean
    yield
    var = segsum(d * d, split=False) * inv_n
    yn = d * lax.rsqrt(var + RWKV_GN_EPS) * ln_g + ln_b
    yield
    bonus = segsum(r * k2 * r_k, split=False) * v
    o_ref[...] = ((yn + bonus) * g).astype(o_ref.dtype)


def _ssd_body(z_ref, xs_ref, bc_ref, dt_ref, cwx_ref, cwbc_ref, vecx_ref, vecbc_ref, vecd_ref,
              eh_ref, ltri_ref, o_ref, st_scr, tailx_scr, tailbc_scr, y_scr):
    RB = xs_ref.shape[0]
    Q = SSD_CHUNK
    P, N = SSD_P, SSD_N
    GW = SSD_W // SSD_GROUPS

    def conv_silu(x_ref, tail_scr, cw_ref, b_ref):
        parts = []
        for q in range(x_ref.shape[1] // 256):
            cs = slice(q * 256, (q + 1) * 256)
            x_raw = x_ref[:, cs]
            ext = jnp.concatenate([tail_scr[:, cs], x_raw], axis=0)
            acc = b_ref[0:1, cs] + cw_ref[SSD_CONV - 1:SSD_CONV, cs] * x_raw
            for s in range(1, SSD_CONV):
                acc = acc + cw_ref[SSD_CONV - 1 - s:SSD_CONV - s, cs] * pltpu.roll(ext, s, 0)[8:, :]
            tail_scr[:, cs] = x_raw[RB - 8:, :]
            parts.append(_silu(acc))
            yield
        return jnp.concatenate(parts, axis=1)

    xs = yield from conv_silu(xs_ref, tailx_scr, cwx_ref, vecx_ref)
    bc = yield from conv_silu(bc_ref, tailbc_scr, cwbc_ref, vecbc_ref)

    eh = eh_ref[...]
    dt = _softplus(dt_ref[...] + vecd_ref[0:1, :])
    a = dt * vecd_ref[1:2, :]
    acum = _dot_sel_lhs(ltri_ref[...], a)
    dt_e = _dot_sel_rhs(dt, eh)
    acum_e = _dot_sel_rhs(acum, eh)
    acum_t = acum.T
    yield
    xdt = xs * dt_e
    xdt_b = xdt.astype(BF16)
    ea = jnp.exp(acum_e)
    bc_b = bc.astype(BF16)
    yield

    li = lax.broadcasted_iota(jnp.int32, (Q, Q), 0)
    si = lax.broadcasted_iota(jnp.int32, (Q, Q), 1)
    causal = si <= li

    for ch in range(RB // Q):
        rs = slice(ch * Q, (ch + 1) * Q)
        a_last = acum_e[(ch + 1) * Q - 1:(ch + 1) * Q, :]
        xdte = (xdt[rs] * jnp.exp(a_last - acum_e[rs])).astype(BF16)
        chunk_decay = jnp.exp(a_last)
        for gi in range(SSD_GROUPS):
            gs = slice(gi * GW, (gi + 1) * GW)
            Bg = bc_b[rs, gi * N:(gi + 1) * N]
            Cg = bc_b[rs, SSD_GROUPS * N + gi * N:SSD_GROUPS * N + (gi + 1) * N]
            cb = _dot_nt(Cg, Bg)
            prev = st_scr[gi]
            y_off = _dot(Cg, prev.astype(BF16)) * ea[rs, gs]
            st_scr[gi] = prev * chunk_decay[:, gs] + _dot(bc[rs, gi * N:(gi + 1) * N].T.astype(BF16), xdte[:, gs])
            yield
            ys = []
            for j in range(SSD_HEADS // SSD_GROUPS):
                hh = gi * (SSD_HEADS // SSD_GROUPS) + j
                hs = slice(hh * P, (hh + 1) * P)
                seg = acum_e[rs, hs] - acum_t[hh:hh + 1, rs]
                wm = jnp.where(causal, cb * jnp.exp(seg), 0.0).astype(BF16)
                ys.append(_dot(wm, xdt_b[rs, hs]))
                if j % 2 == 1:
                    yield
            y_scr[rs, gs] = jnp.concatenate(ys, axis=1) + y_off

    for gi in range(SSD_GROUPS):
        gs = slice(gi * GW, (gi + 1) * GW)
        yg = (y_scr[:, gs] + vecx_ref[2:3, gs] * xs[:, gs]) * _silu(z_ref[:, gs])
        yg = yg * lax.rsqrt(jnp.mean(yg * yg, axis=-1, keepdims=True) + EPS)
        o_ref[:, gs] = (yg * vecx_ref[1:2, gs]).astype(o_ref.dtype)
        yield


N_RWKV_IN, N_SSD_IN = 11, 11
N_RWKV_SCR, N_SSD_SCR = 4, 4
RWKV_TURN, SSD_TURN = 3, 1


def _rwkv_ssd_kernel(*refs):
    n_in = N_RWKV_IN + N_SSD_IN
    rwkv_in, ssd_in = refs[:N_RWKV_IN], refs[N_RWKV_IN:n_in]
    o_rwkv, o_ssd = refs[n_in], refs[n_in + 1]
    scr = refs[n_in + 2:]
    rwkv_scr, ssd_scr = scr[:N_RWKV_SCR], scr[N_RWKV_SCR:]

    @pl.when(pl.program_id(1) == 0)
    def _():
        for s in rwkv_scr[:3] + ssd_scr[:3]:
            s[...] = jnp.zeros_like(s)

    _interleave((_rwkv_body(*rwkv_in, o_rwkv, *rwkv_scr), RWKV_TURN), (_ssd_body(*ssd_in, o_ssd, *ssd_scr), SSD_TURN))


def _interleave(*weighted):
    live = list(weighted)
    while live:
        for item in list(live):
            gen, turn = item
            for _ in range(turn):
                try:
                    next(gen)
                except StopIteration:
                    live.remove(item)
                    break


def _rwkv_ssd_group(p0, rwkv_cols, ssd_cols, batch, seq_len, vec, vec2, w2a2, g2, cwx, cwbc, vecx, vecbc, vecd):
    C, Q = RWKV_CHUNK, SSD_CHUNK
    assert Q == SSD_P
    RB = C * RWKV_BLOCK_CHUNKS
    assert RB == Q * SSD_BLOCK_CHUNKS
    nc = seq_len // RB
    T = batch * seq_len
    cr, ck, cv, czwa, czg = rwkv_cols
    cz, cx, cbc, cdt = ssd_cols
    eseg = jnp.asarray(np.arange(256)[:, None] // RWKV_N == np.arange(256)[None, :] // RWKV_N, BF16)
    same_chunk = np.arange(RB)[:, None] // C == np.arange(RB)[None, :] // C
    ltri = jnp.asarray(np.tril(np.ones((RB, RB))) * same_chunk, BF16)
    csum = ltri
    eh = np.zeros((128, SSD_W), np.float32)
    for hh in range(SSD_HEADS):
        eh[hh, hh * SSD_P:(hh + 1) * SSD_P] = 1.0
    eh = jnp.asarray(eh, BF16)

    def col(width, idx):
        return pl.BlockSpec((RB, width), lambda b, c: (b * nc + c, idx))

    def full(arr):
        return pl.BlockSpec(arr.shape, lambda b, c: (0,) * arr.ndim)

    rwkv_in = [col(RWKV_W, cr), col(RWKV_W, ck), col(RWKV_W, cv), col(128, czwa), col(128, czg),
               full(vec), full(vec2), full(w2a2), full(g2), full(eseg), full(csum)]
    ssd_in = [col(SSD_W, cz), col(SSD_W, cx), col(512, cbc), col(128, cdt),
              full(cwx), full(cwbc), full(vecx), full(vecbc), full(vecd), full(eh), full(ltri)]
    assert len(rwkv_in) == N_RWKV_IN and len(ssd_in) == N_SSD_IN
    rwkv_scr = [pltpu.VMEM((RWKV_HEADS // 2, 2 * RWKV_N, 2 * RWKV_N), F32),
                pltpu.VMEM((8, RWKV_W), F32), pltpu.VMEM((8, 128), F32), pltpu.VMEM((RB, RWKV_W), F32)]
    ssd_scr = [pltpu.VMEM((SSD_GROUPS, SSD_N, SSD_W // SSD_GROUPS), F32),
               pltpu.VMEM((8, SSD_W), F32), pltpu.VMEM((8, 512), F32), pltpu.VMEM((RB, SSD_W), F32)]
    return pl.pallas_call(
        _rwkv_ssd_kernel,
        grid=(batch, nc),
        in_specs=rwkv_in + ssd_in,
        out_specs=[pl.BlockSpec((RB, RWKV_W), lambda b, c: (b * nc + c, 0)),
                   pl.BlockSpec((RB, SSD_W), lambda b, c: (b * nc + c, 0))],
        out_shape=[jax.ShapeDtypeStruct((T, RWKV_W), BF16), jax.ShapeDtypeStruct((T, SSD_W), BF16)],
        scratch_shapes=rwkv_scr + ssd_scr,
        compiler_params=_cparams(("arbitrary", "arbitrary")),
    )(p0, p0, p0, p0, p0, vec, vec2, w2a2, g2, eseg, csum,
      p0, p0, p0, p0, cwx, cwbc, vecx, vecbc, vecd, eh, ltri)


def _s5_kernel(u_ref, blk_ref, bzr_ref, bzi_ref, czr_ref, czi_ref, lam_ref, d_ref, wglu_ref, bglu_ref, o_ref,
               st_scr, zre_scr, zim_scr, sre_scr, sim_scr, y_scr, toep_ref, fold_scr):
    i = pl.program_id(1)
    Mb = u_ref.shape[0]
    LC, NT = S5_LC, S5_TILES
    HW = S5_W // NT * (S5_P // S5_H)

    @pl.when(jnp.logical_and(pl.program_id(0) == 0, i == 0))
    def _():
        toep_ref[...] = jnp.zeros_like(toep_ref)
        for n in range(NT):
            for li in range(LC):
                for lo in range(li, LC):
                    toep_ref[n, li * 128:(li + 1) * 128, lo * 128:(lo + 1) * 128] = blk_ref[lo - li, n]

    @pl.when(i == 0)
    def _():
        st_scr[...] = jnp.zeros_like(st_scr)

    u = u_ref[...]
    ucat = []
    for n in range(NT):
        un = jnp.concatenate([u[:, l * S5_W + n * 128:l * S5_W + (n + 1) * 128] for l in range(LC)],
                             axis=1).astype(BF16)
        ucat.append(un)
        zre_scr[:, n * HW:(n + 1) * HW] = _dot(un, bzr_ref[n])
        zim_scr[:, n * HW:(n + 1) * HW] = _dot(un, bzi_ref[n])

    lr = lam_ref[0:1, :]
    lim = lam_ref[1:2, :]

    def step(m, carry):
        sre, sim = carry
        sre_scr[pl.ds(m, 1), :] = sre
        sim_scr[pl.ds(m, 1), :] = sim
        zr = zre_scr[pl.ds(m, 1), :]
        zi = zim_scr[pl.ds(m, 1), :]
        return lr * sre - lim * sim + zr, lr * sim + lim * sre + zi

    sre, sim = lax.fori_loop(0, Mb, step, (st_scr[0:1, :], st_scr[1:2, :]))
    st_scr[0:1, :] = sre
    st_scr[1:2, :] = sim

    for n in range(NT):
        s_re = sre_scr[:, n * HW:(n + 1) * HW].astype(BF16)
        s_im = sim_scr[:, n * HW:(n + 1) * HW].astype(BF16)
        yn = _dot(ucat[n], toep_ref[n]) + _dot(s_re, czr_ref[n]) + _dot(s_im, czi_ref[n])
        for l in range(LC):
            y_scr[:, l * S5_W + n * 128:l * S5_W + (n + 1) * 128] = yn[:, l * 128:(l + 1) * 128]

    y = y_scr[...] + d_ref[...] * u
    y = 0.5 * y * (1.0 + jnp.tanh(math.sqrt(2.0 / math.pi) * (y + 0.044715 * (y * y * y))))
    wglu = wglu_ref[...]
    for l in range(LC):
        yl = y[:, l * S5_W:(l + 1) * S5_W]
        gate = _sigmoid(_dot(yl.astype(BF16), wglu) + bglu_ref[...])
        out_l = yl * gate
        for n in range(NT):
            fold_scr[n, pl.ds(l, Mb, stride=LC), :] = out_l[:, n * 128:(n + 1) * 128]
    for n in range(NT):
        o_ref[:, n * 128:(n + 1) * 128] = fold_scr[n]


def _s5_tables(lam_re, lam_im, log_dt, b_re, b_im, c_re, c_im):
    LC, NT, G, H, Pn = S5_LC, S5_TILES, S5_GROUPS, S5_H, S5_P
    GL = G // NT
    dt = jnp.exp(log_dt)[:, None]
    ar, ai = lam_re * dt, lam_im * dt
    jj = jnp.arange(LC + 1, dtype=F32)[:, None, None]
    mag = jnp.exp(jj * ar[None])
    pw_re, pw_im = mag * jnp.cos(jj * ai[None]), mag * jnp.sin(jj * ai[None])
    nr, ni = pw_re[1] - 1.0, pw_im[1]
    den = lam_re * lam_re + lam_im * lam_im
    qr, qi = (nr * lam_re + ni * lam_im) / den, (ni * lam_re - nr * lam_im) / den
    bb_re = qr[..., None] * b_re - qi[..., None] * b_im
    bb_im = qr[..., None] * b_im + qi[..., None] * b_re
    cl_re = c_re[None] * pw_re[:, :, None, :] - c_im[None] * pw_im[:, :, None, :]
    cl_im = c_re[None] * pw_im[:, :, None, :] + c_im[None] * pw_re[:, :, None, :]
    bt_re, bt_im = jnp.swapaxes(bb_re, 1, 2), jnp.swapaxes(bb_im, 1, 2)
    kd = jnp.sum(cl_re[:LC, :, :, None, :] * bt_re[None, :, None, :, :]
                 - cl_im[:LC, :, :, None, :] * bt_im[None, :, None, :, :], axis=-1)
    def spread(x, rows_per_group, cols_per_group):
        sel = np.tile(np.eye(cols_per_group, dtype=np.float32), (1, GL))
        keep = (np.arange(GL * rows_per_group)[:, None] // rows_per_group
                == np.arange(GL * cols_per_group)[None, :] // cols_per_group)
        return jnp.where(keep, jnp.einsum("...rc,cd->...rd", x, jnp.asarray(sel)), 0.0)

    blk = spread(jnp.swapaxes(kd, -1, -2).reshape(LC, NT, GL * H, H), H, H).astype(BF16)
    def cz_part(cl):
        x = spread(jnp.swapaxes(cl[1:], -1, -2).reshape(LC, NT, GL * Pn, H), Pn, H).astype(BF16)
        return jnp.concatenate([x[l] for l in range(LC)], axis=-1)
    cz = (cz_part(cl_re), cz_part(-cl_im))
    jr = (LC - 1.0) - jnp.arange(LC, dtype=F32)[:, None, None]
    mag_r = jnp.exp(jr * ar[None])
    rev_re, rev_im = mag_r * jnp.cos(jr * ai[None]), mag_r * jnp.sin(jr * ai[None])
    bl_re = rev_re[..., None] * bb_re[None] - rev_im[..., None] * bb_im[None]
    bl_im = rev_re[..., None] * bb_im[None] + rev_im[..., None] * bb_re[None]

    def bz_part(bl):
        x = spread(jnp.swapaxes(bl, -1, -2).reshape(LC, NT, GL * H, Pn), H, Pn).astype(BF16)
        return jnp.swapaxes(x, 0, 1).reshape(NT, LC * GL * H, GL * Pn)
    bz = (bz_part(bl_re), bz_part(bl_im))
    lam = _rows([pw_re[LC].reshape(-1), pw_im[LC].reshape(-1)], 8)
    return blk, bz, cz, lam


def _s5_group(p1, col_u, batch, seq_len, tables, d_skip, w_glu, b_glu):
    LC = S5_LC
    T = batch * seq_len
    blk, bz, cz, lam = tables
    rows_per_seq = seq_len // LC
    Mb = min(S5_ROWS, rows_per_seq)
    nb = rows_per_seq // Mb
    NPc = p1.shape[1]
    W = LC * S5_W
    d_t = jnp.tile(d_skip.reshape(1, S5_W), (1, LC))
    SW = S5_TILES * (S5_GROUPS // S5_TILES) * S5_P

    def full(arr):
        return pl.BlockSpec(arr.shape, lambda b, i: (0,) * arr.ndim)

    def kern(u_ref, *rest):
        ucat_scr, fold_scr = rest[-2], rest[-1]
        for n in range(S5_TILES):
            fold_scr[n] = u_ref[:, n * 128:(n + 1) * 128]
        for n in range(S5_TILES):
            for l in range(LC):
                ucat_scr[:, l * S5_W + n * 128:l * S5_W + (n + 1) * 128] = fold_scr[n, pl.ds(l, Mb, stride=LC), :]
        _s5_kernel(ucat_scr, *rest[:-2], fold_scr)

    return pl.pallas_call(
        kern,
        grid=(batch, nb),
        in_specs=[pl.BlockSpec((Mb * LC, S5_W), lambda b, i: (b * nb + i, col_u)),
                  _resident(blk.shape), _resident(bz[0].shape), _resident(bz[1].shape),
                  _resident(cz[0].shape), _resident(cz[1].shape), full(lam), full(d_t),
                  full(w_glu),
                  pl.BlockSpec((1, S5_W), lambda b, i: (0, 0))],
        out_specs=pl.BlockSpec((Mb * LC, S5_W), lambda b, i: (b * nb + i, 0)),
        out_shape=jax.ShapeDtypeStruct((T, S5_W), F32),
        scratch_shapes=[pltpu.VMEM((8, SW), F32),
                        pltpu.VMEM((Mb, SW), F32), pltpu.VMEM((Mb, SW), F32),
                        pltpu.VMEM((Mb, SW), F32), pltpu.VMEM((Mb, SW), F32),
                        pltpu.VMEM((Mb, W), F32),
                        pltpu.VMEM((S5_TILES, LC * 128, LC * 128), BF16),
                        pltpu.VMEM((Mb, W), F32),
                        pltpu.VMEM((S5_TILES, Mb * LC, 128), F32)],
        compiler_params=_cparams(("arbitrary", "arbitrary")),
    )(p1, blk, *bz, *cz, lam, d_t, w_glu, b_glu.reshape(1, S5_W))


def _ret_kernel(q_ref, k_ref, v_ref, g_ref, cosl_ref, sinl_ref, base_ref, intra_ref, kdec_ref, qdec_ref,
                o_ref, st_scr, *, chunk_decay):
    c = pl.program_id(1)
    DK, DV = RET_DK, RET_DV

    @pl.when(c == 0)
    def _():
        st_scr[...] = jnp.zeros_like(st_scr)

    Q = intra_ref.shape[1]
    scale = DK ** -0.5
    cb, sb = base_ref[0, 0:1, :], base_ref[0, 1:2, :]
    sign = jnp.where(lax.broadcasted_iota(jnp.int32, (1, DK), 1) < DK // 2, -1.0, 1.0)
    for ch in range(q_ref.shape[0] // Q):
        rs = slice(ch * Q, (ch + 1) * Q)
        cl, sl = cosl_ref[rs, :], sinl_ref[rs, :]
        cosf = cb * cl - sb * sl
        sinf = sign * (sb * cl + cb * sl)
        for h in range(RET_HEADS):
            ks = slice(h * DK, (h + 1) * DK)
            vs = slice(h * DV, (h + 1) * DV)
            qh = q_ref[rs, ks]
            kh = k_ref[rs, ks]
            qh = qh * cosf + pltpu.roll(qh, DK // 2, 1) * sinf
            kh = (kh * cosf + pltpu.roll(kh, DK // 2, 1) * sinf) * scale
            vh = v_ref[rs, vs].astype(BF16)
            scores = (_dot_nt(qh.astype(BF16), kh.astype(BF16)) * intra_ref[h]).astype(BF16)
            prev = st_scr[h]
            y = _dot(scores, vh) + _dot((qh * qdec_ref[:, ks]).astype(BF16), prev.astype(BF16))
            kd = (kh * kdec_ref[:, ks]).T.astype(BF16)
            st_scr[h] = prev * chunk_decay[h] + _dot(kd, vh)
            y = y * lax.rsqrt(jnp.mean(y * y, axis=-1, keepdims=True) + EPS)
            o_ref[rs, vs] = (_silu(g_ref[rs, vs]) * y).astype(o_ref.dtype)


def _ret_group(p1, cols, batch, seq_len):
    Q = min(RET_CHUNK, seq_len)
    RB = min(Q * RET_BLOCK_CHUNKS, seq_len)
    nc = seq_len // RB
    T = batch * seq_len
    cq, ck, cv, cg = cols
    H, DK, DV = RET_HEADS, RET_DK, RET_DV
    half = DK // 2
    theta = 1.0 / (10000.0 ** jnp.linspace(0.0, 1.0, half, dtype=F32))
    theta2 = jnp.concatenate([theta, theta])[None, :]
    ang_l = jnp.arange(RB).astype(F32)[:, None] * theta2
    cos_l, sin_l = jnp.cos(ang_l), jnp.sin(ang_l)
    ang_b = (jnp.arange(nc) * RB).astype(F32)[:, None] * theta2
    base = jnp.concatenate([jnp.cos(ang_b)[:, None, :], jnp.sin(ang_b)[:, None, :],
                            jnp.zeros((nc, 6, DK), F32)], axis=1)
    log_gamma = jnp.log(1.0 - 2.0 ** (-5.0 - jnp.arange(H, dtype=F32)))
    t = jnp.arange(Q, dtype=F32)
    diff = t[:, None] - t[None, :]
    intra = jnp.where((diff >= 0)[None], jnp.exp(jnp.maximum(diff, 0.0)[None] * log_gamma[:, None, None]), 0.0)
    kdec = jnp.repeat(jnp.exp((Q - 1.0 - t)[:, None] * log_gamma[None, :]), DK, axis=1)
    qdec = jnp.repeat(jnp.exp((t + 1.0)[:, None] * log_gamma[None, :]), DK, axis=1)
    chunk_decay = tuple(float((1.0 - 2.0 ** (-5.0 - h)) ** Q) for h in range(H))

    def col(width, idx):
        return pl.BlockSpec((RB, width), lambda b, c: (b * nc + c, idx))

    def full(arr):
        return pl.BlockSpec(arr.shape, lambda b, c: (0,) * arr.ndim)

    return pl.pallas_call(
        functools.partial(_ret_kernel, chunk_decay=chunk_decay),
        grid=(batch, nc),
        in_specs=[col(H * DK, cq), col(H * DK, ck), col(H * DV, cv), col(H * DV, cg),
                  full(cos_l), full(sin_l), pl.BlockSpec((1, 8, DK), lambda b, c: (c, 0, 0)),
                  full(intra), full(kdec), full(qdec)],
        out_specs=pl.BlockSpec((RB, H * DV), lambda b, c: (b * nc + c, 0)),
        out_shape=jax.ShapeDtypeStruct((T, H * DV), BF16),
        scratch_shapes=[pltpu.VMEM((H, DK, DV), F32)],
        compiler_params=_cparams(("arbitrary", "arbitrary")),
    )(p1, p1, p1, p1, cos_l, sin_l, base, intra, kdec, qdec)


def _pad_cols(w, n):
    return jnp.pad(w, ((0, 0), (0, n - w.shape[1])))


def _rows(vectors, n_rows):
    width = vectors[0].shape[0]
    pad = [jnp.zeros((n_rows - len(vectors), width), F32)] if n_rows > len(vectors) else []
    return jnp.concatenate([v.astype(F32).reshape(1, width) for v in vectors] + pad, axis=0)


def kernel(x, l0_norm_mix, l0_w_in, l0_rwkv_mu, l0_rwkv_w0, l0_rwkv_w2, l0_rwkv_a0, l0_rwkv_a2, l0_rwkv_g2, l0_rwkv_k_k, l0_rwkv_k_a, l0_rwkv_r_k, l0_rwkv_ln_g, l0_rwkv_ln_b, l0_ssd_conv_w, l0_ssd_conv_b, l0_ssd_dt_bias, l0_ssd_a_log, l0_ssd_d, l0_ssd_norm_g, l0_w_out, l0_norm_ffn, l0_ffn_up, l0_ffn_conv_w, l0_ffn_conv_b, l0_ffn_down, l1_norm_mix, l1_w_in, l1_s5_lam_re, l1_s5_lam_im, l1_s5_log_dt, l1_s5_b_re, l1_s5_b_im, l1_s5_c_re, l1_s5_c_im, l1_s5_d, l1_s5_w_glu, l1_s5_b_glu, l1_w_out, l1_norm_ffn, l1_ffn_up, l1_ffn_conv_w, l1_ffn_conv_b, l1_ffn_down, final_norm):
    B, L, D = x.shape
    T = B * L
    x2 = x.reshape(T, D).astype(F32)

    RP = 3 * RWKV_W + 256
    o_z = RP
    o_xbc = o_z + SSD_W
    o_dt = o_xbc + SSD_W + 2 * SSD_GROUPS * SSD_N
    n_xbc = o_dt - o_xbc
    w0 = l0_w_in.astype(BF16)
    w_dt = _pad_cols(w0[:, o_dt:], 128)
    p0 = _norm_proj(x2, l0_norm_mix, (w0, w_dt), ((0, o_z, 0, SSD_W), (0, o_xbc, SSD_W, n_xbc),
                                                  (0, 0, SSD_W + n_xbc, RP), (1, 0, SSD_W + n_xbc + RP, 128)))

    mu = l0_rwkv_mu
    vec = _rows([mu[0:512], mu[512:1024], mu[1024:1536], l0_rwkv_w0, l0_rwkv_a0, l0_rwkv_k_k, l0_rwkv_k_a,
                 l0_rwkv_r_k.reshape(-1), l0_rwkv_ln_g, l0_rwkv_ln_b], 16)
    vec2 = _rows([mu[1536:1664], mu[1664:1792]], 8)
    zl = jnp.zeros((64, RWKV_W), F32)
    w2a2 = jnp.concatenate([jnp.concatenate([l0_rwkv_w2.astype(F32), zl], axis=1),
                            jnp.concatenate([zl, l0_rwkv_a2.astype(F32)], axis=1)], axis=0).astype(BF16)

    cw = l0_ssd_conv_w
    cb = l0_ssd_conv_b
    cwx = _rows([cw[k, :SSD_W] for k in range(SSD_CONV)], 8)
    cwbc = _rows([cw[k, SSD_W:] for k in range(SSD_CONV)], 8)
    vecx = _rows([cb[:SSD_W], l0_ssd_norm_g, jnp.repeat(l0_ssd_d, SSD_P)], 8)
    vecbc = _rows([cb[SSD_W:]], 8)
    hpad = jnp.zeros((128 - SSD_HEADS,), F32)
    vecd = _rows([jnp.concatenate([l0_ssd_dt_bias.astype(F32), hpad]),
                  jnp.concatenate([-jnp.exp(l0_ssd_a_log.astype(F32)), hpad])], 8)
    y_a, y_b = _rwkv_ssd_group(p0, (5, 6, 7, 32, 33), (0, 1, 4, 34), B, L, vec, vec2, w2a2,
                               l0_rwkv_g2.astype(BF16), cwx, cwbc, vecx, vecbc, vecd)

    x2 = _mix_ffn(x2, y_a, y_b, l0_w_out.astype(BF16), L, l0_norm_ffn, l0_ffn_up.astype(BF16),
                  l0_ffn_conv_w.astype(F32), l0_ffn_conv_b.astype(F32), l0_ffn_down.astype(BF16),
                  final_norm, False)

    p1 = _norm_proj(x2, l1_norm_mix, (l1_w_in.astype(BF16),),
                    ((0, 1536, 0, 1024), (0, 2560, 1024, 1024), (0, 0, 2048, 1536)))
    tables = _s5_tables(*(t.astype(F32) for t in (l1_s5_lam_re, l1_s5_lam_im, l1_s5_log_dt, l1_s5_b_re,
                                                   l1_s5_b_im, l1_s5_c_re, l1_s5_c_im)))
    y_c = _s5_group(p1, 4, B, L, tables, l1_s5_d.astype(F32), l1_s5_w_glu.astype(BF16),
                    l1_s5_b_glu.astype(F32))
    y_d = _ret_group(p1, (5, 6, 0, 1), B, L)
    x2 = _mix_ffn(x2, y_c, y_d, l1_w_out.astype(BF16), L, l1_norm_ffn, l1_ffn_up.astype(BF16),
                  l1_ffn_conv_w.astype(F32), l1_ffn_conv_b.astype(F32), l1_ffn_down.astype(BF16),
                  final_norm, True)
    return x2.reshape(B, L, D).astype(x.dtype)
```
